```python
import math
import jax
import jax.numpy as jnp
from jax import lax
import numpy as np

D_MODEL = 2048
BATCH = 1
SEQ = 16384
DEPTH = 2

GRID_W = 64
CTX_LEN = 256
ROPE_BASE = 10000.0
LN_EPS = 1e-6
DEEPNORM_ALPHA = (2 * DEPTH) ** 0.25
DEEPNORM_BETA = (8 * DEPTH) ** -0.25
N_EVEN = (DEPTH + 1) // 2
N_ODD = DEPTH // 2

RET_HEADS = 4
RET_QK_DIM = 256
RET_V_DIM = 256
RET_CHUNK = 128
S5_CHANNELS = D_MODEL // 2
S5_GROUP = 16
S5_GROUPS = S5_CHANNELS // S5_GROUP
S5_STATE = 64
S5_CHUNK = 128
EVEN_IN = 2 * RET_HEADS * RET_QK_DIM + 2 * RET_HEADS * RET_V_DIM + S5_CHANNELS

HEAD_DIM = 128
GQA_HEADS = 8
GQA_KV_HEADS = 2
GQA_GROUP = GQA_HEADS // GQA_KV_HEADS
DIFF_HEADS = 4
DIFF_V_DIM = 2 * HEAD_DIM
ATT_BLOCK = 128
ODD_IN = GQA_HEADS * HEAD_DIM + 2 * GQA_KV_HEADS * HEAD_DIM + 2 * DIFF_HEADS * 2 * HEAD_DIM + DIFF_HEADS * DIFF_V_DIM

N_EXPERTS = 64
TOP_K = 8
N_EXPERT_GROUPS = 8
TOPK_GROUPS = 4
EXPERT_HIDDEN = 512
SHARED_HIDDEN = 512
ROUTED_SCALE = 2.5
MOE_BLOCK = 128

kernel_name = 'hybrid_retention_s5_gqa_diffattn_moe_prefix_dit'


def layer_norm(x, g, b):
    xf = x.astype(jnp.float32)
    mu = jnp.mean(xf, -1, keepdims=True)
    var = jnp.mean(jnp.square(xf - mu), -1, keepdims=True)
    return ((xf - mu) * lax.rsqrt(var + LN_EPS)).astype(x.dtype) * g + b


def rms_norm(x, g):
    xf = x.astype(jnp.float32)
    y = xf * lax.rsqrt(jnp.mean(jnp.square(xf), -1, keepdims=True) + LN_EPS)
    return y.astype(x.dtype) * g


def head_group_norm(o):
    of = o.astype(jnp.float32)
    mu = jnp.mean(of, -1, keepdims=True)
    var = jnp.mean(jnp.square(of - mu), -1, keepdims=True)
    return ((of - mu) * lax.rsqrt(var + LN_EPS)).astype(o.dtype)


def rope_1d(x, pos):
    half = x.shape[-1] // 2
    freqs = ROPE_BASE ** (-jnp.arange(half, dtype=jnp.float32) / half)
    ang = pos.astype(jnp.float32)[:, None] * freqs[None, :]
    cos = jnp.cos(ang).astype(x.dtype)
    sin = jnp.sin(ang).astype(x.dtype)
    x1, x2 = x[..., :half], x[..., half:]
    return jnp.concatenate([x1 * cos - x2 * sin, x1 * sin + x2 * cos], axis=-1)


def rope_2d(x, row, col):
    h = x.shape[-1] // 2
    return jnp.concatenate([rope_1d(x[..., :h], row), rope_1d(x[..., h:], col)], axis=-1)


def to_heads(t, n_heads):
    b_, L, _ = t.shape
    return t.reshape(b_, L, n_heads, -1).transpose(0, 2, 1, 3)


def from_heads(o):
    b_, h, L, d = o.shape
    return o.transpose(0, 2, 1, 3).reshape(b_, L, h * d)


def modulation(v, w, b):
    m = (jax.nn.silu(v) @ w + b)[:, None, :]
    return jnp.split(m, 6, axis=-1)


def retention_chunkwise(q, k, v, log_g, s0):
    b_, h, L, dk = q.shape
    dv = v.shape[-1]
    n = L // RET_CHUNK
    dt = q.dtype
    lg = log_g.astype(jnp.float32)
    idx = jnp.arange(RET_CHUNK, dtype=jnp.float32)
    diff = idx[:, None] - idx[None, :]
    intra = jnp.where(diff[None] >= 0, jnp.exp(jnp.maximum(diff, 0.0)[None] * lg[:, None, None]), 0.0).astype(dt)
    k_dec = jnp.exp((RET_CHUNK - 1 - idx)[None, :] * lg[:, None]).astype(dt)
    q_dec = jnp.exp((idx + 1)[None, :] * lg[:, None]).astype(dt)
    g_chunk = jnp.exp(RET_CHUNK * lg).astype(dt)
    qc = q.reshape(b_, h, n, RET_CHUNK, dk)
    kc = k.reshape(b_, h, n, RET_CHUNK, dk)
    vc = v.reshape(b_, h, n, RET_CHUNK, dv)
    scores = jnp.einsum('bhnid,bhnjd->bhnij', qc, kc) * intra[None, :, None]
    inner = jnp.einsum('bhnij,bhnje->bhnie', scores, vc)
    chunk_kv = jnp.einsum('bhnjd,bhnje->nbhde', kc * k_dec[None, :, None, :, None], vc)

    def step(s, kv):
        return g_chunk[None, :, None, None] * s + kv, s

    s_final, s_prev = lax.scan(step, s0, chunk_kv)
    cross = jnp.einsum('bhnid,nbhde->bhnie', qc * q_dec[None, :, None, :, None], s_prev)
    return (inner + cross).reshape(b_, h, L, dv), s_final


def bidir_retention(qc, kc, vc, ql, kl, vl, log_decay):
    b_, h, _, dk = qc.shape
    dv = vc.shape[-1]
    s0 = jnp.zeros((b_, h, dk, dv), vc.dtype)
    flip = lambda t: jnp.flip(t, axis=2)
    oc_f, sc_f = retention_chunkwise(qc, kc, vc, log_decay[0], s0)
    ol_f, _ = retention_chunkwise(ql, kl, vl, log_decay[0], sc_f)
    oc_b, sc_b = retention_chunkwise(flip(qc), flip(kc), flip(vc), log_decay[1], s0)
    ol_b, _ = retention_chunkwise(flip(ql), flip(kl), flip(vl), log_decay[1], sc_b)
    return oc_f + flip(oc_b), ol_f + flip(ol_b)


def s5_discretize(a_re, a_im, log_dt, b_re, b_im):
    dt = jnp.exp(log_dt)[:, None]
    zr, zi = a_re * dt, a_im * dt
    mag = jnp.exp(zr)
    ab_re, ab_im = mag * jnp.cos(zi), mag * jnp.sin(zi)
    den = jnp.square(a_re) + jnp.square(a_im)
    nr, ni = ab_re - 1.0, ab_im
    f_re = (nr * a_re + ni * a_im) / den
    f_im = (ni * a_re - nr * a_im) / den
    bb_re = f_re[..., None] * b_re - f_im[..., None] * b_im
    bb_im = f_re[..., None] * b_im + f_im[..., None] * b_re
    return ab_re, ab_im, bb_re, bb_im


def s5_combine(e1, e2):
    a1r, a1i, b1r, b1i = e1
    a2r, a2i, b2r, b2i = e2
    return (a2r * a1r - a2i * a1i, a2r * a1i + a2i * a1r,
            a2r * b1r - a2i * b1i + b2r, a2r * b1i + a2i * b1r + b2i)


def s5_scan(u, ab_re, ab_im, bb_re, bb_im, c_re, c_im, h0_re, h0_im):
    b_, L = u.shape[:2]
    n = L // S5_CHUNK
    uc = jnp.swapaxes(u.reshape(b_, n, S5_CHUNK, S5_GROUPS, S5_GROUP), 0, 1)

    def chunk_step(carry, u_blk):
        h_re, h_im = carry
        bu_re = jnp.einsum('bcgs,gps->bcgp', u_blk, bb_re)
        bu_im = jnp.einsum('bcgs,gps->bcgp', u_blk, bb_im)
        bu_re = bu_re.at[:, 0].add(ab_re * h_re - ab_im * h_im)
        bu_im = bu_im.at[:, 0].add(ab_re * h_im + ab_im * h_re)
        a_r = jnp.broadcast_to(ab_re, bu_re.shape)
        a_i = jnp.broadcast_to(ab_im, bu_im.shape)
        _, _, xr, xi = lax.associative_scan(s5_combine, (a_r, a_i, bu_re, bu_im), axis=1)
        y = jnp.einsum('bcgp,gsp->bcgs', xr, c_re) - jnp.einsum('bcgp,gsp->bcgs', xi, c_im)
        return (xr[:, -1], xi[:, -1]), y

    (hr, hi), y = lax.scan(chunk_step, (h0_re, h0_im), uc)
    return jnp.swapaxes(y, 0, 1).reshape(b_, L, S5_GROUPS, S5_GROUP), hr, hi


def bidir_s5(uc, ul, a_re, a_im, log_dt, b_re, b_im, c_re, c_im, d_skip):
    f32 = jnp.float32
    dt = ul.dtype
    b_ = ul.shape[0]
    u_c = uc.astype(f32).reshape(b_, -1, S5_GROUPS, S5_GROUP)
    u_l = ul.astype(f32).reshape(b_, -1, S5_GROUPS, S5_GROUP)
    cr, ci = c_re.astype(f32), c_im.astype(f32)
    dsk = d_skip.astype(f32).reshape(S5_GROUPS, S5_GROUP)
    h0 = jnp.zeros((b_, S5_GROUPS, S5_STATE), f32)
    yc = u_c * dsk
    yl = u_l * dsk
    for direction in range(2):
        abr, abi, bbr, bbi = s5_discretize(a_re[direction].astype(f32), a_im[direction].astype(f32),
                                           log_dt[direction].astype(f32), b_re.astype(f32), b_im.astype(f32))
        sc = u_c if direction == 0 else jnp.flip(u_c, 1)
        sl = u_l if direction == 0 else jnp.flip(u_l, 1)
        y_c, hr, hi = s5_scan(sc, abr, abi, bbr, bbi, cr, ci, h0, h0)
        y_l, _, _ = s5_scan(sl, abr, abi, bbr, bbi, cr, ci, hr, hi)
        if direction == 1:
            y_c, y_l = jnp.flip(y_c, 1), jnp.flip(y_l, 1)
        yc = yc + y_c
        yl = yl + y_l
    return yc.reshape(b_, -1, S5_CHANNELS).astype(dt), yl.reshape(b_, -1, S5_CHANNELS).astype(dt)


def even_mixer(xc, xl, row, col, w_in, w_out, ret_log_decay, s5_a_re, s5_a_im, s5_log_dt,
               s5_b_re, s5_b_im, s5_c_re, s5_c_im, s5_d, s5_w_glu, s5_b_glu, need_ctx):
    hq = RET_HEADS * RET_QK_DIM
    hv = RET_HEADS * RET_V_DIM
    cuts = [hq, 2 * hq, 2 * hq + hv, 2 * hq + 2 * hv]
    qc, kc, vc, gc, uc = jnp.split(xc @ w_in, cuts, axis=-1)
    ql, kl, vl, gl, ul = jnp.split(xl @ w_in, cuts, axis=-1)
    scale = RET_QK_DIM ** -0.5
    qc_h = to_heads(qc, RET_HEADS) * scale
    kc_h = to_heads(kc, RET_HEADS)
    ql_h = rope_2d(to_heads(ql, RET_HEADS), row, col) * scale
    kl_h = rope_2d(to_heads(kl, RET_HEADS), row, col)
    rc, rl = bidir_retention(qc_h, kc_h, to_heads(vc, RET_HEADS), ql_h, kl_h, to_heads(vl, RET_HEADS), ret_log_decay)
    sc, sl = bidir_s5(uc, ul, s5_a_re, s5_a_im, s5_log_dt, s5_b_re, s5_b_im, s5_c_re, s5_c_im, s5_d)

    def finish(r, g, s):
        ret = from_heads(head_group_norm(r)) * jax.nn.silu(g)
        z = jax.nn.gelu(s)
        s5o = z * jax.nn.sigmoid(z @ s5_w_glu + s5_b_glu)
        return jnp.concatenate([ret, s5o], axis=-1) @ w_out

    out_l = finish(rl, gl, sl)
    out_c = finish(rc, gc, sc) if need_ctx else None
    return out_c, out_l


def gqa_attend(q, k, v):
    s = jnp.einsum('bkgqd,bksd->bkgqs', q, k) * (HEAD_DIM ** -0.5)
    p = jax.nn.softmax(s.astype(jnp.float32), axis=-1).astype(v.dtype)
    return jnp.einsum('bkgqs,bksd->bkgqd', p, v)


def diff_attend(q, k, v, lam):
    s = jnp.einsum('bhmqd,bhmsd->bhmqs', q, k) * (HEAD_DIM ** -0.5)
    p = jax.nn.softmax(s.astype(jnp.float32), axis=-1)
    w = (p[:, :, 0] - lam * p[:, :, 1]).astype(v.dtype)
    return jnp.einsum('bhqs,bhsd->bhqd', w, v)


def odd_mixer(xc, xl, row, col, w_in, w_out, qk_norm_g, diff_lambda, diff_subln_g, lambda_init, need_ctx):
    gq_w = GQA_HEADS * HEAD_DIM
    gk_w = GQA_KV_HEADS * HEAD_DIM
    dq_w = DIFF_HEADS * 2 * HEAD_DIM
    cuts = [gq_w, gq_w + gk_w, gq_w + 2 * gk_w, gq_w + 2 * gk_w + dq_w, gq_w + 2 * gk_w + 2 * dq_w]

    def diff_heads(t):
        b_, L, _ = t.shape
        return t.reshape(b_, L, DIFF_HEADS, 2, HEAD_DIM).transpose(0, 2, 3, 1, 4)

    def project(x, rotate):
        gq, gk, gv, dq, dk, dv = jnp.split(x @ w_in, cuts, axis=-1)
        gq = rms_norm(to_heads(gq, GQA_HEADS), qk_norm_g[0])
        gk = rms_norm(to_heads(gk, GQA_KV_HEADS), qk_norm_g[1])
        dq = diff_heads(dq)
        dk = diff_heads(dk)
        if rotate:
            gq, gk, dq, dk = rope_2d(gq, row, col), rope_2d(gk, row, col), rope_2d(dq, row, col), rope_2d(dk, row, col)
        b_, _, L, _ = gq.shape
        gq = gq.reshape(b_, GQA_KV_HEADS, GQA_GROUP, L, HEAD_DIM)
        return gq, gk, to_heads(gv, GQA_KV_HEADS), dq, dk, to_heads(dv, DIFF_HEADS)

    lf = diff_lambda.astype(jnp.float32)
    lam = jnp.exp(jnp.sum(lf[0] * lf[1])) - jnp.exp(jnp.sum(lf[2] * lf[3])) + lambda_init

    cq_g, ck_g, cv_g, cq_d, ck_d, cv_d = project(xc, False)
    lq_g, lk_g, lv_g, lq_d, lk_d, lv_d = project(xl, True)
    kg_all = jnp.concatenate([ck_g, lk_g], axis=2)
    vg_all = jnp.concatenate([cv_g, lv_g], axis=2)
    kd_all = jnp.concatenate([ck_d, lk_d], axis=3)
    vd_all = jnp.concatenate([cv_d, lv_d], axis=2)

    def merge(og, od):
        b_, L = og.shape[0], og.shape[3]
        og = og.reshape(b_, GQA_HEADS, L, HEAD_DIM)
        od = rms_norm(od, diff_subln_g) * (1.0 - lambda_init)
        return jnp.concatenate([from_heads(og), from_heads(od)], axis=-1) @ w_out

    b_, L = xl.shape[:2]
    nb = L // ATT_BLOCK
    qg_b = jnp.moveaxis(lq_g.reshape(b_, GQA_KV_HEADS, GQA_GROUP, nb, ATT_BLOCK, HEAD_DIM), 3, 0)
    qd_b = jnp.moveaxis(lq_d.reshape(b_, DIFF_HEADS, 2, nb, ATT_BLOCK, HEAD_DIM), 3, 0)
    og_b, od_b = lax.map(lambda qs: (gqa_attend(qs[0], kg_all, vg_all), diff_attend(qs[1], kd_all, vd_all, lam)),
                         (qg_b, qd_b))
    og = jnp.moveaxis(og_b, 0, 3).reshape(b_, GQA_KV_HEADS, GQA_GROUP, L, HEAD_DIM)
    od = jnp.moveaxis(od_b, 0, 2).reshape(b_, DIFF_HEADS, L, DIFF_V_DIM)
    out_l = merge(og, od)
    out_c = merge(gqa_attend(cq_g, ck_g, cv_g), diff_attend(cq_d, ck_d, cv_d, lam)) if need_ctx else None
    return out_c, out_l


def moe(t, router_w, router_bias, w_gate, w_up, w_down, sw_gate, sw_up, sw_down):
    T, D = t.shape
    s = jax.nn.sigmoid((t @ router_w).astype(jnp.float32))
    sel = s + router_bias.astype(jnp.float32)
    grp_score = jnp.sum(lax.top_k(sel.reshape(T, N_EXPERT_GROUPS, N_EXPERTS // N_EXPERT_GROUPS), 2)[0], -1)
    _, gidx = lax.top_k(grp_score, TOPK_GROUPS)
    gmask = jnp.any(gidx[:, :, None] == jnp.arange(N_EXPERT_GROUPS)[None, None, :], axis=1)
    emask = jnp.repeat(gmask, N_EXPERTS // N_EXPERT_GROUPS, axis=1)
    _, eidx = lax.top_k(jnp.where(emask, sel, -jnp.inf), TOP_K)
    wsel = jnp.take_along_axis(s, eidx, axis=-1)
    wsel = wsel / jnp.sum(wsel, -1, keepdims=True) * ROUTED_SCALE

    n_assign = T * TOP_K
    flat_e = eidx.reshape(-1)
    flat_w = wsel.reshape(-1)
    flat_tok = jnp.repeat(jnp.arange(T, dtype=jnp.int32), TOP_K)
    order = jnp.argsort(flat_e)
    se, stok, sw = flat_e[order], flat_tok[order], flat_w[order]
    counts = jnp.bincount(flat_e, length=N_EXPERTS).astype(jnp.int32)
    padded = (counts + MOE_BLOCK - 1) // MOE_BLOCK * MOE_BLOCK
    pad_end = jnp.cumsum(padded)
    pad_start = pad_end - padded
    start = jnp.cumsum(counts) - counts
    dest = pad_start[se] + (jnp.arange(n_assign, dtype=jnp.int32) - start[se])
    n_blocks = (n_assign + N_EXPERTS * (MOE_BLOCK - 1) + MOE_BLOCK - 1) // MOE_BLOCK
    n_pad = n_blocks * MOE_BLOCK
    tok_buf = jnp.zeros((n_pad,), jnp.int32).at[dest].set(stok)
    w_buf = jnp.zeros((n_pad,), jnp.float32).at[dest].set(sw)
    blk_start = jnp.arange(n_blocks, dtype=jnp.int32) * MOE_BLOCK
    blk_expert = jnp.minimum(jnp.searchsorted(pad_end, blk_start, side='right'), N_EXPERTS - 1)

    def expert_block(args):
        e, tok, wt = args
        xb = t[tok]
        hdn = jax.nn.silu(xb @ w_gate[e]) * (xb @ w_up[e])
        return (hdn @ w_down[e]) * wt[:, None].astype(t.dtype)

    yb = lax.map(expert_block, (blk_expert, tok_buf.reshape(n_blocks, MOE_BLOCK), w_buf.reshape(n_blocks, MOE_BLOCK)))
    routed = jnp.zeros_like(t).at[tok_buf].add(yb.reshape(n_pad, D))
    shared = (jax.nn.silu(t @ sw_gate) * (t @ sw_up)) @ sw_down
    return shared + routed


def setup_inputs(seed: int = 0) -> dict:
    key = jax.random.key(seed)
    ks = iter(jax.random.split(key, 48))
    f32 = jnp.float32
    D = D_MODEL
    nrm = lambda shape, std: jax.random.normal(next(ks), shape, f32) * std
    mix_out = 2048
    base_decay = jnp.log1p(-(2.0 ** (-5.0 - jnp.arange(RET_HEADS, dtype=f32))))
    a_im_base = jnp.pi * jnp.arange(S5_STATE, dtype=f32)
    return {
        'x': nrm((BATCH, SEQ, D), 1.0),
        'c': nrm((BATCH, D), 1.0),
        'ctx': nrm((BATCH, CTX_LEN, D), 1.0),
        'c_ctx': nrm((D,), 1.0),
        'mod_w': nrm((DEPTH, D, 6 * D), 0.5 * D ** -0.5),
        'mod_b': nrm((DEPTH, 6 * D), 0.01),
        'ln_g': 1.0 + nrm((DEPTH, 2, D), 0.02),
        'ln_b': nrm((DEPTH, 2, D), 0.02),
        'ev_w_in': nrm((N_EVEN, D, EVEN_IN), D ** -0.5),
        'ev_w_out': nrm((N_EVEN, mix_out, D), mix_out ** -0.5 * DEEPNORM_BETA),
        'ret_log_decay': base_decay * (1.0 + nrm((N_EVEN, 2, RET_HEADS), 0.01)),
        's5_a_re': -0.5 + nrm((N_EVEN, 2, S5_GROUPS, S5_STATE), 0.01),
        's5_a_im': a_im_base + nrm((N_EVEN, 2, S5_GROUPS, S5_STATE), 0.01),
        's5_log_dt': jax.random.uniform(next(ks), (N_EVEN, 2, S5_GROUPS), f32, math.log(1e-3), math.log(1e-1)),
        's5_b_re': nrm((N_EVEN, S5_GROUPS, S5_STATE, S5_GROUP), (2 * S5_GROUP) ** -0.5),
        's5_b_im': nrm((N_EVEN, S5_GROUPS, S5_STATE, S5_GROUP), (2 * S5_GROUP) ** -0.5),
        's5_c_re': nrm((N_EVEN, S5_GROUPS, S5_GROUP, S5_STATE), 2.0 * S5_STATE ** -0.5),
        's5_c_im': nrm((N_EVEN, S5_GROUPS, S5_GROUP, S5_STATE), 2.0 * S5_STATE ** -0.5),
        's5_d': nrm((N_EVEN, S5_CHANNELS), 1.0),
        's5_w_glu': nrm((N_EVEN, S5_CHANNELS, S5_CHANNELS), S5_CHANNELS ** -0.5),
        's5_b_glu': nrm((N_EVEN, S5_CHANNELS), 0.01),
        'od_w_in': nrm((N_ODD, D, ODD_IN), D ** -0.5),
        'od_w_out': nrm((N_ODD, mix_out, D), mix_out ** -0.5 * DEEPNORM_BETA),
        'qk_norm_g': 1.0 + nrm((N_ODD, 2, HEAD_DIM), 0.02),
        'diff_lambda': nrm((N_ODD, 4, HEAD_DIM), 0.1),
        'diff_subln_g': 1.0 + nrm((N_ODD, DIFF_V_DIM), 0.02),
        'router_w': nrm((DEPTH, D, N_EXPERTS), D ** -0.5),
        'router_bias': nrm((DEPTH, N_EXPERTS), 0.01),
        'exp_w_gate': nrm((DEPTH, N_EXPERTS, D, EXPERT_HIDDEN), D ** -0.5),
        'exp_w_up': nrm((DEPTH, N_EXPERTS, D, EXPERT_HIDDEN), D ** -0.5),
        'exp_w_down': nrm((DEPTH, N_EXPERTS, EXPERT_HIDDEN, D), EXPERT_HIDDEN ** -0.5 * DEEPNORM_BETA),
        'sh_w_gate': nrm((DEPTH, D, SHARED_HIDDEN), D ** -0.5),
        'sh_w_up': nrm((DEPTH, D, SHARED_HIDDEN), D ** -0.5),
        'sh_w_down': nrm((DEPTH, SHARED_HIDDEN, D), SHARED_HIDDEN ** -0.5 * DEEPNORM_BETA),
    }


def reference(x, c, ctx, c_ctx, mod_w, mod_b, ln_g, ln_b, ev_w_in, ev_w_out, ret_log_decay,
              s5_a_re, s5_a_im, s5_log_dt, s5_b_re, s5_b_im, s5_c_re, s5_c_im, s5_d, s5_w_glu, s5_b_glu,
              od_w_in, od_w_out, qk_norm_g, diff_lambda, diff_subln_g,
              router_w, router_bias, exp_w_gate, exp_w_up, exp_w_down, sh_w_gate, sh_w_up, sh_w_down):
    b_, L, D = x.shape
    ROWS = L // GRID_W
    row = jnp.repeat(jnp.arange(ROWS, dtype=jnp.int32), GRID_W)
    col = jnp.tile(jnp.arange(GRID_W, dtype=jnp.int32), ROWS)
    for i in range(DEPTH):
        need_ctx = i < DEPTH - 1
        j = i // 2
        sh1, sc1, g1, sh2, sc2, g2 = modulation(c, mod_w[i], mod_b[i])
        csh1, csc1, cg1, csh2, csc2, cg2 = modulation(c_ctx[None], mod_w[i], mod_b[i])
        xl_in = x * (1.0 + sc1) + sh1
        xc_in = ctx * (1.0 + csc1) + csh1
        if i % 2 == 0:
            yc, yl = even_mixer(xc_in, xl_in, row, col, ev_w_in[j], ev_w_out[j], ret_log_decay[j],
                                s5_a_re[j], s5_a_im[j], s5_log_dt[j], s5_b_re[j], s5_b_im[j],
                                s5_c_re[j], s5_c_im[j], s5_d[j], s5_w_glu[j], s5_b_glu[j], need_ctx)
        else:
            yc, yl = odd_mixer(xc_in, xl_in, row, col, od_w_in[j], od_w_out[j], qk_norm_g[j],
                               diff_lambda[j], diff_subln_g[j], 0.8 - 0.6 * math.exp(-0.3 * i), need_ctx)
        x = layer_norm(DEEPNORM_ALPHA * x + g1 * yl, ln_g[i, 0], ln_b[i, 0])
        xl2 = x * (1.0 + sc2) + sh2
        if need_ctx:
            ctx = layer_norm(DEEPNORM_ALPHA * ctx + cg1 * yc, ln_g[i, 0], ln_b[i, 0])
            xc2 = ctx * (1.0 + csc2) + csh2
            n_c = xc2.shape[0] * xc2.shape[1]
            f = moe(jnp.concatenate([xc2.reshape(-1, D), xl2.reshape(-1, D)], axis=0),
                    router_w[i], router_bias[i], exp_w_gate[i], exp_w_up[i], exp_w_down[i],
                    sh_w_gate[i], sh_w_up[i], sh_w_down[i])
            fc = f[:n_c].reshape(ctx.shape)
            fl = f[n_c:].reshape(x.shape)
            ctx = layer_norm(DEEPNORM_ALPHA * ctx + cg2 * fc, ln_g[i, 1], ln_b[i, 1])
        else:
            fl = moe(xl2.reshape(-1, D), router_w[i], router_bias[i], exp_w_gate[i], exp_w_up[i],
                     exp_w_down[i], sh_w_gate[i], sh_w_up[i], sh_w_down[i]).reshape(x.shape)
        x = layer_norm(DEEPNORM_ALPHA * x + g2 * fl, ln_g[i, 1], ln_b[i, 1])
    return x
```

```python
import functools
import math

import jax
import jax.numpy as jnp
from jax import lax
from jax.experimental import pallas as pl
from jax.experimental.pallas import tpu as pltpu

F32 = jnp.float32
BF16 = jnp.bfloat16
U32 = jnp.uint32
I32 = jnp.int32

D_MODEL = 2048
DEPTH = 2
GRID_W = 64
ROPE_BASE = 10000.0
LN_EPS = 1e-6
DEEPNORM_ALPHA = (2 * DEPTH) ** 0.25
RET_HEADS = 4
RET_DIM = 256
S5_CHANNELS = D_MODEL // 2
S5_GROUP = 16
S5_GROUPS = S5_CHANNELS // S5_GROUP
S5_STATE = 64
HEAD_DIM = 128
GQA_HEADS = 8
GQA_KV_HEADS = 2
GQA_GROUP = GQA_HEADS // GQA_KV_HEADS
DIFF_HEADS = 4
DIFF_V_DIM = 2 * HEAD_DIM
N_EXPERTS = 64
TOP_K = 8
N_EXPERT_GROUPS = 8
TOPK_GROUPS = 4
EXPERT_HIDDEN = 512
ROUTED_SCALE = 2.5
LOG2E = 1.4426950408889634

LANES = 128
VMEM_LIMIT = 56 * 1024 * 1024

RET_CHUNK = 256
S5_STEP = 16
MOE_TM = 256
ROW_TILE = 256


def _cparams(n_axes):
    return pltpu.CompilerParams(dimension_semantics=("arbitrary",) * n_axes, vmem_limit_bytes=VMEM_LIMIT)


def _dot(a, b):
    return jnp.dot(a, b, preferred_element_type=F32)


def _dot_nt(a, b):
    return lax.dot_general(a, b, (((1,), (1,)), ((), ())), preferred_element_type=F32)


def _dot_tn(a, b):
    return lax.dot_general(a, b, (((0,), (0,)), ((), ())), preferred_element_type=F32)


def _split(a):
    hi = a.astype(BF16)
    lo = (a - hi.astype(F32)).astype(BF16)
    return hi, lo


def _dot3(a, b):
    ah, al = _split(a)
    bh, bl = _split(b)
    return _dot(ah, bh) + _dot(al, bh) + _dot(ah, bl)


def _sigmoid(x):
    return 1.0 / (1.0 + jnp.exp(-x))


def _silu(x):
    return x * _sigmoid(x)


def _gelu_tanh(x):
    return 0.5 * x * (1.0 + jnp.tanh(math.sqrt(2.0 / math.pi) * (x + 0.044715 * (x * x * x))))


def _layer_norm(h, g, b):
    mu = jnp.mean(h, axis=-1, keepdims=True)
    d = h - mu
    var = jnp.mean(d * d, axis=-1, keepdims=True)
    return d * lax.rsqrt(var + LN_EPS) * g + b


def _pack_bf16_pairs(x):
    n = x.shape[1] // 2
    lo = lax.bitcast_convert_type(x[:, :n].astype(BF16).astype(F32), U32)
    hi = lax.bitcast_convert_type(x[:, n:].astype(BF16).astype(F32), U32)
    return (lo >> 16) | (hi & jnp.uint32(0xFFFF0000))


def _unpack_bf16_pairs(w):
    lo = lax.bitcast_convert_type(w << 16, F32)
    hi = lax.bitcast_convert_type(w & jnp.uint32(0xFFFF0000), F32)
    return lo, hi


def _modulate_rows(x, mod_ref, shift_row, scale_row, row0, n_lat):
    rows = row0 + lax.broadcasted_iota(I32, (x.shape[0], 1), 0)
    is_ctx = rows >= n_lat
    sc = jnp.where(is_ctx, mod_ref[1, scale_row:scale_row + 1, :], mod_ref[0, scale_row:scale_row + 1, :])
    sh = jnp.where(is_ctx, mod_ref[1, shift_row:shift_row + 1, :], mod_ref[0, shift_row:shift_row + 1, :])
    return x * (1.0 + sc) + sh


def _select_rows(mod_ref, row, row0, n_rows, n_lat):
    rows = row0 + lax.broadcasted_iota(I32, (n_rows, 1), 0)
    return jnp.where(rows >= n_lat, mod_ref[1, row:row + 1, :], mod_ref[0, row:row + 1, :])


def _mod_kernel(v_ref, w_ref, b_ref, o_ref):
    v = v_ref[...]
    o_ref[0] = _dot3(_silu(v), w_ref[0]) + b_ref[0]


def _modulation(c, c_ctx, mod_w, mod_b):
    depth, d, n = mod_w.shape
    v = jnp.concatenate([c[:1], c_ctx[None], jnp.zeros((6, d), F32)], axis=0)
    tn = 512
    out = pl.pallas_call(
        _mod_kernel,
        grid=(depth, n // tn),
        in_specs=[
            pl.BlockSpec((8, d), lambda i, j: (0, 0)),
            pl.BlockSpec((1, d, tn), lambda i, j: (i, 0, j)),
            pl.BlockSpec((1, 1, tn), lambda i, j: (i, 0, j)),
        ],
        out_specs=pl.BlockSpec((1, 8, tn), lambda i, j: (i, 0, j)),
        out_shape=jax.ShapeDtypeStruct((depth, 8, n), F32),
        compiler_params=_cparams(2),
        name="modulation",
    )(v, mod_w, mod_b.reshape(depth, 1, n))
    return out[:, :2].reshape(depth, 2, 6, d)


def _rope_tables(n_lat, n_ctx, head_dim):
    q = head_dim // 4
    t = jnp.arange(n_lat, dtype=I32)
    row = (t // GRID_W).astype(F32)
    col = (t % GRID_W).astype(F32)
    freqs = ROPE_BASE ** (-jnp.arange(q, dtype=F32) / q)
    ar = row[:, None] * freqs[None, :]
    ac = col[:, None] * freqs[None, :]
    cos = jnp.concatenate([jnp.cos(ar), jnp.cos(ar), jnp.cos(ac), jnp.cos(ac)], axis=-1)
    sin = jnp.concatenate([-jnp.sin(ar), jnp.sin(ar), -jnp.sin(ac), jnp.sin(ac)], axis=-1)
    cos = jnp.concatenate([cos, jnp.ones((n_ctx, head_dim), F32)], axis=0)
    sin = jnp.concatenate([sin, jnp.zeros((n_ctx, head_dim), F32)], axis=0)
    return cos, sin


def _proj0_kernel(x_ref, mod_ref, w_ref, cos_ref, sin_ref, o_ref, xm_ref, *, tm, n_lat):
    i = pl.program_id(0)
    j = pl.program_id(1)

    @pl.when(j == 0)
    def _():
        xm_ref[...] = _modulate_rows(x_ref[...], mod_ref, 0, 1, i * tm, n_lat).astype(BF16)

    acc = _dot(xm_ref[...], w_ref[...])

    @pl.when(j >= 2)
    def _():
        o_ref[...] = acc.astype(BF16)

    @pl.when(j < 2)
    def _():
        scale = jnp.where(j == 0, RET_DIM ** -0.5, 1.0).astype(F32)
        for c in range(acc.shape[1] // LANES):
            xs = acc[:, c * LANES:(c + 1) * LANES]
            t0 = (c % 2) * LANES
            rot = xs * cos_ref[:, t0:t0 + LANES] + pltpu.roll(xs, LANES // 2, 1) * sin_ref[:, t0:t0 + LANES]
            o_ref[:, c * LANES:(c + 1) * LANES] = (rot * scale).astype(BF16)


def _proj0(x_all, mod, w_bf, cos, sin, n_lat):
    m, d = x_all.shape
    n = w_bf.shape[1]
    tm = 640 if m % 640 == 0 else 256
    tn = 1024
    return pl.pallas_call(
        functools.partial(_proj0_kernel, tm=tm, n_lat=n_lat),
        grid=(m // tm, n // tn),
        in_specs=[
            pl.BlockSpec((tm, d), lambda i, j: (i, 0)),
            pl.BlockSpec((2, 6, d), lambda i, j: (0, 0, 0)),
            pl.BlockSpec((d, tn), lambda i, j: (0, j)),
            pl.BlockSpec((tm, RET_DIM), lambda i, j: (i, 0)),
            pl.BlockSpec((tm, RET_DIM), lambda i, j: (i, 0)),
        ],
        out_specs=pl.BlockSpec((tm, tn), lambda i, j: (i, j)),
        out_shape=jax.ShapeDtypeStruct((m, n), BF16),
        scratch_shapes=[pltpu.VMEM((tm, d), BF16)],
        compiler_params=_cparams(2),
        name="proj0",
    )(x_all, mod, w_bf, cos, sin)


def _ret_tables(log_decay):
    c = RET_CHUNK
    lg = log_decay.astype(F32)
    idx = jnp.arange(c, dtype=F32)
    diff = idx[:, None] - idx[None, :]
    lower = jnp.where(diff >= 0, jnp.exp(jnp.maximum(diff, 0.0)[None] * lg[0][:, None, None]), 0.0)
    upper = jnp.where(diff <= 0, jnp.exp(jnp.maximum(-diff, 0.0)[None] * lg[1][:, None, None]), 0.0)
    mask = lower + upper
    ones = jnp.ones((1, 1, RET_DIM), F32)
    kdec_f = jnp.exp((c - 1 - idx)[None, :] * lg[0][:, None])[:, :, None] * ones
    qdec_f = jnp.exp((idx + 1)[None, :] * lg[0][:, None])[:, :, None] * ones
    kdec_b = jnp.exp(idx[None, :] * lg[1][:, None])[:, :, None] * ones
    qdec_b = jnp.exp((c - idx)[None, :] * lg[1][:, None])[:, :, None] * ones
    g_chunk = jnp.exp(c * lg)
    return mask, kdec_f, qdec_f, kdec_b, qdec_b, g_chunk


def _ret_bwd_state_kernel(gc_ref, k_ref, v_ref, kdec_ref, sb_ref, s_ref):
    h = pl.program_id(0)
    s = pl.program_id(1)

    @pl.when(s == 0)
    def _():
        s_ref[...] = jnp.zeros_like(s_ref)

    sb_ref[0, 0] = s_ref[...].astype(BF16)
    kd = (k_ref[...].astype(F32) * kdec_ref[0]).astype(BF16)
    s_ref[...] = gc_ref[1, h] * s_ref[...] + _dot_tn(kd, v_ref[...])


def _ret_out_kernel(gc_ref, q_ref, k_ref, v_ref, sb_ref, mask_ref, qdf_ref, qdb_ref, kdf_ref, o_ref, s_ref):
    h = pl.program_id(0)
    s = pl.program_id(1)

    @pl.when(s == 0)
    def _():
        s_ref[...] = jnp.zeros_like(s_ref)

    q = q_ref[...]
    k = k_ref[...]
    v = v_ref[...]
    qf = q.astype(F32)
    w = (_dot_nt(q, k) * mask_ref[0]).astype(BF16)
    o = _dot(w, v)
    o = o + _dot((qf * qdf_ref[0]).astype(BF16), s_ref[...].astype(BF16))
    o = o + _dot((qf * qdb_ref[0]).astype(BF16), sb_ref[0, 0])
    mu = jnp.mean(o, axis=-1, keepdims=True)
    d = o - mu
    var = jnp.mean(d * d, axis=-1, keepdims=True)
    o_ref[...] = (d * lax.rsqrt(var + LN_EPS)).astype(BF16)
    kd = (k.astype(F32) * kdf_ref[0]).astype(BF16)
    s_ref[...] = gc_ref[0, h] * s_ref[...] + _dot_tn(kd, v)


def _retention(p0, log_decay, n_lat, n_ctx):
    m = p0.shape[0]
    c = RET_CHUNK
    nc = m // c
    nlc = n_lat // c
    hh = RET_HEADS
    mask, kdec_f, qdec_f, kdec_b, qdec_b, g_chunk = _ret_tables(log_decay)
    smem = pl.BlockSpec(memory_space=pltpu.SMEM)

    def bchunk(s):
        return nc - 1 - s

    sb = pl.pallas_call(
        _ret_bwd_state_kernel,
        grid=(hh, nc),
        in_specs=[
            smem,
            pl.BlockSpec((c, RET_DIM), lambda h, s: (bchunk(s), hh + h)),
            pl.BlockSpec((c, RET_DIM), lambda h, s: (bchunk(s), 2 * hh + h)),
            pl.BlockSpec((1, c, RET_DIM), lambda h, s: (h, 0, 0)),
        ],
        out_specs=pl.BlockSpec((1, 1, RET_DIM, RET_DIM), lambda h, s: (h, bchunk(s), 0, 0)),
        out_shape=jax.ShapeDtypeStruct((hh, nc, RET_DIM, RET_DIM), BF16),
        scratch_shapes=[pltpu.VMEM((RET_DIM, RET_DIM), F32)],
        compiler_params=_cparams(2),
        name="ret_bwd_state",
    )(g_chunk, p0, p0, kdec_b)

    def fchunk(s):
        return (s + nlc) % nc

    return pl.pallas_call(
        _ret_out_kernel,
        grid=(hh, nc),
        in_specs=[
            smem,
            pl.BlockSpec((c, RET_DIM), lambda h, s: (fchunk(s), h)),
            pl.BlockSpec((c, RET_DIM), lambda h, s: (fchunk(s), hh + h)),
            pl.BlockSpec((c, RET_DIM), lambda h, s: (fchunk(s), 2 * hh + h)),
            pl.BlockSpec((1, 1, RET_DIM, RET_DIM), lambda h, s: (h, fchunk(s), 0, 0)),
            pl.BlockSpec((1, c, c), lambda h, s: (h, 0, 0)),
            pl.BlockSpec((1, c, RET_DIM), lambda h, s: (h, 0, 0)),
            pl.BlockSpec((1, c, RET_DIM), lambda h, s: (h, 0, 0)),
            pl.BlockSpec((1, c, RET_DIM), lambda h, s: (h, 0, 0)),
        ],
        out_specs=pl.BlockSpec((c, RET_DIM), lambda h, s: (fchunk(s), h)),
        out_shape=jax.ShapeDtypeStruct((m, hh * RET_DIM), BF16),
        scratch_shapes=[pltpu.VMEM((RET_DIM, RET_DIM), F32)],
        compiler_params=_cparams(2),
        name="ret_out",
    )(g_chunk, p0, p0, p0, sb, mask, qdec_f, qdec_b, kdec_f)


def _s5_tables(a_re, a_im, log_dt, b_re, b_im, c_re, c_im, d_skip, n_scan):
    f32 = F32
    st = S5_STEP
    g_, p_, s_ = S5_GROUPS, S5_STATE, S5_GROUP
    b_re, b_im, c_re, c_im = (t.astype(f32) for t in (b_re, b_im, c_re, c_im))
    ws, vs, aps, ks = [], [], [], []
    for direction in range(2):
        are, aim = a_re[direction].astype(f32), a_im[direction].astype(f32)
        dt = jnp.exp(log_dt[direction].astype(f32))[:, None]
        zr, zi = are * dt, aim * dt
        mag = jnp.exp(zr)
        ab_re, ab_im = mag * jnp.cos(zi), mag * jnp.sin(zi)
        den = jnp.square(are) + jnp.square(aim)
        nr, ni = ab_re - 1.0, ab_im
        f_re = (nr * are + ni * aim) / den
        f_im = (ni * are - nr * aim) / den
        bb_re = f_re[..., None] * b_re - f_im[..., None] * b_im
        bb_im = f_re[..., None] * b_im + f_im[..., None] * b_re
        pr, pi = [jnp.ones_like(ab_re)], [jnp.zeros_like(ab_im)]
        for _ in range(st):
            pr.append(pr[-1] * ab_re - pi[-1] * ab_im)
            pi.append(pr[-2] * ab_im + pi[-1] * ab_re)
        pw_re, pw_im = jnp.stack(pr), jnp.stack(pi)
        ca_re = c_re[None] * pw_re[:, :, None, :] - c_im[None] * pw_im[:, :, None, :]
        ca_im = c_re[None] * pw_im[:, :, None, :] + c_im[None] * pw_re[:, :, None, :]
        kk = (jnp.einsum('tgip,gpj->tgij', ca_re[:st], bb_re, precision='highest')
              - jnp.einsum('tgip,gpj->tgij', ca_im[:st], bb_im, precision='highest'))
        ks.append(kk)
        e = (st - 1 - jnp.arange(st)) if direction == 0 else jnp.arange(st)
        w_re = pw_re[e][:, :, :, None] * bb_re[None] - pw_im[e][:, :, :, None] * bb_im[None]
        w_im = pw_re[e][:, :, :, None] * bb_im[None] + pw_im[e][:, :, :, None] * bb_re[None]
        w = jnp.concatenate([w_re, w_im], axis=2)
        ws.append(jnp.transpose(w, (1, 0, 3, 2)).reshape(g_, st * s_, 2 * p_))
        e2 = (jnp.arange(st) + 1) if direction == 0 else (st - jnp.arange(st))
        v = jnp.concatenate([ca_re[e2], -ca_im[e2]], axis=3)
        vs.append(jnp.transpose(v, (1, 3, 0, 2)).reshape(g_, 2 * p_, st * s_))
        qr, qi = pw_re[st], pw_im[st]
        rows = []
        for _ in range(n_scan):
            rows.append(jnp.stack([jnp.concatenate([qr, qr], -1), jnp.concatenate([-qi, qi], -1)], axis=1))
            qr, qi = qr * qr - qi * qi, 2.0 * qr * qi
        aps.append(jnp.stack(rows, axis=1))
    tt = jnp.arange(st)
    lag = tt[None, :] - tt[:, None]
    kf = ks[0][jnp.clip(lag, 0, st - 1)]
    kb = ks[1][jnp.clip(-lag, 0, st - 1)]
    tm = jnp.where((lag >= 0)[:, :, None, None, None], kf, 0.0) + jnp.where((lag <= 0)[:, :, None, None, None], kb, 0.0)
    tm = jnp.transpose(tm, (2, 0, 4, 1, 3)).reshape(g_, st * s_, st * s_)
    dsk = d_skip.astype(f32).reshape(g_, s_)
    tm = tm + jnp.eye(st * s_, dtype=f32)[None] * jnp.tile(dsk, (1, st))[:, None, :]
    return tm, jnp.stack(ws), jnp.stack(vs), jnp.stack(aps)


def _s5_kernel(u_ref, t_ref, w_ref, v_ref, a_ref, y_ref, ha_ref, hb_ref, *, ncs, nls, n_scan):
    n = ncs + nls + ncs
    u = u_ref[0]

    def mm(x, mat):
        mh, ml = _split(mat)
        return _dot(x, mh) + _dot(x, ml)

    def mm3(x, mat):
        xh, xl = _split(x)
        mh, ml = _split(mat)
        return _dot(xh, mh) + _dot(xl, mh) + _dot(xh, ml)

    rows = lax.broadcasted_iota(I32, (n, 1), 0)

    def scan(z, direction):
        bufs = (ha_ref, hb_ref)
        bufs[0][...] = z
        for kk in range(n_scan):
            src, dst = bufs[kk % 2], bufs[(kk + 1) % 2]
            sft = 1 << kk
            a1 = a_ref[direction, 0, kk, 0:1, :]
            a2 = a_ref[direction, 0, kk, 1:2, :]
            if sft >= n:
                dst[...] = src[...]
                continue
            if sft % 8 == 0:
                if direction == 0:
                    prev = src[0:n - sft, :]
                    dst[0:sft, :] = src[0:sft, :]
                    dst[sft:n, :] = src[sft:n, :] + a1 * prev + a2 * pltpu.roll(prev, S5_STATE, 1)
                else:
                    nxt = src[sft:n, :]
                    dst[n - sft:n, :] = src[n - sft:n, :]
                    dst[0:n - sft, :] = src[0:n - sft, :] + a1 * nxt + a2 * pltpu.roll(nxt, S5_STATE, 1)
            else:
                cur = src[...]
                if direction == 0:
                    sh = jnp.where(rows >= sft, pltpu.roll(cur, sft, 0), 0.0)
                else:
                    sh = jnp.where(rows < n - sft, pltpu.roll(cur, n - sft, 0), 0.0)
                dst[...] = cur + a1 * sh + a2 * pltpu.roll(sh, S5_STATE, 1)
        return bufs[n_scan % 2][...]

    y = mm(u, t_ref[0])
    hf = scan(mm(u, w_ref[0, 0]), 0)
    hf_prev = jnp.where(rows >= 1, pltpu.roll(hf, 1, 0), 0.0)
    yf = mm3(hf_prev, v_ref[0, 0])
    hb = scan(mm(u, w_ref[1, 0]), 1)
    hb_next = jnp.where(rows < n - 1, pltpu.roll(hb, n - 1, 0), 0.0)
    yb = mm3(hb_next, v_ref[1, 0])
    y_ref[0, 0:nls, :] = (y + yf + yb)[ncs:ncs + nls, :]
    y_ref[0, nls:nls + ncs, :] = (y + yf)[0:ncs, :] + yb[ncs + nls:n, :]


def _s5(u, tabs, n_lat, n_ctx):
    st = S5_STEP
    g_, s_ = S5_GROUPS, S5_GROUP
    nls, ncs = n_lat // st, n_ctx // st
    n = ncs + nls + ncs
    n_scan = max(1, (n - 1).bit_length())
    t_mat, w_mat, v_mat, apow = tabs(n_scan)
    ug = jnp.transpose(u.reshape(nls + ncs, st, g_, s_), (2, 0, 1, 3)).reshape(g_, nls + ncs, st * s_)
    useq = jnp.concatenate([ug[:, nls:], ug[:, :nls], ug[:, nls:]], axis=1)
    tw = st * s_
    y = pl.pallas_call(
        functools.partial(_s5_kernel, ncs=ncs, nls=nls, n_scan=n_scan),
        grid=(g_,),
        in_specs=[
            pl.BlockSpec((1, n, tw), lambda g: (g, 0, 0)),
            pl.BlockSpec((1, tw, tw), lambda g: (g, 0, 0)),
            pl.BlockSpec((2, 1, tw, 2 * S5_STATE), lambda g: (0, g, 0, 0)),
            pl.BlockSpec((2, 1, 2 * S5_STATE, tw), lambda g: (0, g, 0, 0)),
            pl.BlockSpec((2, 1, n_scan, 2, 2 * S5_STATE), lambda g: (0, g, 0, 0, 0)),
        ],
        out_specs=pl.BlockSpec((1, nls + ncs, tw), lambda g: (g, 0, 0)),
        out_shape=jax.ShapeDtypeStruct((g_, nls + ncs, tw), F32),
        scratch_shapes=[pltpu.VMEM((n, 2 * S5_STATE), F32), pltpu.VMEM((n, 2 * S5_STATE), F32)],
        compiler_params=_cparams(1),
        name="s5",
    )(useq, t_mat, w_mat, v_mat, apow)
    return jnp.transpose(y.reshape(g_, nls + ncs, st, s_), (1, 2, 0, 3)).reshape(n_lat + n_ctx, g_ * s_)


def _post_mix(y, x_ref, mod_ref, ln_ref, rw_ref, xo_ref, xp_ref, lg_ref, row0, n_lat):
    tm = y.shape[0]
    g1 = _select_rows(mod_ref, 2, row0, tm, n_lat)
    xn = _layer_norm(DEEPNORM_ALPHA * x_ref[...] + g1 * y, ln_ref[0:1, :], ln_ref[1:2, :])
    xo_ref[...] = xn
    sc2 = _select_rows(mod_ref, 4, row0, tm, n_lat)
    sh2 = _select_rows(mod_ref, 3, row0, tm, n_lat)
    x2 = xn * (1.0 + sc2) + sh2
    xp_ref[...] = _pack_bf16_pairs(x2)
    lg_ref[...] = _dot3(x2, rw_ref[...])


def _finish0_kernel(r_ref, g_ref, s_ref, x_ref, mod_ref, ln_ref, wglu_ref, bglu_ref, wout_ref, rw_ref,
                    xo_ref, xp_ref, lg_ref, *, tm, n_lat):
    row0 = pl.program_id(0) * tm
    ret = (r_ref[...].astype(F32) * _silu(g_ref[...].astype(F32))).astype(BF16)
    z = _gelu_tanh(s_ref[...])
    zb = z.astype(BF16)
    gate = _sigmoid(_dot(zb, wglu_ref[...]) + bglu_ref[...])
    s5o = (z * gate).astype(BF16)
    half = ret.shape[1]
    y = _dot(ret, wout_ref[0:half, :]) + _dot(s5o, wout_ref[half:, :])
    _post_mix(y, x_ref, mod_ref, ln_ref, rw_ref, xo_ref, xp_ref, lg_ref, row0, n_lat)


def _const_spec(shape):
    nd = len(shape)
    return pl.BlockSpec(shape, lambda i: (0,) * nd)


def _finish0(rn, p0, s5y, x_all, mod, ln, wglu, bglu, wout, rw, n_lat):
    m, d = x_all.shape
    tm = ROW_TILE
    half = d // 2
    outs = pl.pallas_call(
        functools.partial(_finish0_kernel, tm=tm, n_lat=n_lat),
        grid=(m // tm,),
        in_specs=[
            pl.BlockSpec((tm, half), lambda i: (i, 0)),
            pl.BlockSpec((tm, half), lambda i: (i, 3)),
            pl.BlockSpec((tm, half), lambda i: (i, 0)),
            pl.BlockSpec((tm, d), lambda i: (i, 0)),
            _const_spec((2, 6, d)),
            _const_spec((2, d)),
            _const_spec((half, half)),
            _const_spec((1, half)),
            _const_spec((d, d)),
            _const_spec((d, LANES)),
        ],
        out_specs=[
            pl.BlockSpec((tm, d), lambda i: (i, 0)),
            pl.BlockSpec((tm, half), lambda i: (i, 0)),
            pl.BlockSpec((tm, LANES), lambda i: (i, 0)),
        ],
        out_shape=[
            jax.ShapeDtypeStruct((m, d), F32),
            jax.ShapeDtypeStruct((m, half), U32),
            jax.ShapeDtypeStruct((m, LANES), F32),
        ],
        compiler_params=_cparams(1),
        name="finish0",
    )(rn, p0, s5y, x_all, mod, ln, wglu, bglu, wout, rw)
    return outs


def _finish1_kernel(a_ref, x_ref, mod_ref, ln_ref, wout_ref, rw_ref, xo_ref, xp_ref, lg_ref, *, tm, n_lat):
    row0 = pl.program_id(0) * tm
    y = _dot(a_ref[...], wout_ref[...])
    _post_mix(y, x_ref, mod_ref, ln_ref, rw_ref, xo_ref, xp_ref, lg_ref, row0, n_lat)


def _finish1(att, x_all, mod, ln, wout, rw, n_lat):
    d = x_all.shape[1]
    m = att.shape[0]
    tm = ROW_TILE
    half = d // 2
    return pl.pallas_call(
        functools.partial(_finish1_kernel, tm=tm, n_lat=n_lat),
        grid=(m // tm,),
        in_specs=[
            pl.BlockSpec((tm, d), lambda i: (i, 0)),
            pl.BlockSpec((tm, d), lambda i: (i, 0)),
            _const_spec((2, 6, d)),
            _const_spec((2, d)),
            _const_spec((d, d)),
            _const_spec((d, LANES)),
        ],
        out_specs=[
            pl.BlockSpec((tm, d), lambda i: (i, 0)),
            pl.BlockSpec((tm, half), lambda i: (i, 0)),
            pl.BlockSpec((tm, LANES), lambda i: (i, 0)),
        ],
        out_shape=[
            jax.ShapeDtypeStruct((m, d), F32),
            jax.ShapeDtypeStruct((m, half), U32),
            jax.ShapeDtypeStruct((m, LANES), F32),
        ],
        compiler_params=_cparams(1),
        name="finish1",
    )(att, x_all, mod, ln, wout, rw)


def _route(logits, router_bias, tm):
    t = logits.shape[0]
    e = N_EXPERTS
    s = jax.nn.sigmoid(logits)
    sel = s + router_bias.astype(F32)
    grp_score = jnp.sum(lax.top_k(sel.reshape(t, N_EXPERT_GROUPS, e // N_EXPERT_GROUPS), 2)[0], -1)
    _, gidx = lax.top_k(grp_score, TOPK_GROUPS)
    gmask = jnp.any(gidx[:, :, None] == jnp.arange(N_EXPERT_GROUPS)[None, None, :], axis=1)
    emask = jnp.repeat(gmask, e // N_EXPERT_GROUPS, axis=1)
    _, eidx = lax.top_k(jnp.where(emask, sel, -jnp.inf), TOP_K)
    wsel = jnp.take_along_axis(s, eidx, axis=-1)
    wsel = wsel / jnp.sum(wsel, -1, keepdims=True) * ROUTED_SCALE
    member = jnp.sum(jax.nn.one_hot(eidx, e, dtype=I32), axis=1)
    incl = jnp.cumsum(member, axis=0)
    counts = incl[-1]
    rank = incl - member
    padded = (counts + tm - 1) // tm * tm
    pad_end = jnp.cumsum(padded)
    pad_start = pad_end - padded
    dest = jnp.take_along_axis(pad_start[None, :] + rank, eidx, axis=1).astype(I32)
    n_blocks = (t * TOP_K + e * (tm - 1)) // tm
    blk_start = jnp.arange(n_blocks, dtype=I32) * tm
    blk_expert = jnp.minimum(jnp.searchsorted(pad_end, blk_start, side='right'), e - 1).astype(I32)
    n_used = (pad_end[-1] // tm).astype(I32).reshape(1)
    return dest, wsel, blk_expert, n_used, n_blocks


def _dispatch_kernel(dest_ref, x_ref, xs_in_ref, xs_ref, sem, *, tb):
    del xs_in_ref

    def row_copy(r, k):
        d = dest_ref[0, 0, r * TOP_K + k]
        return pltpu.make_async_copy(x_ref.at[pl.ds(r, 1)], xs_ref.at[pl.ds(d, 1)], sem)

    def start(r, carry):
        for k in range(TOP_K):
            row_copy(r, k).start()
        return carry

    def wait(r, carry):
        for k in range(TOP_K):
            row_copy(r, k).wait()
        return carry

    lax.fori_loop(0, tb, start, 0)
    lax.fori_loop(0, tb, wait, 0)


def _dispatch(xp, dest, n_pad):
    t, w = xp.shape
    tb = ROW_TILE
    xs0 = jnp.zeros((n_pad, w), U32)
    return pl.pallas_call(
        functools.partial(_dispatch_kernel, tb=tb),
        grid=(t // tb,),
        in_specs=[
            pl.BlockSpec((1, 1, tb * TOP_K), lambda i: (i, 0, 0), memory_space=pltpu.SMEM),
            pl.BlockSpec((tb, w), lambda i: (i, 0)),
            pl.BlockSpec(memory_space=pl.ANY),
        ],
        out_specs=pl.BlockSpec(memory_space=pl.ANY),
        out_shape=jax.ShapeDtypeStruct((n_pad, w), U32),
        scratch_shapes=[pltpu.SemaphoreType.DMA(())],
        input_output_aliases={2: 0},
        compiler_params=_cparams(1),
        name="moe_dispatch",
    )(dest.reshape(t // tb, 1, tb * TOP_K), xp, xs0)


def _expert_kernel(be_ref, nu_ref, xs_ref, wg_ref, wu_ref, wd_ref, y_ref, wgb_ref, wub_ref, wdb_ref):
    b = pl.program_id(0)
    e = be_ref[b]
    prev = be_ref[jnp.maximum(b - 1, 0)]

    @pl.when((b == 0) | (e != prev))
    def _():
        wgb_ref[...] = wg_ref[0].astype(BF16)
        wub_ref[...] = wu_ref[0].astype(BF16)
        wdb_ref[...] = wd_ref[0].astype(BF16)

    @pl.when(b < nu_ref[0])
    def _():
        lo, hi = _unpack_bf16_pairs(xs_ref[...])
        lo = lo.astype(BF16)
        hi = hi.astype(BF16)
        half = lo.shape[1]
        gate = _dot(lo, wgb_ref[0:half, :]) + _dot(hi, wgb_ref[half:, :])
        up = _dot(lo, wub_ref[0:half, :]) + _dot(hi, wub_ref[half:, :])
        hid = (_silu(gate) * up).astype(BF16)
        y_ref[...] = _pack_bf16_pairs(_dot(hid, wdb_ref[...]))

    @pl.when(b >= nu_ref[0])
    def _():
        y_ref[...] = jnp.zeros_like(y_ref)


def _experts(xs, blk_expert, n_used, w_gate, w_up, w_down):
    n_pad, w = xs.shape
    tm = MOE_TM
    d, hdn = w_gate.shape[1], w_gate.shape[2]
    return pl.pallas_call(
        _expert_kernel,
        grid_spec=pltpu.PrefetchScalarGridSpec(
            num_scalar_prefetch=2,
            grid=(n_pad // tm,),
            in_specs=[
                pl.BlockSpec((tm, w), lambda b, be, nu: (b, 0)),
                pl.BlockSpec((1, d, hdn), lambda b, be, nu: (be[b], 0, 0)),
                pl.BlockSpec((1, d, hdn), lambda b, be, nu: (be[b], 0, 0)),
                pl.BlockSpec((1, hdn, d), lambda b, be, nu: (be[b], 0, 0)),
            ],
            out_specs=pl.BlockSpec((tm, w), lambda b, be, nu: (b, 0)),
            scratch_shapes=[pltpu.VMEM((d, hdn), BF16), pltpu.VMEM((d, hdn), BF16), pltpu.VMEM((hdn, d), BF16)],
        ),
        out_shape=jax.ShapeDtypeStruct((n_pad, w), U32),
        compiler_params=_cparams(1),
        name="moe_experts",
    )(blk_expert, n_used, xs, w_gate, w_up, w_down)


def _combine_kernel(dest_ref, xp_ref, ws_ref, x_ref, mod_ref, ln_ref, sg_ref, su_ref, sd_ref, y_hbm,
                    xo_ref, ybuf, sem, *, tb, n_lat):
    row0 = pl.program_id(0) * tb

    def row_copy(r, k):
        d = dest_ref[0, 0, r * TOP_K + k]
        return pltpu.make_async_copy(y_hbm.at[pl.ds(d, 1)], ybuf.at[pl.ds(k * tb + r, 1)], sem)

    def start(r, carry):
        for k in range(TOP_K):
            row_copy(r, k).start()
        return carry

    def wait(r, carry):
        for k in range(TOP_K):
            row_copy(r, k).wait()
        return carry

    lax.fori_loop(0, tb, start, 0)
    lo, hi = _unpack_bf16_pairs(xp_ref[...])
    lo = lo.astype(BF16)
    hi = hi.astype(BF16)
    half = lo.shape[1]
    gate = _dot(lo, sg_ref[0:half, :]) + _dot(hi, sg_ref[half:, :])
    up = _dot(lo, su_ref[0:half, :]) + _dot(hi, su_ref[half:, :])
    f = _dot((_silu(gate) * up).astype(BF16), sd_ref[...])
    lax.fori_loop(0, tb, wait, 0)
    acc_lo = jnp.zeros((tb, half), F32)
    acc_hi = jnp.zeros((tb, half), F32)
    for k in range(TOP_K):
        ylo, yhi = _unpack_bf16_pairs(ybuf[k * tb:(k + 1) * tb, :])
        wk = ws_ref[:, k:k + 1]
        acc_lo = acc_lo + wk * ylo
        acc_hi = acc_hi + wk * yhi
    f = f + jnp.concatenate([acc_lo, acc_hi], axis=1)
    g2 = _select_rows(mod_ref, 5, row0, tb, n_lat)
    xo_ref[...] = _layer_norm(DEEPNORM_ALPHA * x_ref[...] + g2 * f, ln_ref[0:1, :], ln_ref[1:2, :])


def _combine(dest, xp, wsel, x_all, mod, ln, sg, su, sd, y, n_lat):
    t, w = xp.shape
    d = x_all.shape[1]
    tb = ROW_TILE
    hdn = sg.shape[1]
    return pl.pallas_call(
        functools.partial(_combine_kernel, tb=tb, n_lat=n_lat),
        grid=(t // tb,),
        in_specs=[
            pl.BlockSpec((1, 1, tb * TOP_K), lambda i: (i, 0, 0), memory_space=pltpu.SMEM),
            pl.BlockSpec((tb, w), lambda i: (i, 0)),
            pl.BlockSpec((tb, TOP_K), lambda i: (i, 0)),
            pl.BlockSpec((tb, d), lambda i: (i, 0)),
            _const_spec((2, 6, d)),
            _const_spec((2, d)),
            _const_spec((d, hdn)),
            _const_spec((d, hdn)),
            _const_spec((hdn, d)),
            pl.BlockSpec(memory_space=pl.ANY),
        ],
        out_specs=pl.BlockSpec((tb, d), lambda i: (i, 0)),
        out_shape=jax.ShapeDtypeStruct((t, d), F32),
        scratch_shapes=[pltpu.VMEM((TOP_K * tb, w), U32), pltpu.SemaphoreType.DMA(())],
        compiler_params=_cparams(1),
        name="moe_combine",
    )(dest.reshape(t // tb, 1, tb * TOP_K), xp, wsel, x_all, mod, ln, sg, su, sd, y)


def _moe(xp, logits, x_res, mod, ln, router_bias, w_gate, w_up, w_down, sg, su, sd, n_lat):
    t = xp.shape[0]
    dest, wsel, blk_expert, n_used, n_blocks = _route(logits[:, :N_EXPERTS], router_bias, MOE_TM)
    xs = _dispatch(xp, dest, n_blocks * MOE_TM)
    y = _experts(xs, blk_expert, n_used, w_gate, w_up, w_down)
    return _combine(dest, xp, wsel, x_res, mod, ln, sg.astype(BF16), su.astype(BF16), sd.astype(BF16), y, n_lat)


def _rot_lanes(xs, cos, sin):
    q = HEAD_DIM // 4
    lane = lax.broadcasted_iota(I32, xs.shape, 1)
    partner = jnp.where((lane % (2 * q)) < q, pltpu.roll(xs, HEAD_DIM - q, 1), pltpu.roll(xs, q, 1))
    return xs * cos + partner * sin


def _projq_kernel(x_ref, mod_ref, w_ref, cos_ref, sin_ref, o_ref, xm_ref, *, tm, n_lat):
    i = pl.program_id(0)
    j = pl.program_id(1)

    @pl.when(j == 0)
    def _():
        xm_ref[...] = _modulate_rows(x_ref[...], mod_ref, 0, 1, i * tm, n_lat).astype(BF16)

    acc = _dot(xm_ref[...], w_ref[...])
    n_heads = acc.shape[1] // HEAD_DIM

    @pl.when(j == 0)
    def _():
        for c in range(n_heads):
            xs = acc[:, c * HEAD_DIM:(c + 1) * HEAD_DIM]
            xs = xs * lax.rsqrt(jnp.mean(xs * xs, axis=-1, keepdims=True) + LN_EPS)
            o_ref[:, c * HEAD_DIM:(c + 1) * HEAD_DIM] = _rot_lanes(xs, cos_ref[0], sin_ref[0]).astype(BF16)

    @pl.when(j == 1)
    def _():
        for c in range(n_heads):
            xs = acc[:, c * HEAD_DIM:(c + 1) * HEAD_DIM]
            o_ref[:, c * HEAD_DIM:(c + 1) * HEAD_DIM] = _rot_lanes(xs, cos_ref[1], sin_ref[1]).astype(BF16)


def _projq(x_all, mod, w_bf, cos2, sin2, n_lat):
    d = x_all.shape[1]
    n = w_bf.shape[1]
    tm = 512 if n_lat % 512 == 0 else 256
    tn = n // 2
    return pl.pallas_call(
        functools.partial(_projq_kernel, tm=tm, n_lat=n_lat),
        grid=(n_lat // tm, 2),
        in_specs=[
            pl.BlockSpec((tm, d), lambda i, j: (i, 0)),
            pl.BlockSpec((2, 6, d), lambda i, j: (0, 0, 0)),
            pl.BlockSpec((d, tn), lambda i, j: (0, j)),
            pl.BlockSpec((2, tm, HEAD_DIM), lambda i, j: (0, i, 0)),
            pl.BlockSpec((2, tm, HEAD_DIM), lambda i, j: (0, i, 0)),
        ],
        out_specs=pl.BlockSpec((tm, tn), lambda i, j: (i, j)),
        out_shape=jax.ShapeDtypeStruct((n_lat, n), BF16),
        scratch_shapes=[pltpu.VMEM((tm, d), BF16)],
        compiler_params=_cparams(2),
        name="proj1_q",
    )(x_all, mod, w_bf, cos2, sin2)


def _projkv_kernel(x_ref, mod_ref, wv_ref, wkt_ref, cos_ref, sin_ref, v_ref, kt_ref, *, tm, n_lat):
    i = pl.program_id(0)
    xm = _modulate_rows(x_ref[...], mod_ref, 0, 1, i * tm, n_lat).astype(BF16)
    v_ref[...] = _dot(xm, wv_ref[...]).astype(BF16)
    kt = _dot_nt(wkt_ref[...], xm)
    q = HEAD_DIM // 4
    n_gk = GQA_KV_HEADS * HEAD_DIM
    for c in range(kt.shape[0] // HEAD_DIM):
        xs = kt[c * HEAD_DIM:(c + 1) * HEAD_DIM, :]
        t = 0 if c * HEAD_DIM < n_gk else 1
        if t == 0:
            xs = xs * lax.rsqrt(jnp.mean(xs * xs, axis=0, keepdims=True) + LN_EPS)
        partner = jnp.concatenate([xs[q:2 * q], xs[0:q], xs[3 * q:4 * q], xs[2 * q:3 * q]], axis=0)
        kt_ref[c * HEAD_DIM:(c + 1) * HEAD_DIM, :] = (xs * cos_ref[t] + partner * sin_ref[t]).astype(BF16)


def _projkv(x_all, mod, wv_bf, wkt_bf, cos_t, sin_t, n_lat):
    m, d = x_all.shape
    nv = wv_bf.shape[1]
    nk = wkt_bf.shape[0]
    tm = 640 if m % 640 == 0 else 256
    return pl.pallas_call(
        functools.partial(_projkv_kernel, tm=tm, n_lat=n_lat),
        grid=(m // tm,),
        in_specs=[
            pl.BlockSpec((tm, d), lambda i: (i, 0)),
            _const_spec((2, 6, d)),
            _const_spec((d, nv)),
            _const_spec((nk, d)),
            pl.BlockSpec((2, HEAD_DIM, tm), lambda i: (0, 0, i)),
            pl.BlockSpec((2, HEAD_DIM, tm), lambda i: (0, 0, i)),
        ],
        out_specs=[
            pl.BlockSpec((tm, nv), lambda i: (i, 0)),
            pl.BlockSpec((nk, tm), lambda i: (0, i)),
        ],
        out_shape=[jax.ShapeDtypeStruct((m, nv), BF16), jax.ShapeDtypeStruct((nk, m), BF16)],
        compiler_params=_cparams(1),
        name="proj1_kv",
    )(x_all, mod, wv_bf, wkt_bf, cos_t, sin_t)


def _flash(qs, kt_ref, v_ref, m_ref, l_ref, acc_ref, tk, stack):
    rows = qs.shape[0]
    lk = v_ref.shape[0]
    m_ref[...] = jnp.full(m_ref.shape, -jnp.inf, F32)
    l_ref[...] = jnp.zeros(l_ref.shape, F32)
    acc_ref[...] = jnp.zeros(acc_ref.shape, F32)
    rs = rows // stack

    def step(j, carry):
        off = pl.multiple_of(j * tk, tk)
        if stack == 1:
            s = _dot(qs, kt_ref[:, pl.ds(off, tk)])
        else:
            s = jnp.concatenate(
                [_dot(qs[a * rs:(a + 1) * rs], kt_ref[a * HEAD_DIM:(a + 1) * HEAD_DIM, pl.ds(off, tk)])
                 for a in range(stack)], axis=0)
        m_prev = m_ref[...]
        m_next = jnp.maximum(m_prev, jnp.max(s, axis=1, keepdims=True))
        alpha = jnp.exp2(m_prev - m_next)
        p = jnp.exp2(s - m_next[:, 0:1])
        l_ref[...] = alpha * l_ref[...] + jnp.sum(p, axis=1, keepdims=True)
        acc_ref[...] = alpha[:, 0:1] * acc_ref[...] + _dot(p.astype(BF16), v_ref[pl.ds(off, tk), :])
        m_ref[...] = m_next
        return carry

    lax.fori_loop(0, lk // tk, step, 0)
    return acc_ref[...] / l_ref[:, 0:1]


def _gqa_kernel(q_ref, kt_ref, v_ref, o_ref, m_ref, l_ref, acc_ref, *, tk):
    tq = q_ref.shape[0]
    qs = jnp.concatenate([q_ref[:, a * HEAD_DIM:(a + 1) * HEAD_DIM] for a in range(GQA_GROUP)], axis=0)
    o = _flash(qs, kt_ref, v_ref, m_ref, l_ref, acc_ref, tk, 1)
    for a in range(GQA_GROUP):
        o_ref[:, a * HEAD_DIM:(a + 1) * HEAD_DIM] = o[a * tq:(a + 1) * tq].astype(BF16)


def _diff_kernel(lam_ref, q_ref, kt_ref, v_ref, g_ref, o_ref, m_ref, l_ref, acc_ref, *, tk, out_scale):
    tq = q_ref.shape[0]
    qs = jnp.concatenate([q_ref[:, 0:HEAD_DIM], q_ref[:, HEAD_DIM:2 * HEAD_DIM]], axis=0)
    o = _flash(qs, kt_ref, v_ref, m_ref, l_ref, acc_ref, tk, 2)
    od = o[0:tq] - lam_ref[0] * o[tq:2 * tq]
    od = od * lax.rsqrt(jnp.mean(od * od, axis=-1, keepdims=True) + LN_EPS)
    o_ref[...] = (od * g_ref[...] * out_scale).astype(BF16)


def _key_chunk(lk):
    for tk in (1280, 1024, 640, 512, 256, 128):
        if lk % tk == 0 and lk // tk >= 2:
            return tk
    return lk


def _attention(q1, kt, v1, lam, subln_g, lambda_init, n_lat):
    lk = v1.shape[0]
    tk = _key_chunk(lk)
    tq = 256
    gw = GQA_GROUP * HEAD_DIM
    n_gv = GQA_KV_HEADS * HEAD_DIM
    og = pl.pallas_call(
        functools.partial(_gqa_kernel, tk=tk),
        grid=(GQA_KV_HEADS, n_lat // tq),
        in_specs=[
            pl.BlockSpec((tq, gw), lambda g, i: (i, g)),
            pl.BlockSpec((HEAD_DIM, lk), lambda g, i: (g, 0)),
            pl.BlockSpec((lk, HEAD_DIM), lambda g, i: (0, g)),
        ],
        out_specs=pl.BlockSpec((tq, gw), lambda g, i: (i, g)),
        out_shape=jax.ShapeDtypeStruct((n_lat, GQA_HEADS * HEAD_DIM), BF16),
        scratch_shapes=[pltpu.VMEM((GQA_GROUP * tq, LANES), F32), pltpu.VMEM((GQA_GROUP * tq, LANES), F32),
                        pltpu.VMEM((GQA_GROUP * tq, HEAD_DIM), F32)],
        compiler_params=_cparams(2),
        name="gqa_attention",
    )(q1, kt, v1)
    q_off = GQA_HEADS * HEAD_DIM // DIFF_V_DIM
    kv_off = n_gv // DIFF_V_DIM
    od = pl.pallas_call(
        functools.partial(_diff_kernel, tk=tk, out_scale=1.0 - lambda_init),
        grid=(DIFF_HEADS, n_lat // tq),
        in_specs=[
            pl.BlockSpec(memory_space=pltpu.SMEM),
            pl.BlockSpec((tq, DIFF_V_DIM), lambda h, i: (i, q_off + h)),
            pl.BlockSpec((DIFF_V_DIM, lk), lambda h, i: (kv_off + h, 0)),
            pl.BlockSpec((lk, DIFF_V_DIM), lambda h, i: (0, kv_off + h)),
            pl.BlockSpec((1, DIFF_V_DIM), lambda h, i: (0, 0)),
        ],
        out_specs=pl.BlockSpec((tq, DIFF_V_DIM), lambda h, i: (i, h)),
        out_shape=jax.ShapeDtypeStruct((n_lat, DIFF_HEADS * DIFF_V_DIM), BF16),
        scratch_shapes=[pltpu.VMEM((2 * tq, LANES), F32), pltpu.VMEM((2 * tq, LANES), F32),
                        pltpu.VMEM((2 * tq, DIFF_V_DIM), F32)],
        compiler_params=_cparams(2),
        name="diff_attention",
    )(lam, q1, kt, v1, subln_g.reshape(1, DIFF_V_DIM).astype(F32))
    return jnp.concatenate([og, od], axis=1)


def _router_operand(router_w):
    d, e = router_w.shape
    return jnp.concatenate([router_w.astype(F32), jnp.zeros((d, LANES - e), F32)], axis=1)


def kernel(x, c, ctx, c_ctx, mod_w, mod_b, ln_g, ln_b, ev_w_in, ev_w_out, ret_log_decay, s5_a_re, s5_a_im, s5_log_dt, s5_b_re, s5_b_im, s5_c_re, s5_c_im, s5_d, s5_w_glu, s5_b_glu, od_w_in, od_w_out, qk_norm_g, diff_lambda, diff_subln_g, router_w, router_bias, exp_w_gate, exp_w_up, exp_w_down, sh_w_gate, sh_w_up, sh_w_down):
    b_, n_lat, d = x.shape
    n_ctx = ctx.shape[1]
    assert b_ == 1 and d == D_MODEL and n_lat % RET_CHUNK == 0 and n_ctx % RET_CHUNK == 0
    m_all = n_lat + n_ctx
    x_all = jnp.concatenate([x[0], ctx[0]], axis=0)
    mods = _modulation(c, c_ctx, mod_w, mod_b)

    cos0, sin0 = _rope_tables(n_lat, n_ctx, RET_DIM)
    p0 = _proj0(x_all, mods[0], ev_w_in[0].astype(BF16), cos0, sin0, n_lat)
    rn = _retention(p0, ret_log_decay[0], n_lat, n_ctx)
    s5_tabs = functools.partial(_s5_tables, s5_a_re[0], s5_a_im[0], s5_log_dt[0], s5_b_re[0], s5_b_im[0],
                                s5_c_re[0], s5_c_im[0], s5_d[0])
    s5y = _s5(p0[:, 4 * RET_HEADS * RET_DIM:], s5_tabs, n_lat, n_ctx)
    ln0 = jnp.stack([ln_g[0], ln_b[0]], axis=1)
    x_all, xp, logits = _finish0(rn, p0, s5y, x_all, mods[0], ln0[0], s5_w_glu[0].astype(BF16),
                                 s5_b_glu[0].reshape(1, -1).astype(F32), ev_w_out[0].astype(BF16),
                                 _router_operand(router_w[0]), n_lat)
    x_all = _moe(xp, logits, x_all, mods[0], ln0[1], router_bias[0], exp_w_gate[0], exp_w_up[0], exp_w_down[0],
                 sh_w_gate[0], sh_w_up[0], sh_w_down[0], n_lat)

    i = 1
    lambda_init = 0.8 - 0.6 * math.exp(-0.3 * i)
    gq_w = GQA_HEADS * HEAD_DIM
    gk_w = GQA_KV_HEADS * HEAD_DIM
    dq_w = DIFF_HEADS * 2 * HEAD_DIM
    cuts = [gq_w, gq_w + gk_w, gq_w + 2 * gk_w, gq_w + 2 * gk_w + dq_w, gq_w + 2 * gk_w + 2 * dq_w]
    w_in = od_w_in[0]
    w_gq, w_gk, w_gv, w_dq, w_dk, w_dv = (w_in[:, a:b] for a, b in zip([0] + cuts, cuts + [w_in.shape[1]]))
    cos1, sin1 = _rope_tables(n_lat, n_ctx, HEAD_DIM)
    qscale = HEAD_DIM ** -0.5 * LOG2E
    gq_gain = qk_norm_g[0, 0].astype(F32)
    gk_gain = qk_norm_g[0, 1].astype(F32)
    quarter = HEAD_DIM // 4

    def partner_gain(g):
        return jnp.concatenate([g[quarter:2 * quarter], g[0:quarter], g[3 * quarter:], g[2 * quarter:3 * quarter]])

    cos_q = jnp.stack([cos1[:n_lat] * gq_gain[None, :], cos1[:n_lat]]) * qscale
    sin_q = jnp.stack([sin1[:n_lat] * partner_gain(gq_gain)[None, :], sin1[:n_lat]]) * qscale
    cos_k = jnp.stack([cos1.T * gk_gain[:, None], cos1.T])
    sin_k = jnp.stack([sin1.T * partner_gain(gk_gain)[:, None], sin1.T])
    q1 = _projq(x_all, mods[1], jnp.concatenate([w_gq, w_dq], axis=1).astype(BF16), cos_q, sin_q, n_lat)
    v1, kt = _projkv(x_all, mods[1], jnp.concatenate([w_gv, w_dv], axis=1).astype(BF16),
                     jnp.concatenate([w_gk, w_dk], axis=1).T.astype(BF16), cos_k, sin_k, n_lat)
    lf = diff_lambda[0].astype(F32)
    lam = (jnp.exp(jnp.sum(lf[0] * lf[1])) - jnp.exp(jnp.sum(lf[2] * lf[3])) + lambda_init).reshape(1)
    att = _attention(q1, kt, v1, lam, diff_subln_g[0], lambda_init, n_lat)
    ln1 = jnp.stack([ln_g[1], ln_b[1]], axis=1)
    x_lat, xp, logits = _finish1(att, x_all, mods[1], ln1[0], od_w_out[0].astype(BF16),
                                 _router_operand(router_w[1]), n_lat)
    x_lat = _moe(xp, logits, x_lat, mods[1], ln1[1], router_bias[1], exp_w_gate[1], exp_w_up[1], exp_w_down[1],
                 sh_w_gate[1], sh_w_up[1], sh_w_down[1], n_lat)
    return x_lat[None]
```

```python
import functools
import math

import jax
import jax.numpy as jnp
from jax import lax
from jax.experimental import pallas as pl
from jax.experimental.pallas import tpu as pltpu

F32 = jnp.float32
BF16 = jnp.bfloat16
U32 = jnp.uint32
I32 = jnp.int32

D_MODEL = 2048
DEPTH = 2
GRID_W = 64
ROPE_BASE = 10000.0
LN_EPS = 1e-6
DEEPNORM_ALPHA = (2 * DEPTH) ** 0.25
RET_HEADS = 4
RET_DIM = 256
S5_CHANNELS = D_MODEL // 2
S5_GROUP = 16
S5_GROUPS = S5_CHANNELS // S5_GROUP
S5_STATE = 64
HEAD_DIM = 128
GQA_HEADS = 8
GQA_KV_HEADS = 2
GQA_GROUP = GQA_HEADS // GQA_KV_HEADS
DIFF_HEADS = 4
DIFF_V_DIM = 2 * HEAD_DIM
N_EXPERTS = 64
TOP_K = 8
N_EXPERT_GROUPS = 8
TOPK_GROUPS = 4
EXPERT_HIDDEN = 512
ROUTED_SCALE = 2.5
LOG2E = 1.4426950408889634

LANES = 128
VMEM_LIMIT = 56 * 1024 * 1024

RET_CHUNK = 256
S5_STEP = 16
MOE_TM = 256
ROW_TILE = 256
FLASH_ROWS = 64
FLASH_BLOCK = 512


def _cparams(n_axes):
    return pltpu.CompilerParams(dimension_semantics=("arbitrary",) * n_axes, vmem_limit_bytes=VMEM_LIMIT)


def _dot(a, b):
    return jnp.dot(a, b, preferred_element_type=F32)


def _dot_nt(a, b):
    return lax.dot_general(a, b, (((1,), (1,)), ((), ())), preferred_element_type=F32)


def _dot_tn(a, b):
    return lax.dot_general(a, b, (((0,), (0,)), ((), ())), preferred_element_type=F32)


def _split(a):
    hi = a.astype(BF16)
    lo = (a - hi.astype(F32)).astype(BF16)
    return hi, lo


def _dot3(a, b):
    ah, al = _split(a)
    bh, bl = _split(b)
    return _dot(ah, bh) + _dot(al, bh) + _dot(ah, bl)


def _sigmoid(x):
    return 1.0 / (1.0 + jnp.exp(-x))


def _silu(x):
    return x * _sigmoid(x)


def _gelu_tanh(x):
    return 0.5 * x * (1.0 + jnp.tanh(math.sqrt(2.0 / math.pi) * (x + 0.044715 * (x * x * x))))


def _layer_norm(h, g, b):
    mu = jnp.mean(h, axis=-1, keepdims=True)
    d = h - mu
    var = jnp.mean(d * d, axis=-1, keepdims=True)
    return d * lax.rsqrt(var + LN_EPS) * g + b


def _pack_bf16_pairs(x):
    n = x.shape[1] // 2
    lo = lax.bitcast_convert_type(x[:, :n].astype(BF16).astype(F32), U32)
    hi = lax.bitcast_convert_type(x[:, n:].astype(BF16).astype(F32), U32)
    return (lo >> 16) | (hi & jnp.uint32(0xFFFF0000))


def _unpack_bf16_pairs(w):
    lo = lax.bitcast_convert_type(w << 16, F32)
    hi = lax.bitcast_convert_type(w & jnp.uint32(0xFFFF0000), F32)
    return lo, hi


def _modulate_rows(x, mod_ref, shift_row, scale_row, row0, n_lat):
    rows = row0 + lax.broadcasted_iota(I32, (x.shape[0], 1), 0)
    is_ctx = rows >= n_lat
    sc = jnp.where(is_ctx, mod_ref[1, scale_row:scale_row + 1, :], mod_ref[0, scale_row:scale_row + 1, :])
    sh = jnp.where(is_ctx, mod_ref[1, shift_row:shift_row + 1, :], mod_ref[0, shift_row:shift_row + 1, :])
    return x * (1.0 + sc) + sh


def _select_rows(mod_ref, row, row0, n_rows, n_lat):
    rows = row0 + lax.broadcasted_iota(I32, (n_rows, 1), 0)
    return jnp.where(rows >= n_lat, mod_ref[1, row:row + 1, :], mod_ref[0, row:row + 1, :])


def _mod_kernel(v_ref, w_ref, b_ref, o_ref):
    v = v_ref[...]
    o_ref[0] = _dot3(_silu(v), w_ref[0]) + b_ref[0]


def _modulation(c, c_ctx, mod_w, mod_b):
    depth, d, n = mod_w.shape
    v = jnp.concatenate([c[:1], c_ctx[None], jnp.zeros((6, d), F32)], axis=0)
    tn = 512
    out = pl.pallas_call(
        _mod_kernel,
        grid=(depth, n // tn),
        in_specs=[
            pl.BlockSpec((8, d), lambda i, j: (0, 0)),
            pl.BlockSpec((1, d, tn), lambda i, j: (i, 0, j)),
            pl.BlockSpec((1, 1, tn), lambda i, j: (i, 0, j)),
        ],
        out_specs=pl.BlockSpec((1, 8, tn), lambda i, j: (i, 0, j)),
        out_shape=jax.ShapeDtypeStruct((depth, 8, n), F32),
        compiler_params=_cparams(2),
        name="modulation",
    )(v, mod_w, mod_b.reshape(depth, 1, n))
    return out[:, :2].reshape(depth, 2, 6, d)


def _rope_tables(n_lat, n_ctx, head_dim):
    q = head_dim // 4
    t = jnp.arange(n_lat, dtype=I32)
    row = (t // GRID_W).astype(F32)
    col = (t % GRID_W).astype(F32)
    freqs = ROPE_BASE ** (-jnp.arange(q, dtype=F32) / q)
    ar = row[:, None] * freqs[None, :]
    ac = col[:, None] * freqs[None, :]
    cos = jnp.concatenate([jnp.cos(ar), jnp.cos(ar), jnp.cos(ac), jnp.cos(ac)], axis=-1)
    sin = jnp.concatenate([-jnp.sin(ar), jnp.sin(ar), -jnp.sin(ac), jnp.sin(ac)], axis=-1)
    cos = jnp.concatenate([cos, jnp.ones((n_ctx, head_dim), F32)], axis=0)
    sin = jnp.concatenate([sin, jnp.zeros((n_ctx, head_dim), F32)], axis=0)
    return cos, sin


def _proj0_kernel(x_ref, mod_ref, w_ref, cos_ref, sin_ref, o_ref, xm_ref, *, tm, n_lat):
    i = pl.program_id(0)
    j = pl.program_id(1)

    @pl.when(j == 0)
    def _():
        xm_ref[...] = _modulate_rows(x_ref[...], mod_ref, 0, 1, i * tm, n_lat).astype(BF16)

    acc = _dot(xm_ref[...], w_ref[...])

    @pl.when(j >= 2)
    def _():
        o_ref[...] = acc.astype(BF16)

    @pl.when(j < 2)
    def _():
        scale = jnp.where(j == 0, RET_DIM ** -0.5, 1.0).astype(F32)
        for c in range(acc.shape[1] // LANES):
            xs = acc[:, c * LANES:(c + 1) * LANES]
            t0 = (c % 2) * LANES
            rot = xs * cos_ref[:, t0:t0 + LANES] + pltpu.roll(xs, LANES // 2, 1) * sin_ref[:, t0:t0 + LANES]
            o_ref[:, c * LANES:(c + 1) * LANES] = (rot * scale).astype(BF16)


def _proj0(x_all, mod, w_bf, cos, sin, n_lat):
    m, d = x_all.shape
    n = w_bf.shape[1]
    tm = 640 if m % 640 == 0 else 256
    tn = 1024
    return pl.pallas_call(
        functools.partial(_proj0_kernel, tm=tm, n_lat=n_lat),
        grid=(m // tm, n // tn),
        in_specs=[
            pl.BlockSpec((tm, d), lambda i, j: (i, 0)),
            pl.BlockSpec((2, 6, d), lambda i, j: (0, 0, 0)),
            pl.BlockSpec((d, tn), lambda i, j: (0, j)),
            pl.BlockSpec((tm, RET_DIM), lambda i, j: (i, 0)),
            pl.BlockSpec((tm, RET_DIM), lambda i, j: (i, 0)),
        ],
        out_specs=pl.BlockSpec((tm, tn), lambda i, j: (i, j)),
        out_shape=jax.ShapeDtypeStruct((m, n), BF16),
        scratch_shapes=[pltpu.VMEM((tm, d), BF16)],
        compiler_params=_cparams(2),
        name="proj0",
    )(x_all, mod, w_bf, cos, sin)


def _ret_tables(log_decay):
    c = RET_CHUNK
    lg = log_decay.astype(F32)
    idx = jnp.arange(c, dtype=F32)
    diff = idx[:, None] - idx[None, :]
    lower = jnp.where(diff >= 0, jnp.exp(jnp.maximum(diff, 0.0)[None] * lg[0][:, None, None]), 0.0)
    upper = jnp.where(diff <= 0, jnp.exp(jnp.maximum(-diff, 0.0)[None] * lg[1][:, None, None]), 0.0)
    mask = lower + upper
    ones = jnp.ones((1, 1, RET_DIM), F32)
    kdec_f = jnp.exp((c - 1 - idx)[None, :] * lg[0][:, None])[:, :, None] * ones
    qdec_f = jnp.exp((idx + 1)[None, :] * lg[0][:, None])[:, :, None] * ones
    kdec_b = jnp.exp(idx[None, :] * lg[1][:, None])[:, :, None] * ones
    qdec_b = jnp.exp((c - idx)[None, :] * lg[1][:, None])[:, :, None] * ones
    g_chunk = jnp.exp(c * lg)
    return mask, kdec_f, qdec_f, kdec_b, qdec_b, g_chunk


def _ret_bwd_state_kernel(gc_ref, k_ref, v_ref, kdec_ref, sb_ref, s_ref):
    h = pl.program_id(0)
    s = pl.program_id(1)

    @pl.when(s == 0)
    def _():
        s_ref[...] = jnp.zeros_like(s_ref)

    sb_ref[0, 0] = s_ref[...].astype(BF16)
    kd = (k_ref[...].astype(F32) * kdec_ref[0]).astype(BF16)
    s_ref[...] = gc_ref[1, h] * s_ref[...] + _dot_tn(kd, v_ref[...])


def _ret_out_kernel(gc_ref, q_ref, k_ref, v_ref, sb_ref, mask_ref, qdf_ref, qdb_ref, kdf_ref, o_ref, s_ref):
    h = pl.program_id(0)
    s = pl.program_id(1)

    @pl.when(s == 0)
    def _():
        s_ref[...] = jnp.zeros_like(s_ref)

    q = q_ref[...]
    k = k_ref[...]
    v = v_ref[...]
    qf = q.astype(F32)
    w = (_dot_nt(q, k) * mask_ref[0]).astype(BF16)
    o = _dot(w, v)
    o = o + _dot((qf * qdf_ref[0]).astype(BF16), s_ref[...].astype(BF16))
    o = o + _dot((qf * qdb_ref[0]).astype(BF16), sb_ref[0, 0])
    mu = jnp.mean(o, axis=-1, keepdims=True)
    d = o - mu
    var = jnp.mean(d * d, axis=-1, keepdims=True)
    o_ref[...] = (d * lax.rsqrt(var + LN_EPS)).astype(BF16)
    kd = (k.astype(F32) * kdf_ref[0]).astype(BF16)
    s_ref[...] = gc_ref[0, h] * s_ref[...] + _dot_tn(kd, v)


def _retention(p0, log_decay, n_lat, n_ctx):
    m = p0.shape[0]
    c = RET_CHUNK
    nc = m // c
    nlc = n_lat // c
    hh = RET_HEADS
    mask, kdec_f, qdec_f, kdec_b, qdec_b, g_chunk = _ret_tables(log_decay)
    smem = pl.BlockSpec(memory_space=pltpu.SMEM)

    def bchunk(s):
        return nc - 1 - s

    sb = pl.pallas_call(
        _ret_bwd_state_kernel,
        grid=(hh, nc),
        in_specs=[
            smem,
            pl.BlockSpec((c, RET_DIM), lambda h, s: (bchunk(s), hh + h)),
            pl.BlockSpec((c, RET_DIM), lambda h, s: (bchunk(s), 2 * hh + h)),
            pl.BlockSpec((1, c, RET_DIM), lambda h, s: (h, 0, 0)),
        ],
        out_specs=pl.BlockSpec((1, 1, RET_DIM, RET_DIM), lambda h, s: (h, bchunk(s), 0, 0)),
        out_shape=jax.ShapeDtypeStruct((hh, nc, RET_DIM, RET_DIM), BF16),
        scratch_shapes=[pltpu.VMEM((RET_DIM, RET_DIM), F32)],
        compiler_params=_cparams(2),
        name="ret_bwd_state",
    )(g_chunk, p0, p0, kdec_b)

    def fchunk(s):
        return (s + nlc) % nc

    return pl.pallas_call(
        _ret_out_kernel,
        grid=(hh, nc),
        in_specs=[
            smem,
            pl.BlockSpec((c, RET_DIM), lambda h, s: (fchunk(s), h)),
            pl.BlockSpec((c, RET_DIM), lambda h, s: (fchunk(s), hh + h)),
            pl.BlockSpec((c, RET_DIM), lambda h, s: (fchunk(s), 2 * hh + h)),
            pl.BlockSpec((1, 1, RET_DIM, RET_DIM), lambda h, s: (h, fchunk(s), 0, 0)),
            pl.BlockSpec((1, c, c), lambda h, s: (h, 0, 0)),
            pl.BlockSpec((1, c, RET_DIM), lambda h, s: (h, 0, 0)),
            pl.BlockSpec((1, c, RET_DIM), lambda h, s: (h, 0, 0)),
            pl.BlockSpec((1, c, RET_DIM), lambda h, s: (h, 0, 0)),
        ],
        out_specs=pl.BlockSpec((c, RET_DIM), lambda h, s: (fchunk(s), h)),
        out_shape=jax.ShapeDtypeStruct((m, hh * RET_DIM), BF16),
        scratch_shapes=[pltpu.VMEM((RET_DIM, RET_DIM), F32)],
        compiler_params=_cparams(2),
        name="ret_out",
    )(g_chunk, p0, p0, p0, sb, mask, qdec_f, qdec_b, kdec_f)


def _s5_tables(a_re, a_im, log_dt, b_re, b_im, c_re, c_im, d_skip, n_scan):
    f32 = F32
    st = S5_STEP
    g_, p_, s_ = S5_GROUPS, S5_STATE, S5_GROUP
    b_re, b_im, c_re, c_im = (t.astype(f32) for t in (b_re, b_im, c_re, c_im))
    ws, vs, aps, ks = [], [], [], []
    for direction in range(2):
        are, aim = a_re[direction].astype(f32), a_im[direction].astype(f32)
        dt = jnp.exp(log_dt[direction].astype(f32))[:, None]
        zr, zi = are * dt, aim * dt
        mag = jnp.exp(zr)
        ab_re, ab_im = mag * jnp.cos(zi), mag * jnp.sin(zi)
        den = jnp.square(are) + jnp.square(aim)
        nr, ni = ab_re - 1.0, ab_im
        f_re = (nr * are + ni * aim) / den
        f_im = (ni * are - nr * aim) / den
        bb_re = f_re[..., None] * b_re - f_im[..., None] * b_im
        bb_im = f_re[..., None] * b_im + f_im[..., None] * b_re
        pr, pi = [jnp.ones_like(ab_re)], [jnp.zeros_like(ab_im)]
        for _ in range(st):
            pr.append(pr[-1] * ab_re - pi[-1] * ab_im)
            pi.append(pr[-2] * ab_im + pi[-1] * ab_re)
        pw_re, pw_im = jnp.stack(pr), jnp.stack(pi)
        ca_re = c_re[None] * pw_re[:, :, None, :] - c_im[None] * pw_im[:, :, None, :]
        ca_im = c_re[None] * pw_im[:, :, None, :] + c_im[None] * pw_re[:, :, None, :]
        kk = (jnp.einsum('tgip,gpj->tgij', ca_re[:st], bb_re, precision='highest')
              - jnp.einsum('tgip,gpj->tgij', ca_im[:st], bb_im, precision='highest'))
        ks.append(kk)
        e = (st - 1 - jnp.arange(st)) if direction == 0 else jnp.arange(st)
        w_re = pw_re[e][:, :, :, None] * bb_re[None] - pw_im[e][:, :, :, None] * bb_im[None]
        w_im = pw_re[e][:, :, :, None] * bb_im[None] + pw_im[e][:, :, :, None] * bb_re[None]
        w = jnp.concatenate([w_re, w_im], axis=2)
        ws.append(jnp.transpose(w, (1, 0, 3, 2)).reshape(g_, st * s_, 2 * p_))
        e2 = (jnp.arange(st) + 1) if direction == 0 else (st - jnp.arange(st))
        v = jnp.concatenate([ca_re[e2], -ca_im[e2]], axis=3)
        vs.append(jnp.transpose(v, (1, 3, 0, 2)).reshape(g_, 2 * p_, st * s_))
        qr, qi = pw_re[st], pw_im[st]
        rows = []
        for _ in range(n_scan):
            rows.append(jnp.stack([jnp.concatenate([qr, qr], -1), jnp.concatenate([-qi, qi], -1)], axis=1))
            qr, qi = qr * qr - qi * qi, 2.0 * qr * qi
        aps.append(jnp.stack(rows, axis=1))
    tt = jnp.arange(st)
    lag = tt[None, :] - tt[:, None]
    kf = ks[0][jnp.clip(lag, 0, st - 1)]
    kb = ks[1][jnp.clip(-lag, 0, st - 1)]
    tm = jnp.where((lag >= 0)[:, :, None, None, None], kf, 0.0) + jnp.where((lag <= 0)[:, :, None, None, None], kb, 0.0)
    tm = jnp.transpose(tm, (2, 0, 4, 1, 3)).reshape(g_, st * s_, st * s_)
    dsk = d_skip.astype(f32).reshape(g_, s_)
    tm = tm + jnp.eye(st * s_, dtype=f32)[None] * jnp.tile(dsk, (1, st))[:, None, :]
    return tm, jnp.stack(ws), jnp.stack(vs), jnp.stack(aps)


def _s5_kernel(u_ref, t_ref, w_ref, v_ref, a_ref, y_ref, ha_ref, hb_ref, *, ncs, nls, n_scan):
    n = ncs + nls + ncs
    u = u_ref[0]

    def mm(x, mat):
        mh, ml = _split(mat)
        return _dot(x, mh) + _dot(x, ml)

    def mm3(x, mat):
        xh, xl = _split(x)
        mh, ml = _split(mat)
        return _dot(xh, mh) + _dot(xl, mh) + _dot(xh, ml)

    rows = lax.broadcasted_iota(I32, (n, 1), 0)

    def scan(z, direction):
        bufs = (ha_ref, hb_ref)
        bufs[0][...] = z
        for kk in range(n_scan):
            src, dst = bufs[kk % 2], bufs[(kk + 1) % 2]
            sft = 1 << kk
            a1 = a_ref[direction, 0, kk, 0:1, :]
            a2 = a_ref[direction, 0, kk, 1:2, :]
            if sft >= n:
                dst[...] = src[...]
                continue
            if sft % 8 == 0:
                if direction == 0:
                    prev = src[0:n - sft, :]
                    dst[0:sft, :] = src[0:sft, :]
                    dst[sft:n, :] = src[sft:n, :] + a1 * prev + a2 * pltpu.roll(prev, S5_STATE, 1)
                else:
                    nxt = src[sft:n, :]
                    dst[n - sft:n, :] = src[n - sft:n, :]
                    dst[0:n - sft, :] = src[0:n - sft, :] + a1 * nxt + a2 * pltpu.roll(nxt, S5_STATE, 1)
            else:
                cur = src[...]
                if direction == 0:
                    sh = jnp.where(rows >= sft, pltpu.roll(cur, sft, 0), 0.0)
                else:
                    sh = jnp.where(rows < n - sft, pltpu.roll(cur, n - sft, 0), 0.0)
                dst[...] = cur + a1 * sh + a2 * pltpu.roll(sh, S5_STATE, 1)
        return bufs[n_scan % 2][...]

    y = mm(u, t_ref[0])
    hf = scan(mm(u, w_ref[0, 0]), 0)
    hf_prev = jnp.where(rows >= 1, pltpu.roll(hf, 1, 0), 0.0)
    yf = mm3(hf_prev, v_ref[0, 0])
    hb = scan(mm(u, w_ref[1, 0]), 1)
    hb_next = jnp.where(rows < n - 1, pltpu.roll(hb, n - 1, 0), 0.0)
    yb = mm3(hb_next, v_ref[1, 0])
    y_ref[0, 0:nls, :] = (y + yf + yb)[ncs:ncs + nls, :]
    y_ref[0, nls:nls + ncs, :] = (y + yf)[0:ncs, :] + yb[ncs + nls:n, :]


def _s5(u, tabs, n_lat, n_ctx):
    st = S5_STEP
    g_, s_ = S5_GROUPS, S5_GROUP
    nls, ncs = n_lat // st, n_ctx // st
    n = ncs + nls + ncs
    n_scan = max(1, (n - 1).bit_length())
    t_mat, w_mat, v_mat, apow = tabs(n_scan)
    ug = jnp.transpose(u.reshape(nls + ncs, st, g_, s_), (2, 0, 1, 3)).reshape(g_, nls + ncs, st * s_)
    useq = jnp.concatenate([ug[:, nls:], ug[:, :nls], ug[:, nls:]], axis=1)
    tw = st * s_
    y = pl.pallas_call(
        functools.partial(_s5_kernel, ncs=ncs, nls=nls, n_scan=n_scan),
        grid=(g_,),
        in_specs=[
            pl.BlockSpec((1, n, tw), lambda g: (g, 0, 0)),
            pl.BlockSpec((1, tw, tw), lambda g: (g, 0, 0)),
            pl.BlockSpec((2, 1, tw, 2 * S5_STATE), lambda g: (0, g, 0, 0)),
            pl.BlockSpec((2, 1, 2 * S5_STATE, tw), lambda g: (0, g, 0, 0)),
            pl.BlockSpec((2, 1, n_scan, 2, 2 * S5_STATE), lambda g: (0, g, 0, 0, 0)),
        ],
        out_specs=pl.BlockSpec((1, nls + ncs, tw), lambda g: (g, 0, 0)),
        out_shape=jax.ShapeDtypeStruct((g_, nls + ncs, tw), F32),
        scratch_shapes=[pltpu.VMEM((n, 2 * S5_STATE), F32), pltpu.VMEM((n, 2 * S5_STATE), F32)],
        compiler_params=_cparams(1),
        name="s5",
    )(useq, t_mat, w_mat, v_mat, apow)
    return jnp.transpose(y.reshape(g_, nls + ncs, st, s_), (1, 2, 0, 3)).reshape(n_lat + n_ctx, g_ * s_)


def _post_mix(y, x_ref, mod_ref, ln_ref, rw_ref, xo_ref, xp_ref, lg_ref, row0, n_lat):
    tm = y.shape[0]
    g1 = _select_rows(mod_ref, 2, row0, tm, n_lat)
    xn = _layer_norm(DEEPNORM_ALPHA * x_ref[...] + g1 * y, ln_ref[0:1, :], ln_ref[1:2, :])
    xo_ref[...] = xn
    sc2 = _select_rows(mod_ref, 4, row0, tm, n_lat)
    sh2 = _select_rows(mod_ref, 3, row0, tm, n_lat)
    x2 = xn * (1.0 + sc2) + sh2
    xp_ref[...] = _pack_bf16_pairs(x2)
    xh, xl = _split(x2)
    rh, rl = _split(rw_ref[...])
    lg_ref[...] = _dot_nt(rh, xh) + _dot_nt(rh, xl) + _dot_nt(rl, xh)


def _finish0_kernel(r_ref, g_ref, s_ref, x_ref, mod_ref, ln_ref, wglu_ref, bglu_ref, wout_ref, rw_ref,
                    xo_ref, xp_ref, lg_ref, *, tm, n_lat):
    row0 = pl.program_id(0) * tm
    ret = (r_ref[...].astype(F32) * _silu(g_ref[...].astype(F32))).astype(BF16)
    z = _gelu_tanh(s_ref[...])
    zb = z.astype(BF16)
    gate = _sigmoid(_dot(zb, wglu_ref[...]) + bglu_ref[...])
    s5o = (z * gate).astype(BF16)
    half = ret.shape[1]
    y = _dot(ret, wout_ref[0:half, :]) + _dot(s5o, wout_ref[half:, :])
    _post_mix(y, x_ref, mod_ref, ln_ref, rw_ref, xo_ref, xp_ref, lg_ref, row0, n_lat)


def _const_spec(shape):
    nd = len(shape)
    return pl.BlockSpec(shape, lambda i: (0,) * nd)


def _finish0(rn, p0, s5y, x_all, mod, ln, wglu, bglu, wout, rw, n_lat):
    m, d = x_all.shape
    tm = ROW_TILE
    half = d // 2
    outs = pl.pallas_call(
        functools.partial(_finish0_kernel, tm=tm, n_lat=n_lat),
        grid=(m // tm,),
        in_specs=[
            pl.BlockSpec((tm, half), lambda i: (i, 0)),
            pl.BlockSpec((tm, half), lambda i: (i, 3)),
            pl.BlockSpec((tm, half), lambda i: (i, 0)),
            pl.BlockSpec((tm, d), lambda i: (i, 0)),
            _const_spec((2, 6, d)),
            _const_spec((2, d)),
            _const_spec((half, half)),
            _const_spec((1, half)),
            _const_spec((d, d)),
            _const_spec((N_EXPERTS, d)),
        ],
        out_specs=[
            pl.BlockSpec((tm, d), lambda i: (i, 0)),
            pl.BlockSpec((tm, half), lambda i: (i, 0)),
            pl.BlockSpec((N_EXPERTS, tm), lambda i: (0, i)),
        ],
        out_shape=[
            jax.ShapeDtypeStruct((m, d), F32),
            jax.ShapeDtypeStruct((m, half), U32),
            jax.ShapeDtypeStruct((N_EXPERTS, m), F32),
        ],
        compiler_params=_cparams(1),
        name="finish0",
    )(rn, p0, s5y, x_all, mod, ln, wglu, bglu, wout, rw)
    return outs


def _finish1_kernel(og_ref, od_ref, x_ref, mod_ref, ln_ref, wout_ref, rw_ref, xo_ref, xp_ref, lg_ref, *, tm, n_lat):
    row0 = pl.program_id(0) * tm
    half = og_ref.shape[1]
    y = _dot(og_ref[...], wout_ref[0:half, :]) + _dot(od_ref[...], wout_ref[half:, :])
    _post_mix(y, x_ref, mod_ref, ln_ref, rw_ref, xo_ref, xp_ref, lg_ref, row0, n_lat)


def _finish1(og, od, x_all, mod, ln, wout, rw, n_lat):
    d = x_all.shape[1]
    m = og.shape[0]
    tm = ROW_TILE
    half = d // 2
    return pl.pallas_call(
        functools.partial(_finish1_kernel, tm=tm, n_lat=n_lat),
        grid=(m // tm,),
        in_specs=[
            pl.BlockSpec((tm, half), lambda i: (i, 0)),
            pl.BlockSpec((tm, half), lambda i: (i, 0)),
            pl.BlockSpec((tm, d), lambda i: (i, 0)),
            _const_spec((2, 6, d)),
            _const_spec((2, d)),
            _const_spec((d, d)),
            _const_spec((N_EXPERTS, d)),
        ],
        out_specs=[
            pl.BlockSpec((tm, d), lambda i: (i, 0)),
            pl.BlockSpec((tm, half), lambda i: (i, 0)),
            pl.BlockSpec((N_EXPERTS, tm), lambda i: (0, i)),
        ],
        out_shape=[
            jax.ShapeDtypeStruct((m, d), F32),
            jax.ShapeDtypeStruct((m, half), U32),
            jax.ShapeDtypeStruct((N_EXPERTS, m), F32),
        ],
        compiler_params=_cparams(1),
        name="finish1",
    )(og, od, x_all, mod, ln, wout, rw)


def _route_kernel(lg_ref, bias_ref, e_ref, r_ref, w_ref, cnt_ref, carry_ref, *, tb):
    ne = N_EXPERTS
    gsz = ne // N_EXPERT_GROUPS
    neg = -jnp.inf

    @pl.when(pl.program_id(0) == 0)
    def _():
        carry_ref[...] = jnp.zeros_like(carry_ref)

    s = _sigmoid(lg_ref[...])
    sel = s + bias_ref[...]
    gs = []
    for g in range(N_EXPERT_GROUPS):
        blk = sel[g * gsz:(g + 1) * gsz, :]
        m1 = jnp.max(blk, axis=0, keepdims=True)
        n_eq = jnp.sum(jnp.where(blk == m1, 1.0, 0.0), axis=0, keepdims=True)
        m2 = jnp.max(jnp.where(blk < m1, blk, neg), axis=0, keepdims=True)
        gs.append(m1 + jnp.where(n_eq >= 2.0, m1, m2))
    masked = []
    for g in range(N_EXPERT_GROUPS):
        ahead = jnp.zeros_like(gs[g])
        for h in range(N_EXPERT_GROUPS):
            if h == g:
                continue
            beats = ((gs[h] > gs[g]) | (gs[h] == gs[g])) if h < g else (gs[h] > gs[g])
            ahead = ahead + jnp.where(beats, 1.0, 0.0)
        masked.append(jnp.where(ahead < float(TOPK_GROUPS), sel[g * gsz:(g + 1) * gsz, :], neg))
    selm = jnp.concatenate(masked, axis=0)
    eid = lax.broadcasted_iota(I32, (ne, 1), 0)
    ahead = jnp.zeros_like(selm)
    for e in range(ne):
        row = selm[e:e + 1, :]
        beats = (row > selm) | ((row == selm) & (eid > e))
        ahead = ahead + jnp.where(beats, 1.0, 0.0)
    chosen = ahead < float(TOP_K)
    member = jnp.where(chosen, 1.0, 0.0)
    ssel = jnp.where(chosen, s, 0.0)
    wd = ssel / jnp.sum(ssel, axis=0, keepdims=True) * ROUTED_SCALE
    mb = member.astype(BF16)
    ti = lax.broadcasted_iota(I32, (tb, tb), 0)
    tj = lax.broadcasted_iota(I32, (tb, tb), 1)
    tri = jnp.where(ti < tj, 1.0, 0.0).astype(BF16)
    rank = carry_ref[:, 0:1] + _dot(mb, tri)
    carry_ref[...] = carry_ref[...] + jnp.sum(member, axis=1, keepdims=True)
    cnt_ref[...] = carry_ref[...]
    ei = lax.broadcasted_iota(I32, (ne, ne), 0)
    ej = lax.broadcasted_iota(I32, (ne, ne), 1)
    low = jnp.where(ej < ei, 1.0, 0.0).astype(BF16)
    slot = _dot(low, mb)
    eidf = eid.astype(F32)
    for k in range(TOP_K):
        hit = chosen & (slot == float(k))
        e_ref[k:k + 1, :] = jnp.sum(jnp.where(hit, eidf, 0.0), axis=0, keepdims=True).astype(I32)
        r_ref[k:k + 1, :] = jnp.sum(jnp.where(hit, rank, 0.0), axis=0, keepdims=True).astype(I32)
        w_ref[k:k + 1, :] = jnp.sum(jnp.where(hit, wd, 0.0), axis=0, keepdims=True)


def _route(logits_t, router_bias, tm):
    e, t = logits_t.shape
    tb = 512 if t % 512 == 0 else 256
    eidx, rank, wts, counts = pl.pallas_call(
        functools.partial(_route_kernel, tb=tb),
        grid=(t // tb,),
        in_specs=[
            pl.BlockSpec((e, tb), lambda i: (0, i)),
            pl.BlockSpec((e, 1), lambda i: (0, 0)),
        ],
        out_specs=[
            pl.BlockSpec((TOP_K, tb), lambda i: (0, i)),
            pl.BlockSpec((TOP_K, tb), lambda i: (0, i)),
            pl.BlockSpec((TOP_K, tb), lambda i: (0, i)),
            pl.BlockSpec((e, LANES), lambda i: (0, 0)),
        ],
        out_shape=[
            jax.ShapeDtypeStruct((TOP_K, t), I32),
            jax.ShapeDtypeStruct((TOP_K, t), I32),
            jax.ShapeDtypeStruct((TOP_K, t), F32),
            jax.ShapeDtypeStruct((e, LANES), F32),
        ],
        scratch_shapes=[pltpu.VMEM((e, LANES), F32)],
        compiler_params=_cparams(1),
        name="moe_route",
    )(logits_t, router_bias.astype(F32).reshape(e, 1))
    counts = counts[:, 0].astype(I32)
    padded = (counts + tm - 1) // tm * tm
    pad_end = jnp.cumsum(padded)
    pad_start = pad_end - padded
    dest = (pad_start[eidx] + rank).T
    n_blocks = (t * TOP_K + e * (tm - 1)) // tm
    blk_start = jnp.arange(n_blocks, dtype=I32) * tm
    blk_expert = jnp.minimum(jnp.searchsorted(pad_end, blk_start, side='right'), e - 1).astype(I32)
    n_used = (pad_end[-1] // tm).astype(I32).reshape(1)
    return dest, wts.T, blk_expert, n_used, n_blocks


def _dispatch_kernel(dest_ref, x_ref, xs_in_ref, xs_ref, sem, *, tb):
    del xs_in_ref

    def row_copy(r, k):
        d = dest_ref[0, 0, r * TOP_K + k]
        return pltpu.make_async_copy(x_ref.at[pl.ds(r, 1)], xs_ref.at[pl.ds(d, 1)], sem)

    def start(r, carry):
        for k in range(TOP_K):
            row_copy(r, k).start()
        return carry

    def wait(r, carry):
        for k in range(TOP_K):
            row_copy(r, k).wait()
        return carry

    lax.fori_loop(0, tb, start, 0)
    lax.fori_loop(0, tb, wait, 0)


def _dispatch(xp, dest, n_pad):
    t, w = xp.shape
    tb = ROW_TILE
    xs0 = jnp.zeros((n_pad, w), U32)
    return pl.pallas_call(
        functools.partial(_dispatch_kernel, tb=tb),
        grid=(t // tb,),
        in_specs=[
            pl.BlockSpec((1, 1, tb * TOP_K), lambda i: (i, 0, 0), memory_space=pltpu.SMEM),
            pl.BlockSpec((tb, w), lambda i: (i, 0)),
            pl.BlockSpec(memory_space=pl.ANY),
        ],
        out_specs=pl.BlockSpec(memory_space=pl.ANY),
        out_shape=jax.ShapeDtypeStruct((n_pad, w), U32),
        scratch_shapes=[pltpu.SemaphoreType.DMA(())],
        input_output_aliases={2: 0},
        compiler_params=_cparams(1),
        name="moe_dispatch",
    )(dest.reshape(t // tb, 1, tb * TOP_K), xp, xs0)


def _expert_kernel(be_ref, nu_ref, xs_ref, wg_ref, wu_ref, wd_ref, y_ref, wgb_ref, wub_ref, wdb_ref):
    b = pl.program_id(0)
    e = be_ref[b]
    prev = be_ref[jnp.maximum(b - 1, 0)]

    @pl.when((b == 0) | (e != prev))
    def _():
        wgb_ref[...] = wg_ref[0].astype(BF16)
        wub_ref[...] = wu_ref[0].astype(BF16)
        wdb_ref[...] = wd_ref[0].astype(BF16)

    @pl.when(b < nu_ref[0])
    def _():
        lo, hi = _unpack_bf16_pairs(xs_ref[...])
        lo = lo.astype(BF16)
        hi = hi.astype(BF16)
        half = lo.shape[1]
        gate = _dot(lo, wgb_ref[0:half, :]) + _dot(hi, wgb_ref[half:, :])
        up = _dot(lo, wub_ref[0:half, :]) + _dot(hi, wub_ref[half:, :])
        hid = (_silu(gate) * up).astype(BF16)
        y_ref[...] = _pack_bf16_pairs(_dot(hid, wdb_ref[...]))

    @pl.when(b >= nu_ref[0])
    def _():
        y_ref[...] = jnp.zeros_like(y_ref)


def _experts(xs, blk_expert, n_used, w_gate, w_up, w_down):
    n_pad, w = xs.shape
    tm = MOE_TM
    d, hdn = w_gate.shape[1], w_gate.shape[2]
    return pl.pallas_call(
        _expert_kernel,
        grid_spec=pltpu.PrefetchScalarGridSpec(
            num_scalar_prefetch=2,
            grid=(n_pad // tm,),
            in_specs=[
                pl.BlockSpec((tm, w), lambda b, be, nu: (b, 0)),
                pl.BlockSpec((1, d, hdn), lambda b, be, nu: (be[b], 0, 0)),
                pl.BlockSpec((1, d, hdn), lambda b, be, nu: (be[b], 0, 0)),
                pl.BlockSpec((1, hdn, d), lambda b, be, nu: (be[b], 0, 0)),
            ],
            out_specs=pl.BlockSpec((tm, w), lambda b, be, nu: (b, 0)),
            scratch_shapes=[pltpu.VMEM((d, hdn), BF16), pltpu.VMEM((d, hdn), BF16), pltpu.VMEM((hdn, d), BF16)],
        ),
        out_shape=jax.ShapeDtypeStruct((n_pad, w), U32),
        compiler_params=_cparams(1),
        name="moe_experts",
    )(blk_expert, n_used, xs, w_gate, w_up, w_down)


def _combine_kernel(dest_ref, xp_ref, ws_ref, x_ref, mod_ref, ln_ref, sg_ref, su_ref, sd_ref, y_hbm,
                    xo_ref, ybuf, sem, *, tb, n_lat):
    row0 = pl.program_id(0) * tb

    def row_copy(r, k):
        d = dest_ref[0, 0, r * TOP_K + k]
        return pltpu.make_async_copy(y_hbm.at[pl.ds(d, 1)], ybuf.at[pl.ds(k * tb + r, 1)], sem)

    def start(r, carry):
        for k in range(TOP_K):
            row_copy(r, k).start()
        return carry

    def wait(r, carry):
        for k in range(TOP_K):
            row_copy(r, k).wait()
        return carry

    lax.fori_loop(0, tb, start, 0)
    lo, hi = _unpack_bf16_pairs(xp_ref[...])
    lo = lo.astype(BF16)
    hi = hi.astype(BF16)
    half = lo.shape[1]
    gate = _dot(lo, sg_ref[0:half, :]) + _dot(hi, sg_ref[half:, :])
    up = _dot(lo, su_ref[0:half, :]) + _dot(hi, su_ref[half:, :])
    f = _dot((_silu(gate) * up).astype(BF16), sd_ref[...])
    lax.fori_loop(0, tb, wait, 0)
    acc_lo = jnp.zeros((tb, half), F32)
    acc_hi = jnp.zeros((tb, half), F32)
    for k in range(TOP_K):
        ylo, yhi = _unpack_bf16_pairs(ybuf[k * tb:(k + 1) * tb, :])
        wk = ws_ref[:, k:k + 1]
        acc_lo = acc_lo + wk * ylo
        acc_hi = acc_hi + wk * yhi
    f = f + jnp.concatenate([acc_lo, acc_hi], axis=1)
    g2 = _select_rows(mod_ref, 5, row0, tb, n_lat)
    xo_ref[...] = _layer_norm(DEEPNORM_ALPHA * x_ref[...] + g2 * f, ln_ref[0:1, :], ln_ref[1:2, :])


def _combine(dest, xp, wsel, x_all, mod, ln, sg, su, sd, y, n_lat):
    t, w = xp.shape
    d = x_all.shape[1]
    tb = ROW_TILE
    hdn = sg.shape[1]
    return pl.pallas_call(
        functools.partial(_combine_kernel, tb=tb, n_lat=n_lat),
        grid=(t // tb,),
        in_specs=[
            pl.BlockSpec((1, 1, tb * TOP_K), lambda i: (i, 0, 0), memory_space=pltpu.SMEM),
            pl.BlockSpec((tb, w), lambda i: (i, 0)),
            pl.BlockSpec((tb, TOP_K), lambda i: (i, 0)),
            pl.BlockSpec((tb, d), lambda i: (i, 0)),
            _const_spec((2, 6, d)),
            _const_spec((2, d)),
            _const_spec((d, hdn)),
            _const_spec((d, hdn)),
            _const_spec((hdn, d)),
            pl.BlockSpec(memory_space=pl.ANY),
        ],
        out_specs=pl.BlockSpec((tb, d), lambda i: (i, 0)),
        out_shape=jax.ShapeDtypeStruct((t, d), F32),
        scratch_shapes=[pltpu.VMEM((TOP_K * tb, w), U32), pltpu.SemaphoreType.DMA(())],
        compiler_params=_cparams(1),
        name="moe_combine",
    )(dest.reshape(t // tb, 1, tb * TOP_K), xp, wsel, x_all, mod, ln, sg, su, sd, y)


def _moe(xp, logits, x_res, mod, ln, router_bias, w_gate, w_up, w_down, sg, su, sd, n_lat):
    dest, wsel, blk_expert, n_used, n_blocks = _route(logits, router_bias, MOE_TM)
    xs = _dispatch(xp, dest, n_blocks * MOE_TM)
    y = _experts(xs, blk_expert, n_used, w_gate, w_up, w_down)
    return _combine(dest, xp, wsel, x_res, mod, ln, sg.astype(BF16), su.astype(BF16), sd.astype(BF16), y, n_lat)


def _rot_lanes(xs, cos, sin):
    q = HEAD_DIM // 4
    lane = lax.broadcasted_iota(I32, xs.shape, 1)
    partner = jnp.where((lane % (2 * q)) < q, pltpu.roll(xs, HEAD_DIM - q, 1), pltpu.roll(xs, q, 1))
    return xs * cos + partner * sin


def _projq_kernel(x_ref, mod_ref, w_ref, cos_ref, sin_ref, o_ref, xm_ref, *, tm, n_lat):
    i = pl.program_id(0)
    j = pl.program_id(1)

    @pl.when(j == 0)
    def _():
        xm_ref[...] = _modulate_rows(x_ref[...], mod_ref, 0, 1, i * tm, n_lat).astype(BF16)

    acc = _dot(xm_ref[...], w_ref[...])
    n_heads = acc.shape[1] // HEAD_DIM

    @pl.when(j == 0)
    def _():
        for c in range(n_heads):
            xs = acc[:, c * HEAD_DIM:(c + 1) * HEAD_DIM]
            xs = xs * lax.rsqrt(jnp.mean(xs * xs, axis=-1, keepdims=True) + LN_EPS)
            o_ref[:, c * HEAD_DIM:(c + 1) * HEAD_DIM] = _rot_lanes(xs, cos_ref[0], sin_ref[0]).astype(BF16)

    @pl.when(j == 1)
    def _():
        for c in range(n_heads):
            xs = acc[:, c * HEAD_DIM:(c + 1) * HEAD_DIM]
            o_ref[:, c * HEAD_DIM:(c + 1) * HEAD_DIM] = _rot_lanes(xs, cos_ref[1], sin_ref[1]).astype(BF16)


def _projq(x_all, mod, w_bf, cos2, sin2, n_lat):
    d = x_all.shape[1]
    n = w_bf.shape[1]
    tm = 512 if n_lat % 512 == 0 else 256
    tn = n // 2
    return pl.pallas_call(
        functools.partial(_projq_kernel, tm=tm, n_lat=n_lat),
        grid=(n_lat // tm, 2),
        in_specs=[
            pl.BlockSpec((tm, d), lambda i, j: (i, 0)),
            pl.BlockSpec((2, 6, d), lambda i, j: (0, 0, 0)),
            pl.BlockSpec((d, tn), lambda i, j: (0, j)),
            pl.BlockSpec((2, tm, HEAD_DIM), lambda i, j: (0, i, 0)),
            pl.BlockSpec((2, tm, HEAD_DIM), lambda i, j: (0, i, 0)),
        ],
        out_specs=pl.BlockSpec((tm, tn), lambda i, j: (i, j)),
        out_shape=jax.ShapeDtypeStruct((n_lat, n), BF16),
        scratch_shapes=[pltpu.VMEM((tm, d), BF16)],
        compiler_params=_cparams(2),
        name="proj1_q",
    )(x_all, mod, w_bf, cos2, sin2)


def _projkv_kernel(x_ref, mod_ref, wv_ref, wkt_ref, cos_ref, sin_ref, v_ref, kt_ref, *, tm, n_lat):
    i = pl.program_id(0)
    xm = _modulate_rows(x_ref[...], mod_ref, 0, 1, i * tm, n_lat).astype(BF16)
    v_ref[...] = _dot(xm, wv_ref[...]).astype(BF16)
    kt = _dot_nt(wkt_ref[...], xm)
    q = HEAD_DIM // 4
    n_gk = GQA_KV_HEADS * HEAD_DIM
    for c in range(kt.shape[0] // HEAD_DIM):
        xs = kt[c * HEAD_DIM:(c + 1) * HEAD_DIM, :]
        t = 0 if c * HEAD_DIM < n_gk else 1
        if t == 0:
            xs = xs * lax.rsqrt(jnp.mean(xs * xs, axis=0, keepdims=True) + LN_EPS)
        partner = jnp.concatenate([xs[q:2 * q], xs[0:q], xs[3 * q:4 * q], xs[2 * q:3 * q]], axis=0)
        kt_ref[c * HEAD_DIM:(c + 1) * HEAD_DIM, :] = (xs * cos_ref[t] + partner * sin_ref[t]).astype(BF16)


def _projkv(x_all, mod, wv_bf, wkt_bf, cos_t, sin_t, n_lat):
    m, d = x_all.shape
    nv = wv_bf.shape[1]
    nk = wkt_bf.shape[0]
    tm = 640 if m % 640 == 0 else 256
    return pl.pallas_call(
        functools.partial(_projkv_kernel, tm=tm, n_lat=n_lat),
        grid=(m // tm,),
        in_specs=[
            pl.BlockSpec((tm, d), lambda i: (i, 0)),
            _const_spec((2, 6, d)),
            _const_spec((d, nv)),
            _const_spec((nk, d)),
            pl.BlockSpec((2, HEAD_DIM, tm), lambda i: (0, 0, i)),
            pl.BlockSpec((2, HEAD_DIM, tm), lambda i: (0, 0, i)),
        ],
        out_specs=[
            pl.BlockSpec((tm, nv), lambda i: (i, 0)),
            pl.BlockSpec((nk, tm), lambda i: (0, i)),
        ],
        out_shape=[jax.ShapeDtypeStruct((m, nv), BF16), jax.ShapeDtypeStruct((nk, m), BF16)],
        compiler_params=_cparams(1),
        name="proj1_kv",
    )(x_all, mod, wv_bf, wkt_bf, cos_t, sin_t)


def _flash(qs, kt_ref, v_ref, scratch, tk, stack):
    (qs_ref, s_refs, pm_refs, p_refs, al_refs, m_ref, l_ref, acc_ref) = scratch
    rows = qs_ref.shape[0]
    lk = v_ref.shape[0]
    n = lk // tk
    rs = rows // stack
    part = rows // len(qs)
    for a, q_part in enumerate(qs):
        qs_ref[a * part:(a + 1) * part, :] = q_part
    m_ref[...] = jnp.full(m_ref.shape, -jnp.inf, F32)
    l_ref[...] = jnp.zeros(l_ref.shape, F32)
    acc_ref[...] = jnp.zeros(acc_ref.shape, F32)

    n_grp = tk // LANES
    dv = acc_ref.shape[1]
    rb = FLASH_BLOCK if rs % FLASH_BLOCK == 0 else rs
    n_rb = rows // rb

    def score_rows(j, r, s_ref, pm_ref):
        off = pl.multiple_of(j * tk, tk)
        ka = 0 if stack == 1 else pl.multiple_of((r // rs) * HEAD_DIM, HEAD_DIM)
        s = _dot(qs_ref[pl.ds(r, rb), :], kt_ref[pl.ds(ka, HEAD_DIM), pl.ds(off, tk)])
        s_ref[pl.ds(r, rb), :] = s
        pm = s[:, 0:LANES]
        for c in range(1, n_grp):
            pm = jnp.maximum(pm, s[:, c * LANES:(c + 1) * LANES])
        pm_ref[pl.ds(r, rb), :] = pm

    def softmax_rows(r, s_ref, pm_ref, p_ref, al_ref):
        blk = pl.ds(r, rb)
        m_prev = m_ref[blk, :]
        m_next = jnp.maximum(m_prev, jnp.max(pm_ref[blk, :], axis=1, keepdims=True))
        alpha = jnp.exp2(m_prev - m_next)
        al_ref[blk, :] = alpha
        m_ref[blk, :] = m_next
        for q in range(rb // FLASH_ROWS):
            sub = pl.ds(pl.multiple_of(r + q * FLASH_ROWS, FLASH_ROWS), FLASH_ROWS)
            mb = m_next[q * FLASH_ROWS:(q + 1) * FLASH_ROWS]
            lsum = alpha[q * FLASH_ROWS:(q + 1) * FLASH_ROWS] * l_ref[sub, :]
            for c in range(n_grp):
                cols = slice(c * LANES, (c + 1) * LANES)
                pc = jnp.exp2(s_ref[sub, cols] - mb)
                lsum = lsum + pc
                p_ref[sub, cols] = pc.astype(BF16)
            l_ref[sub, :] = lsum

    def value_rows(j, r, p_ref, al_ref):
        off = pl.multiple_of(j * tk, tk)
        blk = pl.ds(r, rb)
        pv = _dot(p_ref[blk, :], v_ref[pl.ds(off, tk), :])
        al = al_ref[blk, :]
        for c in range(dv // LANES):
            cols = slice(c * LANES, (c + 1) * LANES)
            acc_ref[blk, cols] = al * acc_ref[blk, cols] + pv[:, cols]

    def step_parity(j, cur, with_scores, with_values):
        nxt = 1 - cur

        def body(i, carry):
            r = pl.multiple_of(i * rb, rb)
            if with_scores:
                score_rows(j + 1, r, s_refs[nxt], pm_refs[nxt])
            softmax_rows(r, s_refs[cur], pm_refs[cur], p_refs[cur], al_refs[cur])
            if with_values:
                value_rows(j - 1, r, p_refs[nxt], al_refs[nxt])
            return carry

        lax.fori_loop(0, n_rb, body, 0)

    def first_scores(i, carry):
        score_rows(0, pl.multiple_of(i * rb, rb), s_refs[0], pm_refs[0])
        return carry

    lax.fori_loop(0, n_rb, first_scores, 0)
    if n == 1:
        step_parity(0, 0, False, False)
    else:
        step_parity(0, 0, True, False)
        mid = n - 2

        def pair(i, carry):
            j = 1 + 2 * i
            step_parity(j, 1, True, True)
            step_parity(j + 1, 0, True, True)
            return carry

        lax.fori_loop(0, mid // 2, pair, 0)
        if mid % 2:
            step_parity(n - 2, (n - 2) % 2, True, True)
        step_parity(n - 1, (n - 1) % 2, False, True)

    def last_values(i, carry):
        value_rows(n - 1, pl.multiple_of(i * rb, rb), p_refs[(n - 1) % 2], al_refs[(n - 1) % 2])
        return carry

    lax.fori_loop(0, n_rb, last_values, 0)
    return acc_ref[...] / jnp.sum(l_ref[...], axis=1, keepdims=True)


def _flash_scratch(rows, tk, dv):
    stat = pltpu.VMEM((rows, LANES), F32)
    return [pltpu.VMEM((rows, HEAD_DIM), BF16),
            pltpu.VMEM((rows, tk), F32), pltpu.VMEM((rows, tk), F32), stat, stat,
            pltpu.VMEM((rows, tk), BF16), pltpu.VMEM((rows, tk), BF16), stat, stat,
            stat, stat, pltpu.VMEM((rows, dv), F32)]


def _unpack_flash_scratch(refs):
    qs_ref, s0, s1, pm0, pm1, p0, p1, a0, a1, m_ref, l_ref, acc_ref = refs
    return (qs_ref, (s0, s1), (pm0, pm1), (p0, p1), (a0, a1), m_ref, l_ref, acc_ref)


def _gqa_kernel(q_ref, kt_ref, v_ref, o_ref, *scratch, tk):
    tq = q_ref.shape[0]
    qs = [q_ref[:, a * HEAD_DIM:(a + 1) * HEAD_DIM] for a in range(GQA_GROUP)]
    o = _flash(qs, kt_ref, v_ref, _unpack_flash_scratch(scratch), tk, 1)
    for a in range(GQA_GROUP):
        o_ref[:, a * HEAD_DIM:(a + 1) * HEAD_DIM] = o[a * tq:(a + 1) * tq].astype(BF16)


def _diff_kernel(lam_ref, q_ref, kt_ref, v_ref, g_ref, o_ref, *scratch, tk, out_scale):
    tq = q_ref.shape[0]
    qs = [q_ref[:, 0:HEAD_DIM], q_ref[:, HEAD_DIM:2 * HEAD_DIM]]
    o = _flash(qs, kt_ref, v_ref, _unpack_flash_scratch(scratch), tk, 2)
    od = o[0:tq] - lam_ref[0] * o[tq:2 * tq]
    od = od * lax.rsqrt(jnp.mean(od * od, axis=-1, keepdims=True) + LN_EPS)
    o_ref[...] = (od * g_ref[...] * out_scale).astype(BF16)


def _key_chunk(lk):
    for tk in (1280, 1024, 640, 512, 256, 128):
        if lk % tk == 0 and lk // tk >= 2:
            return tk
    return lk


def _attention(q1, kt, v1, lam, subln_g, lambda_init, n_lat):
    lk = v1.shape[0]
    tk = _key_chunk(lk)
    tq = 256
    gw = GQA_GROUP * HEAD_DIM
    n_gv = GQA_KV_HEADS * HEAD_DIM
    once = pl.Buffered(1)
    og = pl.pallas_call(
        functools.partial(_gqa_kernel, tk=tk),
        grid=(GQA_KV_HEADS, n_lat // tq),
        in_specs=[
            pl.BlockSpec((tq, gw), lambda g, i: (i, g)),
            pl.BlockSpec((HEAD_DIM, lk), lambda g, i: (g, 0), pipeline_mode=once),
            pl.BlockSpec((lk, HEAD_DIM), lambda g, i: (0, g), pipeline_mode=once),
        ],
        out_specs=pl.BlockSpec((tq, gw), lambda g, i: (i, g)),
        out_shape=jax.ShapeDtypeStruct((n_lat, GQA_HEADS * HEAD_DIM), BF16),
        scratch_shapes=_flash_scratch(GQA_GROUP * tq, tk, HEAD_DIM),
        compiler_params=_cparams(2),
        name="gqa_attention",
    )(q1, kt, v1)
    q_off = GQA_HEADS * HEAD_DIM // DIFF_V_DIM
    kv_off = n_gv // DIFF_V_DIM
    tq = 512 if n_lat % 512 == 0 else 256
    od = pl.pallas_call(
        functools.partial(_diff_kernel, tk=tk, out_scale=1.0 - lambda_init),
        grid=(DIFF_HEADS, n_lat // tq),
        in_specs=[
            pl.BlockSpec(memory_space=pltpu.SMEM),
            pl.BlockSpec((tq, DIFF_V_DIM), lambda h, i: (i, q_off + h)),
            pl.BlockSpec((DIFF_V_DIM, lk), lambda h, i: (kv_off + h, 0), pipeline_mode=once),
            pl.BlockSpec((lk, DIFF_V_DIM), lambda h, i: (0, kv_off + h), pipeline_mode=once),
            pl.BlockSpec((1, DIFF_V_DIM), lambda h, i: (0, 0)),
        ],
        out_specs=pl.BlockSpec((tq, DIFF_V_DIM), lambda h, i: (i, h)),
        out_shape=jax.ShapeDtypeStruct((n_lat, DIFF_HEADS * DIFF_V_DIM), BF16),
        scratch_shapes=_flash_scratch(2 * tq, tk, DIFF_V_DIM),
        compiler_params=_cparams(2),
        name="diff_attention",
    )(lam, q1, kt, v1, subln_g.reshape(1, DIFF_V_DIM).astype(F32))
    return og, od


def _router_operand(router_w):
    return router_w.astype(F32).T


def kernel(x, c, ctx, c_ctx, mod_w, mod_b, ln_g, ln_b, ev_w_in, ev_w_out, ret_log_decay, s5_a_re, s5_a_im, s5_log_dt, s5_b_re, s5_b_im, s5_c_re, s5_c_im, s5_d, s5_w_glu, s5_b_glu, od_w_in, od_w_out, qk_norm_g, diff_lambda, diff_subln_g, router_w, router_bias, exp_w_gate, exp_w_up, exp_w_down, sh_w_gate, sh_w_up, sh_w_down):
    b_, n_lat, d = x.shape
    n_ctx = ctx.shape[1]
    assert b_ == 1 and d == D_MODEL and n_lat % RET_CHUNK == 0 and n_ctx % RET_CHUNK == 0
    m_all = n_lat + n_ctx
    x_all = jnp.concatenate([x[0], ctx[0]], axis=0)
    mods = _modulation(c, c_ctx, mod_w, mod_b)

    cos0, sin0 = _rope_tables(n_lat, n_ctx, RET_DIM)
    p0 = _proj0(x_all, mods[0], ev_w_in[0].astype(BF16), cos0, sin0, n_lat)
    rn = _retention(p0, ret_log_decay[0], n_lat, n_ctx)
    s5_tabs = functools.partial(_s5_tables, s5_a_re[0], s5_a_im[0], s5_log_dt[0], s5_b_re[0], s5_b_im[0],
                                s5_c_re[0], s5_c_im[0], s5_d[0])
    s5y = _s5(p0[:, 4 * RET_HEADS * RET_DIM:], s5_tabs, n_lat, n_ctx)
    ln0 = jnp.stack([ln_g[0], ln_b[0]], axis=1)
    x_all, xp, logits = _finish0(rn, p0, s5y, x_all, mods[0], ln0[0], s5_w_glu[0].astype(BF16),
                                 s5_b_glu[0].reshape(1, -1).astype(F32), ev_w_out[0].astype(BF16),
                                 _router_operand(router_w[0]), n_lat)
    x_all = _moe(xp, logits, x_all, mods[0], ln0[1], router_bias[0], exp_w_gate[0], exp_w_up[0], exp_w_down[0],
                 sh_w_gate[0], sh_w_up[0], sh_w_down[0], n_lat)

    i = 1
    lambda_init = 0.8 - 0.6 * math.exp(-0.3 * i)
    gq_w = GQA_HEADS * HEAD_DIM
    gk_w = GQA_KV_HEADS * HEAD_DIM
    dq_w = DIFF_HEADS * 2 * HEAD_DIM
    cuts = [gq_w, gq_w + gk_w, gq_w + 2 * gk_w, gq_w + 2 * gk_w + dq_w, gq_w + 2 * gk_w + 2 * dq_w]
    w_in = od_w_in[0]
    w_gq, w_gk, w_gv, w_dq, w_dk, w_dv = (w_in[:, a:b] for a, b in zip([0] + cuts, cuts + [w_in.shape[1]]))
    cos1, sin1 = _rope_tables(n_lat, n_ctx, HEAD_DIM)
    qscale = HEAD_DIM ** -0.5 * LOG2E
    gq_gain = qk_norm_g[0, 0].astype(F32)
    gk_gain = qk_norm_g[0, 1].astype(F32)
    quarter = HEAD_DIM // 4

    def partner_gain(g):
        return jnp.concatenate([g[quarter:2 * quarter], g[0:quarter], g[3 * quarter:], g[2 * quarter:3 * quarter]])

    cos_q = jnp.stack([cos1[:n_lat] * gq_gain[None, :], cos1[:n_lat]]) * qscale
    sin_q = jnp.stack([sin1[:n_lat] * partner_gain(gq_gain)[None, :], sin1[:n_lat]]) * qscale
    cos_k = jnp.stack([cos1.T * gk_gain[:, None], cos1.T])
    sin_k = jnp.stack([sin1.T * partner_gain(gk_gain)[:, None], sin1.T])
    q1 = _projq(x_all, mods[1], jnp.concatenate([w_gq, w_dq], axis=1).astype(BF16), cos_q, sin_q, n_lat)
    v1, kt = _projkv(x_all, mods[1], jnp.concatenate([w_gv, w_dv], axis=1).astype(BF16),
                     jnp.concatenate([w_gk, w_dk], axis=1).T.astype(BF16), cos_k, sin_k, n_lat)
    lf = diff_lambda[0].astype(F32)
    lam = (jnp.exp(jnp.sum(lf[0] * lf[1])) - jnp.exp(jnp.sum(lf[2] * lf[3])) + lambda_init).reshape(1)
    og, od = _attention(q1, kt, v1, lam, diff_subln_g[0], lambda_init, n_lat)
    ln1 = jnp.stack([ln_g[1], ln_b[1]], axis=1)
    x_lat, xp, logits = _finish1(og, od, x_all, mods[1], ln1[0], od_w_out[0].astype(BF16),
                                 _router_operand(router_w[1]), n_lat)
    x_lat = _moe(xp, logits, x_lat, mods[1], ln1[1], router_bias[1], exp_w_gate[1], exp_w_up[1], exp_w_down[1],
                 sh_w_gate[1], sh_w_up[1], sh_w_down[1], n_lat)
    return x_lat[None]
```

```python
import functools
import math

import jax
import jax.numpy as jnp
from jax import lax
from jax.experimental import pallas as pl
from jax.experimental.pallas import tpu as pltpu

F32 = jnp.float32
BF16 = jnp.bfloat16
U32 = jnp.uint32
I32 = jnp.int32

D_MODEL = 2048
DEPTH = 2
GRID_W = 64
ROPE_BASE = 10000.0
LN_EPS = 1e-6
DEEPNORM_ALPHA = (2 * DEPTH) ** 0.25
RET_HEADS = 4
RET_DIM = 256
S5_CHANNELS = D_MODEL // 2
S5_GROUP = 16
S5_GROUPS = S5_CHANNELS // S5_GROUP
S5_STATE = 64
HEAD_DIM = 128
GQA_HEADS = 8
GQA_KV_HEADS = 2
GQA_GROUP = GQA_HEADS // GQA_KV_HEADS
DIFF_HEADS = 4
DIFF_V_DIM = 2 * HEAD_DIM
N_EXPERTS = 64
TOP_K = 8
N_EXPERT_GROUPS = 8
TOPK_GROUPS = 4
EXPERT_HIDDEN = 512
ROUTED_SCALE = 2.5
LOG2E = 1.4426950408889634

LANES = 128
VMEM_LIMIT = 56 * 1024 * 1024

RET_CHUNK = 256
S5_STEP = 16
MOE_TM = 256
ROW_TILE = 256
FLASH_ROWS = 64
FLASH_BLOCK = 512
FLASH_SUB = 256


def _cparams(n_axes):
    return pltpu.CompilerParams(dimension_semantics=("arbitrary",) * n_axes, vmem_limit_bytes=VMEM_LIMIT)


def _dot(a, b):
    return jnp.dot(a, b, preferred_element_type=F32)


def _dot_nt(a, b):
    return lax.dot_general(a, b, (((1,), (1,)), ((), ())), preferred_element_type=F32)


def _dot_tn(a, b):
    return lax.dot_general(a, b, (((0,), (0,)), ((), ())), preferred_element_type=F32)


def _split(a):
    hi = a.astype(BF16)
    lo = (a - hi.astype(F32)).astype(BF16)
    return hi, lo


def _dot3(a, b):
    ah, al = _split(a)
    bh, bl = _split(b)
    return _dot(ah, bh) + _dot(al, bh) + _dot(ah, bl)


def _sigmoid(x):
    return 1.0 / (1.0 + jnp.exp(-x))


def _silu(x):
    return x * _sigmoid(x)


def _gelu_tanh(x):
    return 0.5 * x * (1.0 + jnp.tanh(math.sqrt(2.0 / math.pi) * (x + 0.044715 * (x * x * x))))


def _layer_norm(h, g, b):
    mu = jnp.mean(h, axis=-1, keepdims=True)
    d = h - mu
    var = jnp.mean(d * d, axis=-1, keepdims=True)
    return d * lax.rsqrt(var + LN_EPS) * g + b


def _pack_bf16_pairs(x):
    n = x.shape[1] // 2
    lo = lax.bitcast_convert_type(x[:, :n].astype(BF16).astype(F32), U32)
    hi = lax.bitcast_convert_type(x[:, n:].astype(BF16).astype(F32), U32)
    return (lo >> 16) | (hi & jnp.uint32(0xFFFF0000))


def _unpack_bf16_pairs(w):
    lo = lax.bitcast_convert_type(w << 16, F32)
    hi = lax.bitcast_convert_type(w & jnp.uint32(0xFFFF0000), F32)
    return lo, hi


def _modulate_rows(x, mod_ref, shift_row, scale_row, row0, n_lat):
    rows = row0 + lax.broadcasted_iota(I32, (x.shape[0], 1), 0)
    is_ctx = rows >= n_lat
    sc = jnp.where(is_ctx, mod_ref[1, scale_row:scale_row + 1, :], mod_ref[0, scale_row:scale_row + 1, :])
    sh = jnp.where(is_ctx, mod_ref[1, shift_row:shift_row + 1, :], mod_ref[0, shift_row:shift_row + 1, :])
    return x * (1.0 + sc) + sh


def _select_rows(mod_ref, row, row0, n_rows, n_lat):
    rows = row0 + lax.broadcasted_iota(I32, (n_rows, 1), 0)
    return jnp.where(rows >= n_lat, mod_ref[1, row:row + 1, :], mod_ref[0, row:row + 1, :])


def _mod_kernel(v_ref, w_ref, b_ref, o_ref):
    v = v_ref[...]
    o_ref[0] = _dot3(_silu(v), w_ref[0]) + b_ref[0]


def _modulation(c, c_ctx, mod_w, mod_b):
    depth, d, n = mod_w.shape
    v = jnp.concatenate([c[:1], c_ctx[None], jnp.zeros((6, d), F32)], axis=0)
    tn = 512
    out = pl.pallas_call(
        _mod_kernel,
        grid=(depth, n // tn),
        in_specs=[
            pl.BlockSpec((8, d), lambda i, j: (0, 0)),
            pl.BlockSpec((1, d, tn), lambda i, j: (i, 0, j)),
            pl.BlockSpec((1, 1, tn), lambda i, j: (i, 0, j)),
        ],
        out_specs=pl.BlockSpec((1, 8, tn), lambda i, j: (i, 0, j)),
        out_shape=jax.ShapeDtypeStruct((depth, 8, n), F32),
        compiler_params=_cparams(2),
        name="modulation",
    )(v, mod_w, mod_b.reshape(depth, 1, n))
    return out[:, :2].reshape(depth, 2, 6, d)


def _rope_tables(n_lat, n_ctx, head_dim):
    q = head_dim // 4
    t = jnp.arange(n_lat, dtype=I32)
    row = (t // GRID_W).astype(F32)
    col = (t % GRID_W).astype(F32)
    freqs = ROPE_BASE ** (-jnp.arange(q, dtype=F32) / q)
    ar = row[:, None] * freqs[None, :]
    ac = col[:, None] * freqs[None, :]
    cos = jnp.concatenate([jnp.cos(ar), jnp.cos(ar), jnp.cos(ac), jnp.cos(ac)], axis=-1)
    sin = jnp.concatenate([-jnp.sin(ar), jnp.sin(ar), -jnp.sin(ac), jnp.sin(ac)], axis=-1)
    cos = jnp.concatenate([cos, jnp.ones((n_ctx, head_dim), F32)], axis=0)
    sin = jnp.concatenate([sin, jnp.zeros((n_ctx, head_dim), F32)], axis=0)
    return cos, sin


def _proj0_kernel(x_ref, mod_ref, w_ref, cos_ref, sin_ref, o_ref, xm_ref, *, tm, n_lat):
    i = pl.program_id(0)
    j = pl.program_id(1)

    @pl.when(j == 0)
    def _():
        xm_ref[...] = _modulate_rows(x_ref[...], mod_ref, 0, 1, i * tm, n_lat).astype(BF16)

    acc = _dot(xm_ref[...], w_ref[...])

    @pl.when(j >= 2)
    def _():
        o_ref[...] = acc.astype(BF16)

    @pl.when(j < 2)
    def _():
        scale = jnp.where(j == 0, RET_DIM ** -0.5, 1.0).astype(F32)
        for c in range(acc.shape[1] // LANES):
            xs = acc[:, c * LANES:(c + 1) * LANES]
            t0 = (c % 2) * LANES
            rot = xs * cos_ref[:, t0:t0 + LANES] + pltpu.roll(xs, LANES // 2, 1) * sin_ref[:, t0:t0 + LANES]
            o_ref[:, c * LANES:(c + 1) * LANES] = (rot * scale).astype(BF16)


def _proj0(x_all, mod, w_bf, cos, sin, n_lat):
    m, d = x_all.shape
    n = w_bf.shape[1]
    tm = 640 if m % 640 == 0 else 256
    tn = 1024
    return pl.pallas_call(
        functools.partial(_proj0_kernel, tm=tm, n_lat=n_lat),
        grid=(m // tm, n // tn),
        in_specs=[
            pl.BlockSpec((tm, d), lambda i, j: (i, 0)),
            pl.BlockSpec((2, 6, d), lambda i, j: (0, 0, 0)),
            pl.BlockSpec((d, tn), lambda i, j: (0, j)),
            pl.BlockSpec((tm, RET_DIM), lambda i, j: (i, 0)),
            pl.BlockSpec((tm, RET_DIM), lambda i, j: (i, 0)),
        ],
        out_specs=pl.BlockSpec((tm, tn), lambda i, j: (i, j)),
        out_shape=jax.ShapeDtypeStruct((m, n), BF16),
        scratch_shapes=[pltpu.VMEM((tm, d), BF16)],
        compiler_params=_cparams(2),
        name="proj0",
    )(x_all, mod, w_bf, cos, sin)


def _ret_tables(log_decay):
    c = RET_CHUNK
    lg = log_decay.astype(F32)
    idx = jnp.arange(c, dtype=F32)
    diff = idx[:, None] - idx[None, :]
    lower = jnp.where(diff >= 0, jnp.exp(jnp.maximum(diff, 0.0)[None] * lg[0][:, None, None]), 0.0)
    upper = jnp.where(diff <= 0, jnp.exp(jnp.maximum(-diff, 0.0)[None] * lg[1][:, None, None]), 0.0)
    mask = lower + upper
    ones = jnp.ones((1, 1, RET_DIM), F32)
    kdec_f = jnp.exp((c - 1 - idx)[None, :] * lg[0][:, None])[:, :, None] * ones
    qdec_f = jnp.exp((idx + 1)[None, :] * lg[0][:, None])[:, :, None] * ones
    kdec_b = jnp.exp(idx[None, :] * lg[1][:, None])[:, :, None] * ones
    qdec_b = jnp.exp((c - idx)[None, :] * lg[1][:, None])[:, :, None] * ones
    g_chunk = jnp.exp(c * lg)
    return mask, kdec_f, qdec_f, kdec_b, qdec_b, g_chunk


def _ret_bwd_state_kernel(gc_ref, k_ref, v_ref, kdec_ref, sb_ref, s_ref):
    h = pl.program_id(0)
    s = pl.program_id(1)

    @pl.when(s == 0)
    def _():
        s_ref[...] = jnp.zeros_like(s_ref)

    sb_ref[0, 0] = s_ref[...].astype(BF16)
    kd = (k_ref[...].astype(F32) * kdec_ref[0]).astype(BF16)
    s_ref[...] = gc_ref[1, h] * s_ref[...] + _dot_tn(kd, v_ref[...])


def _ret_out_kernel(gc_ref, q_ref, k_ref, v_ref, sb_ref, mask_ref, qdf_ref, qdb_ref, kdf_ref, o_ref, s_ref):
    h = pl.program_id(0)
    s = pl.program_id(1)

    @pl.when(s == 0)
    def _():
        s_ref[...] = jnp.zeros_like(s_ref)

    q = q_ref[...]
    k = k_ref[...]
    v = v_ref[...]
    qf = q.astype(F32)
    w = (_dot_nt(q, k) * mask_ref[0]).astype(BF16)
    o = _dot(w, v)
    o = o + _dot((qf * qdf_ref[0]).astype(BF16), s_ref[...].astype(BF16))
    o = o + _dot((qf * qdb_ref[0]).astype(BF16), sb_ref[0, 0])
    mu = jnp.mean(o, axis=-1, keepdims=True)
    d = o - mu
    var = jnp.mean(d * d, axis=-1, keepdims=True)
    o_ref[...] = (d * lax.rsqrt(var + LN_EPS)).astype(BF16)
    kd = (k.astype(F32) * kdf_ref[0]).astype(BF16)
    s_ref[...] = gc_ref[0, h] * s_ref[...] + _dot_tn(kd, v)


def _retention(p0, log_decay, n_lat, n_ctx):
    m = p0.shape[0]
    c = RET_CHUNK
    nc = m // c
    nlc = n_lat // c
    hh = RET_HEADS
    mask, kdec_f, qdec_f, kdec_b, qdec_b, g_chunk = _ret_tables(log_decay)
    smem = pl.BlockSpec(memory_space=pltpu.SMEM)

    def bchunk(s):
        return nc - 1 - s

    sb = pl.pallas_call(
        _ret_bwd_state_kernel,
        grid=(hh, nc),
        in_specs=[
            smem,
            pl.BlockSpec((c, RET_DIM), lambda h, s: (bchunk(s), hh + h)),
            pl.BlockSpec((c, RET_DIM), lambda h, s: (bchunk(s), 2 * hh + h)),
            pl.BlockSpec((1, c, RET_DIM), lambda h, s: (h, 0, 0)),
        ],
        out_specs=pl.BlockSpec((1, 1, RET_DIM, RET_DIM), lambda h, s: (h, bchunk(s), 0, 0)),
        out_shape=jax.ShapeDtypeStruct((hh, nc, RET_DIM, RET_DIM), BF16),
        scratch_shapes=[pltpu.VMEM((RET_DIM, RET_DIM), F32)],
        compiler_params=_cparams(2),
        name="ret_bwd_state",
    )(g_chunk, p0, p0, kdec_b)

    def fchunk(s):
        return (s + nlc) % nc

    return pl.pallas_call(
        _ret_out_kernel,
        grid=(hh, nc),
        in_specs=[
            smem,
            pl.BlockSpec((c, RET_DIM), lambda h, s: (fchunk(s), h)),
            pl.BlockSpec((c, RET_DIM), lambda h, s: (fchunk(s), hh + h)),
            pl.BlockSpec((c, RET_DIM), lambda h, s: (fchunk(s), 2 * hh + h)),
            pl.BlockSpec((1, 1, RET_DIM, RET_DIM), lambda h, s: (h, fchunk(s), 0, 0)),
            pl.BlockSpec((1, c, c), lambda h, s: (h, 0, 0)),
            pl.BlockSpec((1, c, RET_DIM), lambda h, s: (h, 0, 0)),
            pl.BlockSpec((1, c, RET_DIM), lambda h, s: (h, 0, 0)),
            pl.BlockSpec((1, c, RET_DIM), lambda h, s: (h, 0, 0)),
        ],
        out_specs=pl.BlockSpec((c, RET_DIM), lambda h, s: (fchunk(s), h)),
        out_shape=jax.ShapeDtypeStruct((m, hh * RET_DIM), BF16),
        scratch_shapes=[pltpu.VMEM((RET_DIM, RET_DIM), F32)],
        compiler_params=_cparams(2),
        name="ret_out",
    )(g_chunk, p0, p0, p0, sb, mask, qdec_f, qdec_b, kdec_f)


def _s5_tables(a_re, a_im, log_dt, b_re, b_im, c_re, c_im, d_skip, n_scan):
    f32 = F32
    st = S5_STEP
    g_, p_, s_ = S5_GROUPS, S5_STATE, S5_GROUP
    b_re, b_im, c_re, c_im = (t.astype(f32) for t in (b_re, b_im, c_re, c_im))
    ws, vs, aps, ks = [], [], [], []
    for direction in range(2):
        are, aim = a_re[direction].astype(f32), a_im[direction].astype(f32)
        dt = jnp.exp(log_dt[direction].astype(f32))[:, None]
        zr, zi = are * dt, aim * dt
        mag = jnp.exp(zr)
        ab_re, ab_im = mag * jnp.cos(zi), mag * jnp.sin(zi)
        den = jnp.square(are) + jnp.square(aim)
        nr, ni = ab_re - 1.0, ab_im
        f_re = (nr * are + ni * aim) / den
        f_im = (ni * are - nr * aim) / den
        bb_re = f_re[..., None] * b_re - f_im[..., None] * b_im
        bb_im = f_re[..., None] * b_im + f_im[..., None] * b_re
        pr, pi = [jnp.ones_like(ab_re)], [jnp.zeros_like(ab_im)]
        for _ in range(st):
            pr.append(pr[-1] * ab_re - pi[-1] * ab_im)
            pi.append(pr[-2] * ab_im + pi[-1] * ab_re)
        pw_re, pw_im = jnp.stack(pr), jnp.stack(pi)
        ca_re = c_re[None] * pw_re[:, :, None, :] - c_im[None] * pw_im[:, :, None, :]
        ca_im = c_re[None] * pw_im[:, :, None, :] + c_im[None] * pw_re[:, :, None, :]
        kk = (jnp.einsum('tgip,gpj->tgij', ca_re[:st], bb_re, precision='highest')
              - jnp.einsum('tgip,gpj->tgij', ca_im[:st], bb_im, precision='highest'))
        ks.append(kk)
        e = (st - 1 - jnp.arange(st)) if direction == 0 else jnp.arange(st)
        w_re = pw_re[e][:, :, :, None] * bb_re[None] - pw_im[e][:, :, :, None] * bb_im[None]
        w_im = pw_re[e][:, :, :, None] * bb_im[None] + pw_im[e][:, :, :, None] * bb_re[None]
        w = jnp.concatenate([w_re, w_im], axis=2)
        ws.append(jnp.transpose(w, (1, 0, 3, 2)).reshape(g_, st * s_, 2 * p_))
        e2 = (jnp.arange(st) + 1) if direction == 0 else (st - jnp.arange(st))
        v = jnp.concatenate([ca_re[e2], -ca_im[e2]], axis=3)
        vs.append(jnp.transpose(v, (1, 3, 0, 2)).reshape(g_, 2 * p_, st * s_))
        qr, qi = pw_re[st], pw_im[st]
        rows = []
        for _ in range(n_scan):
            rows.append(jnp.stack([jnp.concatenate([qr, qr], -1), jnp.concatenate([-qi, qi], -1)], axis=1))
            qr, qi = qr * qr - qi * qi, 2.0 * qr * qi
        aps.append(jnp.stack(rows, axis=1))
    tt = jnp.arange(st)
    lag = tt[None, :] - tt[:, None]
    kf = ks[0][jnp.clip(lag, 0, st - 1)]
    kb = ks[1][jnp.clip(-lag, 0, st - 1)]
    tm = jnp.where((lag >= 0)[:, :, None, None, None], kf, 0.0) + jnp.where((lag <= 0)[:, :, None, None, None], kb, 0.0)
    tm = jnp.transpose(tm, (2, 0, 4, 1, 3)).reshape(g_, st * s_, st * s_)
    dsk = d_skip.astype(f32).reshape(g_, s_)
    tm = tm + jnp.eye(st * s_, dtype=f32)[None] * jnp.tile(dsk, (1, st))[:, None, :]
    return tm, jnp.stack(ws), jnp.stack(vs), jnp.stack(aps)


def _s5_kernel(u_ref, t_ref, w_ref, v_ref, a_ref, y_ref, uf_ref, ut_ref, yg_ref, ha_ref, hb_ref,
               *, ncs, nls, n_scan):
    st = S5_STEP
    nt = nls + ncs
    groups = LANES // S5_GROUP
    tw = st * S5_GROUP
    shift = S5_GROUP.bit_length() - 1
    lane_shift = LANES.bit_length() - 1
    uf_ref[...] = u_ref[...].astype(F32)
    for t in range(st):
        ut_ref[t // 2, :, (t % 2) * LANES:(t % 2 + 1) * LANES] = uf_ref[pl.ds(t, nt, stride=st), :].astype(BF16)
    src = lax.broadcasted_iota(I32, (2 * LANES, tw), 0)
    pos = lax.broadcasted_iota(I32, (2 * LANES, tw), 1)
    for g in range(groups):
        base = ((src & (LANES - 1)) == g * S5_GROUP + (pos & (S5_GROUP - 1)))
        ug = jnp.zeros((nt, tw), F32)
        for p in range(st // 2):
            sel = jnp.where(base & ((pos >> shift) == 2 * p + (src >> lane_shift)), 1.0, 0.0).astype(BF16)
            ug = ug + _dot(ut_ref[p], sel)
        ug = ug.astype(BF16)
        useq = jnp.concatenate([ug[nls:nt], ug[0:nls], ug[nls:nt]], axis=0)
        yg_ref[g] = _s5_group(useq, t_ref, w_ref, v_ref, a_ref, ha_ref, hb_ref, g, ncs, nls, n_scan).astype(BF16)
    dst = lax.broadcasted_iota(I32, (tw, 2 * LANES), 1)
    pos_t = lax.broadcasted_iota(I32, (tw, 2 * LANES), 0)
    for p in range(st // 2):
        rows2 = jnp.zeros((nt, 2 * LANES), F32)
        for g in range(groups):
            hit = (((dst & (LANES - 1)) == g * S5_GROUP + (pos_t & (S5_GROUP - 1)))
                   & ((pos_t >> shift) == 2 * p + (dst >> lane_shift)))
            rows2 = rows2 + _dot(yg_ref[g], jnp.where(hit, 1.0, 0.0).astype(BF16))
        uf_ref[pl.ds(2 * p, nt, stride=st), :] = rows2[:, 0:LANES]
        uf_ref[pl.ds(2 * p + 1, nt, stride=st), :] = rows2[:, LANES:2 * LANES]
    y_ref[...] = uf_ref[...].astype(BF16)


def _s5_group(u, t_ref, w_ref, v_ref, a_ref, ha_ref, hb_ref, g, ncs, nls, n_scan):
    n = ncs + nls + ncs

    def mm(x, mat):
        mh, ml = _split(mat)
        return _dot(x, mh) + _dot(x, ml)

    def mm3(x, mat):
        xh, xl = _split(x)
        mh, ml = _split(mat)
        return _dot(xh, mh) + _dot(xl, mh) + _dot(xh, ml)

    rows = lax.broadcasted_iota(I32, (n, 1), 0)

    def scan(z, direction):
        bufs = (ha_ref, hb_ref)
        bufs[0][...] = z
        for kk in range(n_scan):
            src, dst = bufs[kk % 2], bufs[(kk + 1) % 2]
            sft = 1 << kk
            a1 = a_ref[direction, g, kk, 0:1, :]
            a2 = a_ref[direction, g, kk, 1:2, :]
            if sft >= n:
                dst[...] = src[...]
                continue
            if sft % 8 == 0:
                if direction == 0:
                    prev = src[0:n - sft, :]
                    dst[0:sft, :] = src[0:sft, :]
                    dst[sft:n, :] = src[sft:n, :] + a1 * prev + a2 * pltpu.roll(prev, S5_STATE, 1)
                else:
                    nxt = src[sft:n, :]
                    dst[n - sft:n, :] = src[n - sft:n, :]
                    dst[0:n - sft, :] = src[0:n - sft, :] + a1 * nxt + a2 * pltpu.roll(nxt, S5_STATE, 1)
            else:
                cur = src[...]
                if direction == 0:
                    sh = jnp.where(rows >= sft, pltpu.roll(cur, sft, 0), 0.0)
                else:
                    sh = jnp.where(rows < n - sft, pltpu.roll(cur, n - sft, 0), 0.0)
                dst[...] = cur + a1 * sh + a2 * pltpu.roll(sh, S5_STATE, 1)
        return bufs[n_scan % 2][...]

    y = mm(u, t_ref[g])
    hf = scan(mm(u, w_ref[0, g]), 0)
    hf_prev = jnp.where(rows >= 1, pltpu.roll(hf, 1, 0), 0.0)
    yf = mm3(hf_prev, v_ref[0, g])
    hb = scan(mm(u, w_ref[1, g]), 1)
    hb_next = jnp.where(rows < n - 1, pltpu.roll(hb, n - 1, 0), 0.0)
    yb = mm3(hb_next, v_ref[1, g])
    return jnp.concatenate([(y + yf + yb)[ncs:ncs + nls, :], (y + yf)[0:ncs, :] + yb[ncs + nls:n, :]], axis=0)


def _s5(p0, col0, tabs, n_lat, n_ctx):
    m = p0.shape[0]
    st = S5_STEP
    nls, ncs = n_lat // st, n_ctx // st
    n = ncs + nls + ncs
    nt = nls + ncs
    n_scan = max(1, (n - 1).bit_length())
    t_mat, w_mat, v_mat, apow = tabs(n_scan)
    tw = st * S5_GROUP
    gpl = LANES // S5_GROUP
    blk0 = col0 // LANES
    return pl.pallas_call(
        functools.partial(_s5_kernel, ncs=ncs, nls=nls, n_scan=n_scan),
        grid=(S5_CHANNELS // LANES,),
        in_specs=[
            pl.BlockSpec((m, LANES), lambda j: (0, blk0 + j)),
            pl.BlockSpec((gpl, tw, tw), lambda j: (j, 0, 0)),
            pl.BlockSpec((2, gpl, tw, 2 * S5_STATE), lambda j: (0, j, 0, 0)),
            pl.BlockSpec((2, gpl, 2 * S5_STATE, tw), lambda j: (0, j, 0, 0)),
            pl.BlockSpec((2, gpl, n_scan, 2, 2 * S5_STATE), lambda j: (0, j, 0, 0, 0)),
        ],
        out_specs=pl.BlockSpec((m, LANES), lambda j: (0, j)),
        out_shape=jax.ShapeDtypeStruct((m, S5_CHANNELS), BF16),
        scratch_shapes=[pltpu.VMEM((m, LANES), F32), pltpu.VMEM((st // 2, nt, 2 * LANES), BF16),
                        pltpu.VMEM((gpl, nt, tw), BF16),
                        pltpu.VMEM((n, 2 * S5_STATE), F32), pltpu.VMEM((n, 2 * S5_STATE), F32)],
        compiler_params=_cparams(1),
        name="s5",
    )(p0, t_mat, w_mat, v_mat, apow)


def _post_mix(y, x_ref, mod_ref, ln_ref, rw_ref, xo_ref, xp_ref, lg_ref, row0, n_lat):
    tm = y.shape[0]
    g1 = _select_rows(mod_ref, 2, row0, tm, n_lat)
    xn = _layer_norm(DEEPNORM_ALPHA * x_ref[...] + g1 * y, ln_ref[0:1, :], ln_ref[1:2, :])
    xo_ref[...] = xn
    sc2 = _select_rows(mod_ref, 4, row0, tm, n_lat)
    sh2 = _select_rows(mod_ref, 3, row0, tm, n_lat)
    x2 = xn * (1.0 + sc2) + sh2
    xp_ref[...] = _pack_bf16_pairs(x2)
    xh, xl = _split(x2)
    rh, rl = _split(rw_ref[...])
    lg_ref[...] = _dot_nt(rh, xh) + _dot_nt(rh, xl) + _dot_nt(rl, xh)


def _finish0_kernel(r_ref, g_ref, s_ref, x_ref, mod_ref, ln_ref, wglu_ref, bglu_ref, wout_ref, rw_ref,
                    xo_ref, xp_ref, lg_ref, *, tm, n_lat):
    row0 = pl.program_id(0) * tm
    ret = (r_ref[...].astype(F32) * _silu(g_ref[...].astype(F32))).astype(BF16)
    z = _gelu_tanh(s_ref[...].astype(F32))
    zb = z.astype(BF16)
    gate = _sigmoid(_dot(zb, wglu_ref[...]) + bglu_ref[...])
    s5o = (z * gate).astype(BF16)
    half = ret.shape[1]
    y = _dot(ret, wout_ref[0:half, :]) + _dot(s5o, wout_ref[half:, :])
    _post_mix(y, x_ref, mod_ref, ln_ref, rw_ref, xo_ref, xp_ref, lg_ref, row0, n_lat)


def _const_spec(shape):
    nd = len(shape)
    return pl.BlockSpec(shape, lambda i: (0,) * nd)


def _finish0(rn, p0, s5y, x_all, mod, ln, wglu, bglu, wout, rw, n_lat):
    m, d = x_all.shape
    tm = ROW_TILE
    half = d // 2
    outs = pl.pallas_call(
        functools.partial(_finish0_kernel, tm=tm, n_lat=n_lat),
        grid=(m // tm,),
        in_specs=[
            pl.BlockSpec((tm, half), lambda i: (i, 0)),
            pl.BlockSpec((tm, half), lambda i: (i, 3)),
            pl.BlockSpec((tm, half), lambda i: (i, 0)),
            pl.BlockSpec((tm, d), lambda i: (i, 0)),
            _const_spec((2, 6, d)),
            _const_spec((2, d)),
            _const_spec((half, half)),
            _const_spec((1, half)),
            _const_spec((d, d)),
            _const_spec((N_EXPERTS, d)),
        ],
        out_specs=[
            pl.BlockSpec((tm, d), lambda i: (i, 0)),
            pl.BlockSpec((tm, half), lambda i: (i, 0)),
            pl.BlockSpec((N_EXPERTS, tm), lambda i: (0, i)),
        ],
        out_shape=[
            jax.ShapeDtypeStruct((m, d), F32),
            jax.ShapeDtypeStruct((m, half), U32),
            jax.ShapeDtypeStruct((N_EXPERTS, m), F32),
        ],
        compiler_params=_cparams(1),
        name="finish0",
    )(rn, p0, s5y, x_all, mod, ln, wglu, bglu, wout, rw)
    return outs


def _finish1_kernel(og_ref, od_ref, x_ref, mod_ref, ln_ref, wout_ref, rw_ref, xo_ref, xp_ref, lg_ref, *, tm, n_lat):
    row0 = pl.program_id(0) * tm
    half = og_ref.shape[1]
    y = _dot(og_ref[...], wout_ref[0:half, :]) + _dot(od_ref[...], wout_ref[half:, :])
    _post_mix(y, x_ref, mod_ref, ln_ref, rw_ref, xo_ref, xp_ref, lg_ref, row0, n_lat)


def _finish1(og, od, x_all, mod, ln, wout, rw, n_lat):
    d = x_all.shape[1]
    m = og.shape[0]
    tm = ROW_TILE
    half = d // 2
    return pl.pallas_call(
        functools.partial(_finish1_kernel, tm=tm, n_lat=n_lat),
        grid=(m // tm,),
        in_specs=[
            pl.BlockSpec((tm, half), lambda i: (i, 0)),
            pl.BlockSpec((tm, half), lambda i: (i, 0)),
            pl.BlockSpec((tm, d), lambda i: (i, 0)),
            _const_spec((2, 6, d)),
            _const_spec((2, d)),
            _const_spec((d, d)),
            _const_spec((N_EXPERTS, d)),
        ],
        out_specs=[
            pl.BlockSpec((tm, d), lambda i: (i, 0)),
            pl.BlockSpec((tm, half), lambda i: (i, 0)),
            pl.BlockSpec((N_EXPERTS, tm), lambda i: (0, i)),
        ],
        out_shape=[
            jax.ShapeDtypeStruct((m, d), F32),
            jax.ShapeDtypeStruct((m, half), U32),
            jax.ShapeDtypeStruct((N_EXPERTS, m), F32),
        ],
        compiler_params=_cparams(1),
        name="finish1",
    )(og, od, x_all, mod, ln, wout, rw)


def _route_kernel(lg_ref, bias_ref, e_ref, r_ref, w_ref, cnt_ref, carry_ref, *, tb):
    ne = N_EXPERTS
    gsz = ne // N_EXPERT_GROUPS
    neg = -jnp.inf

    @pl.when(pl.program_id(0) == 0)
    def _():
        carry_ref[...] = jnp.zeros_like(carry_ref)

    s = _sigmoid(lg_ref[...])
    sel = s + bias_ref[...]
    gs = []
    for g in range(N_EXPERT_GROUPS):
        blk = sel[g * gsz:(g + 1) * gsz, :]
        m1 = jnp.max(blk, axis=0, keepdims=True)
        n_eq = jnp.sum(jnp.where(blk == m1, 1.0, 0.0), axis=0, keepdims=True)
        m2 = jnp.max(jnp.where(blk < m1, blk, neg), axis=0, keepdims=True)
        gs.append(m1 + jnp.where(n_eq >= 2.0, m1, m2))
    masked = []
    for g in range(N_EXPERT_GROUPS):
        ahead = jnp.zeros_like(gs[g])
        for h in range(N_EXPERT_GROUPS):
            if h == g:
                continue
            beats = ((gs[h] > gs[g]) | (gs[h] == gs[g])) if h < g else (gs[h] > gs[g])
            ahead = ahead + jnp.where(beats, 1.0, 0.0)
        masked.append(jnp.where(ahead < float(TOPK_GROUPS), sel[g * gsz:(g + 1) * gsz, :], neg))
    selm = jnp.concatenate(masked, axis=0)
    eid = lax.broadcasted_iota(I32, (ne, 1), 0)
    ahead = jnp.zeros_like(selm)
    for e in range(ne):
        row = selm[e:e + 1, :]
        beats = (row > selm) | ((row == selm) & (eid > e))
        ahead = ahead + jnp.where(beats, 1.0, 0.0)
    chosen = ahead < float(TOP_K)
    member = jnp.where(chosen, 1.0, 0.0)
    ssel = jnp.where(chosen, s, 0.0)
    wd = ssel / jnp.sum(ssel, axis=0, keepdims=True) * ROUTED_SCALE
    mb = member.astype(BF16)
    ti = lax.broadcasted_iota(I32, (tb, tb), 0)
    tj = lax.broadcasted_iota(I32, (tb, tb), 1)
    tri = jnp.where(ti < tj, 1.0, 0.0).astype(BF16)
    rank = carry_ref[:, 0:1] + _dot(mb, tri)
    carry_ref[...] = carry_ref[...] + jnp.sum(member, axis=1, keepdims=True)
    cnt_ref[...] = carry_ref[...]
    ei = lax.broadcasted_iota(I32, (ne, ne), 0)
    ej = lax.broadcasted_iota(I32, (ne, ne), 1)
    low = jnp.where(ej < ei, 1.0, 0.0).astype(BF16)
    slot = _dot(low, mb)
    eidf = eid.astype(F32)
    for k in range(TOP_K):
        hit = chosen & (slot == float(k))
        e_ref[k:k + 1, :] = jnp.sum(jnp.where(hit, eidf, 0.0), axis=0, keepdims=True).astype(I32)
        r_ref[k:k + 1, :] = jnp.sum(jnp.where(hit, rank, 0.0), axis=0, keepdims=True).astype(I32)
        w_ref[k:k + 1, :] = jnp.sum(jnp.where(hit, wd, 0.0), axis=0, keepdims=True)


def _route(logits_t, router_bias, tm):
    e, t = logits_t.shape
    tb = 512 if t % 512 == 0 else 256
    eidx, rank, wts, counts = pl.pallas_call(
        functools.partial(_route_kernel, tb=tb),
        grid=(t // tb,),
        in_specs=[
            pl.BlockSpec((e, tb), lambda i: (0, i)),
            pl.BlockSpec((e, 1), lambda i: (0, 0)),
        ],
        out_specs=[
            pl.BlockSpec((TOP_K, tb), lambda i: (0, i)),
            pl.BlockSpec((TOP_K, tb), lambda i: (0, i)),
            pl.BlockSpec((TOP_K, tb), lambda i: (0, i)),
            pl.BlockSpec((e, LANES), lambda i: (0, 0)),
        ],
        out_shape=[
            jax.ShapeDtypeStruct((TOP_K, t), I32),
            jax.ShapeDtypeStruct((TOP_K, t), I32),
            jax.ShapeDtypeStruct((TOP_K, t), F32),
            jax.ShapeDtypeStruct((e, LANES), F32),
        ],
        scratch_shapes=[pltpu.VMEM((e, LANES), F32)],
        compiler_params=_cparams(1),
        name="moe_route",
    )(logits_t, router_bias.astype(F32).reshape(e, 1))
    counts = counts[:, 0].astype(I32)
    padded = (counts + tm - 1) // tm * tm
    pad_end = jnp.cumsum(padded)
    pad_start = pad_end - padded
    ids = jnp.arange(e, dtype=I32)
    start_of = jnp.sum(jnp.where(eidx[:, :, None] == ids, pad_start, 0), axis=-1)
    dest = (start_of + rank).T
    n_blocks = (t * TOP_K + e * (tm - 1)) // tm
    blk_start = jnp.arange(n_blocks, dtype=I32) * tm
    blk_expert = jnp.minimum(jnp.sum((blk_start[:, None] >= pad_end[None, :]).astype(I32), axis=1), e - 1)
    n_used = (pad_end[-1] // tm).astype(I32).reshape(1)
    return dest, wts.T, blk_expert, n_used, n_blocks, pad_end.astype(I32), padded.astype(I32)


def _dispatch_kernel(pad_end_ref, padded_ref, dest_ref, x_ref, xs_ref, zero_ref, sem, zsem, *, tb, tm):
    @pl.when(pl.program_id(0) == 0)
    def _():
        zero_ref[...] = jnp.zeros_like(zero_ref)

        def fill(e):
            first = pl.multiple_of(pad_end_ref[e] - tm, tm)
            return pltpu.make_async_copy(zero_ref, xs_ref.at[pl.ds(first, tm)], zsem)

        for e in range(N_EXPERTS):
            @pl.when(padded_ref[e] > 0)
            def _():
                fill(e).start()
        for e in range(N_EXPERTS):
            @pl.when(padded_ref[e] > 0)
            def _():
                fill(e).wait()

    def row_copy(r, k):
        d = dest_ref[0, 0, r * TOP_K + k]
        return pltpu.make_async_copy(x_ref.at[pl.ds(r, 1)], xs_ref.at[pl.ds(d, 1)], sem)

    def start(r, carry):
        for k in range(TOP_K):
            row_copy(r, k).start(priority=k % 2)
        return carry

    def wait(r, carry):
        for k in range(TOP_K):
            row_copy(r, k).wait()
        return carry

    lax.fori_loop(0, tb, start, 0)
    lax.fori_loop(0, tb, wait, 0)


def _dispatch(xp, dest, pad_end, padded, n_pad, tm):
    t, w = xp.shape
    tb = ROW_TILE
    smem = pl.BlockSpec(memory_space=pltpu.SMEM)
    return pl.pallas_call(
        functools.partial(_dispatch_kernel, tb=tb, tm=tm),
        grid=(t // tb,),
        in_specs=[
            smem,
            smem,
            pl.BlockSpec((1, 1, tb * TOP_K), lambda i: (i, 0, 0), memory_space=pltpu.SMEM),
            pl.BlockSpec((tb, w), lambda i: (i, 0)),
        ],
        out_specs=pl.BlockSpec(memory_space=pl.ANY),
        out_shape=jax.ShapeDtypeStruct((n_pad, w), U32),
        scratch_shapes=[pltpu.VMEM((tm, w), U32), pltpu.SemaphoreType.DMA(()), pltpu.SemaphoreType.DMA(())],
        compiler_params=_cparams(1),
        name="moe_dispatch",
    )(pad_end, padded, dest.reshape(t // tb, 1, tb * TOP_K), xp)


def _expert_kernel(be_ref, nu_ref, xs_ref, wg_ref, wu_ref, wd_ref, y_ref, wgb_ref, wub_ref, wdb_ref):
    b = pl.program_id(0)
    e = be_ref[b]
    prev = be_ref[jnp.maximum(b - 1, 0)]

    @pl.when((b == 0) | (e != prev))
    def _():
        wgb_ref[...] = wg_ref[0].astype(BF16)
        wub_ref[...] = wu_ref[0].astype(BF16)
        wdb_ref[...] = wd_ref[0].astype(BF16)

    @pl.when(b < nu_ref[0])
    def _():
        lo, hi = _unpack_bf16_pairs(xs_ref[...])
        lo = lo.astype(BF16)
        hi = hi.astype(BF16)
        half = lo.shape[1]
        gate = _dot(lo, wgb_ref[0:half, :]) + _dot(hi, wgb_ref[half:, :])
        up = _dot(lo, wub_ref[0:half, :]) + _dot(hi, wub_ref[half:, :])
        hid = (_silu(gate) * up).astype(BF16)
        y_ref[...] = _pack_bf16_pairs(_dot(hid, wdb_ref[...]))

    @pl.when(b >= nu_ref[0])
    def _():
        y_ref[...] = jnp.zeros_like(y_ref)


def _experts(xs, blk_expert, n_used, w_gate, w_up, w_down):
    n_pad, w = xs.shape
    tm = MOE_TM
    d, hdn = w_gate.shape[1], w_gate.shape[2]
    return pl.pallas_call(
        _expert_kernel,
        grid_spec=pltpu.PrefetchScalarGridSpec(
            num_scalar_prefetch=2,
            grid=(n_pad // tm,),
            in_specs=[
                pl.BlockSpec((tm, w), lambda b, be, nu: (b, 0)),
                pl.BlockSpec((1, d, hdn), lambda b, be, nu: (be[b], 0, 0)),
                pl.BlockSpec((1, d, hdn), lambda b, be, nu: (be[b], 0, 0)),
                pl.BlockSpec((1, hdn, d), lambda b, be, nu: (be[b], 0, 0)),
            ],
            out_specs=pl.BlockSpec((tm, w), lambda b, be, nu: (b, 0)),
            scratch_shapes=[pltpu.VMEM((d, hdn), BF16), pltpu.VMEM((d, hdn), BF16), pltpu.VMEM((hdn, d), BF16)],
        ),
        out_shape=jax.ShapeDtypeStruct((n_pad, w), U32),
        compiler_params=_cparams(1),
        name="moe_experts",
    )(blk_expert, n_used, xs, w_gate, w_up, w_down)


def _combine_kernel(dest_ref, xp_ref, ws_ref, x_ref, mod_ref, ln_ref, sg_ref, su_ref, sd_ref, y_hbm,
                    xo_ref, ybuf, sem, *, tb, n_lat):
    row0 = pl.program_id(0) * tb

    def row_copy(r, k):
        d = dest_ref[0, 0, r * TOP_K + k]
        return pltpu.make_async_copy(y_hbm.at[pl.ds(d, 1)], ybuf.at[pl.ds(k * tb + r, 1)], sem)

    def start(r, carry):
        for k in range(TOP_K):
            row_copy(r, k).start(priority=k % 2)
        return carry

    def wait(r, carry):
        for k in range(TOP_K):
            row_copy(r, k).wait()
        return carry

    lax.fori_loop(0, tb, start, 0)
    lo, hi = _unpack_bf16_pairs(xp_ref[...])
    lo = lo.astype(BF16)
    hi = hi.astype(BF16)
    half = lo.shape[1]
    gate = _dot(lo, sg_ref[0:half, :]) + _dot(hi, sg_ref[half:, :])
    up = _dot(lo, su_ref[0:half, :]) + _dot(hi, su_ref[half:, :])
    f = _dot((_silu(gate) * up).astype(BF16), sd_ref[...])
    lax.fori_loop(0, tb, wait, 0)
    acc_lo = jnp.zeros((tb, half), F32)
    acc_hi = jnp.zeros((tb, half), F32)
    for k in range(TOP_K):
        ylo, yhi = _unpack_bf16_pairs(ybuf[k * tb:(k + 1) * tb, :])
        wk = ws_ref[:, k:k + 1]
        acc_lo = acc_lo + wk * ylo
        acc_hi = acc_hi + wk * yhi
    f = f + jnp.concatenate([acc_lo, acc_hi], axis=1)
    g2 = _select_rows(mod_ref, 5, row0, tb, n_lat)
    xo_ref[...] = _layer_norm(DEEPNORM_ALPHA * x_ref[...] + g2 * f, ln_ref[0:1, :], ln_ref[1:2, :])


def _combine(dest, xp, wsel, x_all, mod, ln, sg, su, sd, y, n_lat):
    t, w = xp.shape
    d = x_all.shape[1]
    tb = ROW_TILE
    hdn = sg.shape[1]
    return pl.pallas_call(
        functools.partial(_combine_kernel, tb=tb, n_lat=n_lat),
        grid=(t // tb,),
        in_specs=[
            pl.BlockSpec((1, 1, tb * TOP_K), lambda i: (i, 0, 0), memory_space=pltpu.SMEM),
            pl.BlockSpec((tb, w), lambda i: (i, 0)),
            pl.BlockSpec((tb, TOP_K), lambda i: (i, 0)),
            pl.BlockSpec((tb, d), lambda i: (i, 0)),
            _const_spec((2, 6, d)),
            _const_spec((2, d)),
            _const_spec((d, hdn)),
            _const_spec((d, hdn)),
            _const_spec((hdn, d)),
            pl.BlockSpec(memory_space=pl.ANY),
        ],
        out_specs=pl.BlockSpec((tb, d), lambda i: (i, 0)),
        out_shape=jax.ShapeDtypeStruct((t, d), F32),
        scratch_shapes=[pltpu.VMEM((TOP_K * tb, w), U32), pltpu.SemaphoreType.DMA(())],
        compiler_params=_cparams(1),
        name="moe_combine",
    )(dest.reshape(t // tb, 1, tb * TOP_K), xp, wsel, x_all, mod, ln, sg, su, sd, y)


def _moe(xp, logits, x_res, mod, ln, router_bias, w_gate, w_up, w_down, sg, su, sd, n_lat):
    dest, wsel, blk_expert, n_used, n_blocks, pad_end, padded = _route(logits, router_bias, MOE_TM)
    xs = _dispatch(xp, dest, pad_end, padded, n_blocks * MOE_TM, MOE_TM)
    y = _experts(xs, blk_expert, n_used, w_gate, w_up, w_down)
    return _combine(dest, xp, wsel, x_res, mod, ln, sg.astype(BF16), su.astype(BF16), sd.astype(BF16), y, n_lat)


def _rot_lanes(xs, cos, sin):
    q = HEAD_DIM // 4
    lane = lax.broadcasted_iota(I32, xs.shape, 1)
    partner = jnp.where((lane % (2 * q)) < q, pltpu.roll(xs, HEAD_DIM - q, 1), pltpu.roll(xs, q, 1))
    return xs * cos + partner * sin


def _projq_kernel(x_ref, mod_ref, w_ref, cos_ref, sin_ref, o_ref, xm_ref, *, tm, n_lat):
    i = pl.program_id(0)
    j = pl.program_id(1)

    @pl.when(j == 0)
    def _():
        xm_ref[...] = _modulate_rows(x_ref[...], mod_ref, 0, 1, i * tm, n_lat).astype(BF16)

    acc = _dot(xm_ref[...], w_ref[...])
    n_heads = acc.shape[1] // HEAD_DIM

    @pl.when(j == 0)
    def _():
        for c in range(n_heads):
            xs = acc[:, c * HEAD_DIM:(c + 1) * HEAD_DIM]
            xs = xs * lax.rsqrt(jnp.mean(xs * xs, axis=-1, keepdims=True) + LN_EPS)
            o_ref[:, c * HEAD_DIM:(c + 1) * HEAD_DIM] = _rot_lanes(xs, cos_ref[0], sin_ref[0]).astype(BF16)

    @pl.when(j == 1)
    def _():
        for c in range(n_heads):
            xs = acc[:, c * HEAD_DIM:(c + 1) * HEAD_DIM]
            o_ref[:, c * HEAD_DIM:(c + 1) * HEAD_DIM] = _rot_lanes(xs, cos_ref[1], sin_ref[1]).astype(BF16)


def _projq(x_all, mod, w_bf, cos2, sin2, n_lat):
    d = x_all.shape[1]
    n = w_bf.shape[1]
    tm = 512 if n_lat % 512 == 0 else 256
    tn = n // 2
    return pl.pallas_call(
        functools.partial(_projq_kernel, tm=tm, n_lat=n_lat),
        grid=(n_lat // tm, 2),
        in_specs=[
            pl.BlockSpec((tm, d), lambda i, j: (i, 0)),
            pl.BlockSpec((2, 6, d), lambda i, j: (0, 0, 0)),
            pl.BlockSpec((d, tn), lambda i, j: (0, j)),
            pl.BlockSpec((2, tm, HEAD_DIM), lambda i, j: (0, i, 0)),
            pl.BlockSpec((2, tm, HEAD_DIM), lambda i, j: (0, i, 0)),
        ],
        out_specs=pl.BlockSpec((tm, tn), lambda i, j: (i, j)),
        out_shape=jax.ShapeDtypeStruct((n_lat, n), BF16),
        scratch_shapes=[pltpu.VMEM((tm, d), BF16)],
        compiler_params=_cparams(2),
        name="proj1_q",
    )(x_all, mod, w_bf, cos2, sin2)


def _projkv_kernel(x_ref, mod_ref, wv_ref, wkt_ref, cos_ref, sin_ref, v_ref, kt_ref, *, tm, n_lat):
    i = pl.program_id(0)
    xm = _modulate_rows(x_ref[...], mod_ref, 0, 1, i * tm, n_lat).astype(BF16)
    v_ref[...] = _dot(xm, wv_ref[...]).astype(BF16)
    kt = _dot_nt(wkt_ref[...], xm)
    q = HEAD_DIM // 4
    n_gk = GQA_KV_HEADS * HEAD_DIM
    for c in range(kt.shape[0] // HEAD_DIM):
        xs = kt[c * HEAD_DIM:(c + 1) * HEAD_DIM, :]
        t = 0 if c * HEAD_DIM < n_gk else 1
        if t == 0:
            xs = xs * lax.rsqrt(jnp.mean(xs * xs, axis=0, keepdims=True) + LN_EPS)
        partner = jnp.concatenate([xs[q:2 * q], xs[0:q], xs[3 * q:4 * q], xs[2 * q:3 * q]], axis=0)
        kt_ref[c * HEAD_DIM:(c + 1) * HEAD_DIM, :] = (xs * cos_ref[t] + partner * sin_ref[t]).astype(BF16)


def _projkv(x_all, mod, wv_bf, wkt_bf, cos_t, sin_t, n_lat):
    m, d = x_all.shape
    nv = wv_bf.shape[1]
    nk = wkt_bf.shape[0]
    tm = 640 if m % 640 == 0 else 256
    return pl.pallas_call(
        functools.partial(_projkv_kernel, tm=tm, n_lat=n_lat),
        grid=(m // tm,),
        in_specs=[
            pl.BlockSpec((tm, d), lambda i: (i, 0)),
            _const_spec((2, 6, d)),
            _const_spec((d, nv)),
            _const_spec((nk, d)),
            pl.BlockSpec((2, HEAD_DIM, tm), lambda i: (0, 0, i)),
            pl.BlockSpec((2, HEAD_DIM, tm), lambda i: (0, 0, i)),
        ],
        out_specs=[
            pl.BlockSpec((tm, nv), lambda i: (i, 0)),
            pl.BlockSpec((nk, tm), lambda i: (0, i)),
        ],
        out_shape=[jax.ShapeDtypeStruct((m, nv), BF16), jax.ShapeDtypeStruct((nk, m), BF16)],
        compiler_params=_cparams(1),
        name="proj1_kv",
    )(x_all, mod, wv_bf, wkt_bf, cos_t, sin_t)


def _flash(qs, kt_ref, v_ref, scratch, tk, stack):
    (qs_ref, s_refs, pm_refs, m_ref, l_ref, acc_ref) = scratch
    rows = qs_ref.shape[0]
    lk = v_ref.shape[0]
    n = lk // tk
    rs = rows // stack
    part = rows // len(qs)
    for a, q_part in enumerate(qs):
        qs_ref[a * part:(a + 1) * part, :] = q_part
    m_ref[...] = jnp.full(m_ref.shape, -jnp.inf, F32)
    l_ref[...] = jnp.zeros(l_ref.shape, F32)
    acc_ref[...] = jnp.zeros(acc_ref.shape, F32)

    n_grp = tk // LANES
    dv = acc_ref.shape[1]
    rb = FLASH_BLOCK if rs % FLASH_BLOCK == 0 else rs
    n_rb = rows // rb

    def score_rows(j, r, s_ref, pm_ref):
        off = pl.multiple_of(j * tk, tk)
        ka = 0 if stack == 1 else pl.multiple_of((r // rs) * HEAD_DIM, HEAD_DIM)
        s = _dot(qs_ref[pl.ds(r, rb), :], kt_ref[pl.ds(ka, HEAD_DIM), pl.ds(off, tk)])
        s_ref[pl.ds(r, rb), :] = s
        pm = s[:, 0:LANES]
        for c in range(1, n_grp):
            pm = jnp.maximum(pm, s[:, c * LANES:(c + 1) * LANES])
        pm_ref[pl.ds(r, rb), :] = pm

    sb = FLASH_SUB if rb % FLASH_SUB == 0 else rb

    def attend_rows(j, r, s_ref, pm_ref):
        off = pl.multiple_of(j * tk, tk)
        blk = pl.ds(r, sb)
        m_prev = m_ref[blk, :]
        m_next = jnp.maximum(m_prev, jnp.max(pm_ref[blk, :], axis=1, keepdims=True))
        alpha = jnp.exp2(m_prev - m_next)
        m_ref[blk, :] = m_next
        pieces = []
        for q in range(sb // FLASH_ROWS):
            sub = pl.ds(pl.multiple_of(r + q * FLASH_ROWS, FLASH_ROWS), FLASH_ROWS)
            mb = m_next[q * FLASH_ROWS:(q + 1) * FLASH_ROWS]
            lsum = alpha[q * FLASH_ROWS:(q + 1) * FLASH_ROWS] * l_ref[sub, :]
            row = []
            for c in range(n_grp):
                pc = jnp.exp2(s_ref[sub, c * LANES:(c + 1) * LANES] - mb)
                lsum = lsum + pc
                row.append(pc.astype(BF16))
            l_ref[sub, :] = lsum
            pieces.append(jnp.concatenate(row, axis=1))
        pv = _dot(jnp.concatenate(pieces, axis=0), v_ref[pl.ds(off, tk), :])
        for c in range(dv // LANES):
            cols = slice(c * LANES, (c + 1) * LANES)
            acc_ref[blk, cols] = alpha * acc_ref[blk, cols] + pv[:, cols]

    def step_parity(j, cur, with_scores):
        nxt = 1 - cur

        def body(i, carry):
            r = pl.multiple_of(i * rb, rb)
            if with_scores:
                score_rows(j + 1, r, s_refs[nxt], pm_refs[nxt])
            for h in range(rb // sb):
                attend_rows(j, pl.multiple_of(r + h * sb, sb), s_refs[cur], pm_refs[cur])
            return carry

        lax.fori_loop(0, n_rb, body, 0)

    def first_scores(i, carry):
        score_rows(0, pl.multiple_of(i * rb, rb), s_refs[0], pm_refs[0])
        return carry

    lax.fori_loop(0, n_rb, first_scores, 0)

    def pair(i, carry):
        j = 2 * i
        step_parity(j, 0, True)
        step_parity(j + 1, 1, True)
        return carry

    lax.fori_loop(0, (n - 1) // 2, pair, 0)
    if (n - 1) % 2:
        step_parity(n - 2, (n - 2) % 2, True)
    step_parity(n - 1, (n - 1) % 2, False)
    return acc_ref[...] / jnp.sum(l_ref[...], axis=1, keepdims=True)


def _flash_scratch(rows, tk, dv):
    stat = pltpu.VMEM((rows, LANES), F32)
    return [pltpu.VMEM((rows, HEAD_DIM), BF16),
            pltpu.VMEM((rows, tk), F32), pltpu.VMEM((rows, tk), F32), stat, stat,
            stat, stat, pltpu.VMEM((rows, dv), F32)]


def _unpack_flash_scratch(refs):
    qs_ref, s0, s1, pm0, pm1, m_ref, l_ref, acc_ref = refs
    return (qs_ref, (s0, s1), (pm0, pm1), m_ref, l_ref, acc_ref)


def _gqa_kernel(q_ref, kt_ref, v_ref, o_ref, *scratch, tk):
    tq = q_ref.shape[0]
    qs = [q_ref[:, a * HEAD_DIM:(a + 1) * HEAD_DIM] for a in range(GQA_GROUP)]
    o = _flash(qs, kt_ref, v_ref, _unpack_flash_scratch(scratch), tk, 1)
    for a in range(GQA_GROUP):
        o_ref[:, a * HEAD_DIM:(a + 1) * HEAD_DIM] = o[a * tq:(a + 1) * tq].astype(BF16)


def _diff_kernel(lam_ref, q_ref, kt_ref, v_ref, g_ref, o_ref, *scratch, tk, out_scale):
    tq = q_ref.shape[0]
    qs = [q_ref[:, 0:HEAD_DIM], q_ref[:, HEAD_DIM:2 * HEAD_DIM]]
    o = _flash(qs, kt_ref, v_ref, _unpack_flash_scratch(scratch), tk, 2)
    od = o[0:tq] - lam_ref[0] * o[tq:2 * tq]
    od = od * lax.rsqrt(jnp.mean(od * od, axis=-1, keepdims=True) + LN_EPS)
    o_ref[...] = (od * g_ref[...] * out_scale).astype(BF16)


def _key_chunk(lk):
    for tk in (1280, 1024, 640, 512, 256, 128):
        if lk % tk == 0 and lk // tk >= 2:
            return tk
    return lk


def _attention(q1, kt, v1, lam, subln_g, lambda_init, n_lat):
    lk = v1.shape[0]
    tk = _key_chunk(lk)
    tq = 256
    gw = GQA_GROUP * HEAD_DIM
    n_gv = GQA_KV_HEADS * HEAD_DIM
    once = pl.Buffered(1)
    og = pl.pallas_call(
        functools.partial(_gqa_kernel, tk=tk),
        grid=(GQA_KV_HEADS, n_lat // tq),
        in_specs=[
            pl.BlockSpec((tq, gw), lambda g, i: (i, g)),
            pl.BlockSpec((HEAD_DIM, lk), lambda g, i: (g, 0), pipeline_mode=once),
            pl.BlockSpec((lk, HEAD_DIM), lambda g, i: (0, g), pipeline_mode=once),
        ],
        out_specs=pl.BlockSpec((tq, gw), lambda g, i: (i, g)),
        out_shape=jax.ShapeDtypeStruct((n_lat, GQA_HEADS * HEAD_DIM), BF16),
        scratch_shapes=_flash_scratch(GQA_GROUP * tq, tk, HEAD_DIM),
        compiler_params=_cparams(2),
        name="gqa_attention",
    )(q1, kt, v1)
    q_off = GQA_HEADS * HEAD_DIM // DIFF_V_DIM
    kv_off = n_gv // DIFF_V_DIM
    tq = 512 if n_lat % 512 == 0 else 256
    od = pl.pallas_call(
        functools.partial(_diff_kernel, tk=tk, out_scale=1.0 - lambda_init),
        grid=(DIFF_HEADS, n_lat // tq),
        in_specs=[
            pl.BlockSpec(memory_space=pltpu.SMEM),
            pl.BlockSpec((tq, DIFF_V_DIM), lambda h, i: (i, q_off + h)),
            pl.BlockSpec((DIFF_V_DIM, lk), lambda h, i: (kv_off + h, 0), pipeline_mode=once),
            pl.BlockSpec((lk, DIFF_V_DIM), lambda h, i: (0, kv_off + h), pipeline_mode=once),
            pl.BlockSpec((1, DIFF_V_DIM), lambda h, i: (0, 0)),
        ],
        out_specs=pl.BlockSpec((tq, DIFF_V_DIM), lambda h, i: (i, h)),
        out_shape=jax.ShapeDtypeStruct((n_lat, DIFF_HEADS * DIFF_V_DIM), BF16),
        scratch_shapes=_flash_scratch(2 * tq, tk, DIFF_V_DIM),
        compiler_params=_cparams(2),
        name="diff_attention",
    )(lam, q1, kt, v1, subln_g.reshape(1, DIFF_V_DIM).astype(F32))
    return og, od


def _router_operand(router_w):
    return router_w.astype(F32).T


def kernel(x, c, ctx, c_ctx, mod_w, mod_b, ln_g, ln_b, ev_w_in, ev_w_out, ret_log_decay, s5_a_re, s5_a_im, s5_log_dt, s5_b_re, s5_b_im, s5_c_re, s5_c_im, s5_d, s5_w_glu, s5_b_glu, od_w_in, od_w_out, qk_norm_g, diff_lambda, diff_subln_g, router_w, router_bias, exp_w_gate, exp_w_up, exp_w_down, sh_w_gate, sh_w_up, sh_w_down):
    b_, n_lat, d = x.shape
    n_ctx = ctx.shape[1]
    assert b_ == 1 and d == D_MODEL and n_lat % RET_CHUNK == 0 and n_ctx % RET_CHUNK == 0
    m_all = n_lat + n_ctx
    x_all = jnp.concatenate([x[0], ctx[0]], axis=0)
    mods = _modulation(c, c_ctx, mod_w, mod_b)

    cos0, sin0 = _rope_tables(n_lat, n_ctx, RET_DIM)
    p0 = _proj0(x_all, mods[0], ev_w_in[0].astype(BF16), cos0, sin0, n_lat)
    rn = _retention(p0, ret_log_decay[0], n_lat, n_ctx)
    s5_tabs = functools.partial(_s5_tables, s5_a_re[0], s5_a_im[0], s5_log_dt[0], s5_b_re[0], s5_b_im[0],
                                s5_c_re[0], s5_c_im[0], s5_d[0])
    s5y = _s5(p0, 4 * RET_HEADS * RET_DIM, s5_tabs, n_lat, n_ctx)
    ln0 = jnp.stack([ln_g[0], ln_b[0]], axis=1)
    x_all, xp, logits = _finish0(rn, p0, s5y, x_all, mods[0], ln0[0], s5_w_glu[0].astype(BF16),
                                 s5_b_glu[0].reshape(1, -1).astype(F32), ev_w_out[0].astype(BF16),
                                 _router_operand(router_w[0]), n_lat)
    x_all = _moe(xp, logits, x_all, mods[0], ln0[1], router_bias[0], exp_w_gate[0], exp_w_up[0], exp_w_down[0],
                 sh_w_gate[0], sh_w_up[0], sh_w_down[0], n_lat)

    i = 1
    lambda_init = 0.8 - 0.6 * math.exp(-0.3 * i)
    gq_w = GQA_HEADS * HEAD_DIM
    gk_w = GQA_KV_HEADS * HEAD_DIM
    dq_w = DIFF_HEADS * 2 * HEAD_DIM
    cuts = [gq_w, gq_w + gk_w, gq_w + 2 * gk_w, gq_w + 2 * gk_w + dq_w, gq_w + 2 * gk_w + 2 * dq_w]
    w_in = od_w_in[0]
    w_gq, w_gk, w_gv, w_dq, w_dk, w_dv = (w_in[:, a:b] for a, b in zip([0] + cuts, cuts + [w_in.shape[1]]))
    cos1, sin1 = _rope_tables(n_lat, n_ctx, HEAD_DIM)
    qscale = HEAD_DIM ** -0.5 * LOG2E
    gq_gain = qk_norm_g[0, 0].astype(F32)
    gk_gain = qk_norm_g[0, 1].astype(F32)
    quarter = HEAD_DIM // 4

    def partner_gain(g):
        return jnp.concatenate([g[quarter:2 * quarter], g[0:quarter], g[3 * quarter:], g[2 * quarter:3 * quarter]])

    cos_q = jnp.stack([cos1[:n_lat] * gq_gain[None, :], cos1[:n_lat]]) * qscale
    sin_q = jnp.stack([sin1[:n_lat] * partner_gain(gq_gain)[None, :], sin1[:n_lat]]) * qscale
    cos_k = jnp.stack([cos1.T * gk_gain[:, None], cos1.T])
    sin_k = jnp.stack([sin1.T * partner_gain(gk_gain)[:, None], sin1.T])
    q1 = _projq(x_all, mods[1], jnp.concatenate([w_gq, w_dq], axis=1).astype(BF16), cos_q, sin_q, n_lat)
    v1, kt = _projkv(x_all, mods[1], jnp.concatenate([w_gv, w_dv], axis=1).astype(BF16),
                     jnp.concatenate([w_gk, w_dk], axis=1).T.astype(BF16), cos_k, sin_k, n_lat)
    lf = diff_lambda[0].astype(F32)
    lam = (jnp.exp(jnp.sum(lf[0] * lf[1])) - jnp.exp(jnp.sum(lf[2] * lf[3])) + lambda_init).reshape(1)
    og, od = _attention(q1, kt, v1, lam, diff_subln_g[0], lambda_init, n_lat)
    ln1 = jnp.stack([ln_g[1], ln_b[1]], axis=1)
    x_lat, xp, logits = _finish1(og, od, x_all, mods[1], ln1[0], od_w_out[0].astype(BF16),
                                 _router_operand(router_w[1]), n_lat)
    x_lat = _moe(xp, logits, x_lat, mods[1], ln1[1], router_bias[1], exp_w_gate[1], exp_w_up[1], exp_w_down[1],
                 sh_w_gate[1], sh_w_up[1], sh_w_down[1], n_lat)
    return x_lat[None]
```

```python
import functools
import math

import jax
import jax.numpy as jnp
from jax import lax
from jax.experimental import pallas as pl
from jax.experimental.pallas import tpu as pltpu

F32 = jnp.float32
BF16 = jnp.bfloat16
U32 = jnp.uint32
I32 = jnp.int32

D_MODEL = 2048
DEPTH = 2
GRID_W = 64
ROPE_BASE = 10000.0
LN_EPS = 1e-6
DEEPNORM_ALPHA = (2 * DEPTH) ** 0.25
RET_HEADS = 4
RET_DIM = 256
S5_CHANNELS = D_MODEL // 2
S5_GROUP = 16
S5_GROUPS = S5_CHANNELS // S5_GROUP
S5_STATE = 64
HEAD_DIM = 128
GQA_HEADS = 8
GQA_KV_HEADS = 2
GQA_GROUP = GQA_HEADS // GQA_KV_HEADS
DIFF_HEADS = 4
DIFF_V_DIM = 2 * HEAD_DIM
N_EXPERTS = 64
TOP_K = 8
N_EXPERT_GROUPS = 8
TOPK_GROUPS = 4
EXPERT_HIDDEN = 512
ROUTED_SCALE = 2.5
LOG2E = 1.4426950408889634

LANES = 128
VMEM_LIMIT = 56 * 1024 * 1024

RET_CHUNK = 256
S5_STEP = 16
MOE_TM = 256
ROW_TILE = 256
FLASH_ROWS = 64
FLASH_BLOCK = 1024
FLASH_SUB = 256


def _cparams(n_axes):
    return pltpu.CompilerParams(dimension_semantics=("arbitrary",) * n_axes, vmem_limit_bytes=VMEM_LIMIT)


def _dot(a, b):
    return jnp.dot(a, b, preferred_element_type=F32)


def _dot_nt(a, b):
    return lax.dot_general(a, b, (((1,), (1,)), ((), ())), preferred_element_type=F32)


def _dot_tn(a, b):
    return lax.dot_general(a, b, (((0,), (0,)), ((), ())), preferred_element_type=F32)


def _split(a):
    hi = a.astype(BF16)
    lo = (a - hi.astype(F32)).astype(BF16)
    return hi, lo


def _dot3(a, b):
    ah, al = _split(a)
    bh, bl = _split(b)
    return _dot(ah, bh) + _dot(al, bh) + _dot(ah, bl)


def _sigmoid(x):
    return 1.0 / (1.0 + jnp.exp(-x))


def _silu(x):
    return x * _sigmoid(x)


def _gelu_tanh(x):
    return 0.5 * x * (1.0 + jnp.tanh(math.sqrt(2.0 / math.pi) * (x + 0.044715 * (x * x * x))))


def _layer_norm(h, g, b):
    mu = jnp.mean(h, axis=-1, keepdims=True)
    d = h - mu
    var = jnp.mean(d * d, axis=-1, keepdims=True)
    return d * lax.rsqrt(var + LN_EPS) * g + b


def _pack_bf16_pairs(x):
    n = x.shape[1] // 2
    lo = lax.bitcast_convert_type(x[:, :n].astype(BF16).astype(F32), U32)
    hi = lax.bitcast_convert_type(x[:, n:].astype(BF16).astype(F32), U32)
    return (lo >> 16) | (hi & jnp.uint32(0xFFFF0000))


def _unpack_bf16_pairs(w):
    lo = lax.bitcast_convert_type(w << 16, F32)
    hi = lax.bitcast_convert_type(w & jnp.uint32(0xFFFF0000), F32)
    return lo, hi


def _modulate_rows(x, mod_ref, shift_row, scale_row, row0, n_lat):
    rows = row0 + lax.broadcasted_iota(I32, (x.shape[0], 1), 0)
    is_ctx = rows >= n_lat
    sc = jnp.where(is_ctx, mod_ref[1, scale_row:scale_row + 1, :], mod_ref[0, scale_row:scale_row + 1, :])
    sh = jnp.where(is_ctx, mod_ref[1, shift_row:shift_row + 1, :], mod_ref[0, shift_row:shift_row + 1, :])
    return x * (1.0 + sc) + sh


def _select_rows(mod_ref, row, row0, n_rows, n_lat):
    rows = row0 + lax.broadcasted_iota(I32, (n_rows, 1), 0)
    return jnp.where(rows >= n_lat, mod_ref[1, row:row + 1, :], mod_ref[0, row:row + 1, :])


def _mod_kernel(v_ref, w_ref, b_ref, o_ref):
    v = v_ref[...]
    o_ref[0] = _dot3(_silu(v), w_ref[0]) + b_ref[0]


def _modulation(c, c_ctx, mod_w, mod_b):
    depth, d, n = mod_w.shape
    v = jnp.concatenate([c[:1], c_ctx[None], jnp.zeros((6, d), F32)], axis=0)
    tn = 512
    out = pl.pallas_call(
        _mod_kernel,
        grid=(depth, n // tn),
        in_specs=[
            pl.BlockSpec((8, d), lambda i, j: (0, 0)),
            pl.BlockSpec((1, d, tn), lambda i, j: (i, 0, j)),
            pl.BlockSpec((1, 1, tn), lambda i, j: (i, 0, j)),
        ],
        out_specs=pl.BlockSpec((1, 8, tn), lambda i, j: (i, 0, j)),
        out_shape=jax.ShapeDtypeStruct((depth, 8, n), F32),
        compiler_params=_cparams(2),
        name="modulation",
    )(v, mod_w, mod_b.reshape(depth, 1, n))
    return out[:, :2].reshape(depth, 2, 6, d)


def _rope_tables(n_lat, n_ctx, head_dim):
    q = head_dim // 4
    t = jnp.arange(n_lat, dtype=I32)
    row = (t // GRID_W).astype(F32)
    col = (t % GRID_W).astype(F32)
    freqs = ROPE_BASE ** (-jnp.arange(q, dtype=F32) / q)
    ar = row[:, None] * freqs[None, :]
    ac = col[:, None] * freqs[None, :]
    cos = jnp.concatenate([jnp.cos(ar), jnp.cos(ar), jnp.cos(ac), jnp.cos(ac)], axis=-1)
    sin = jnp.concatenate([-jnp.sin(ar), jnp.sin(ar), -jnp.sin(ac), jnp.sin(ac)], axis=-1)
    cos = jnp.concatenate([cos, jnp.ones((n_ctx, head_dim), F32)], axis=0)
    sin = jnp.concatenate([sin, jnp.zeros((n_ctx, head_dim), F32)], axis=0)
    return cos, sin


def _proj0_kernel(x_ref, mod_ref, w_ref, cos_ref, sin_ref, o_ref, xm_ref, *, tm, n_lat):
    i = pl.program_id(0)
    j = pl.program_id(1)

    @pl.when(j == 0)
    def _():
        xm_ref[...] = _modulate_rows(x_ref[...], mod_ref, 0, 1, i * tm, n_lat).astype(BF16)

    acc = _dot(xm_ref[...], w_ref[...])

    @pl.when(j >= 2)
    def _():
        o_ref[...] = acc.astype(BF16)

    @pl.when(j < 2)
    def _():
        scale = jnp.where(j == 0, RET_DIM ** -0.5, 1.0).astype(F32)
        for c in range(acc.shape[1] // LANES):
            xs = acc[:, c * LANES:(c + 1) * LANES]
            t0 = (c % 2) * LANES
            rot = xs * cos_ref[:, t0:t0 + LANES] + pltpu.roll(xs, LANES // 2, 1) * sin_ref[:, t0:t0 + LANES]
            o_ref[:, c * LANES:(c + 1) * LANES] = (rot * scale).astype(BF16)


def _proj0(x_all, mod, w_bf, cos, sin, n_lat):
    m, d = x_all.shape
    n = w_bf.shape[1]
    tm = 640 if m % 640 == 0 else 256
    tn = 1024
    return pl.pallas_call(
        functools.partial(_proj0_kernel, tm=tm, n_lat=n_lat),
        grid=(m // tm, n // tn),
        in_specs=[
            pl.BlockSpec((tm, d), lambda i, j: (i, 0)),
            pl.BlockSpec((2, 6, d), lambda i, j: (0, 0, 0)),
            pl.BlockSpec((d, tn), lambda i, j: (0, j)),
            pl.BlockSpec((tm, RET_DIM), lambda i, j: (i, 0)),
            pl.BlockSpec((tm, RET_DIM), lambda i, j: (i, 0)),
        ],
        out_specs=pl.BlockSpec((tm, tn), lambda i, j: (i, j)),
        out_shape=jax.ShapeDtypeStruct((m, n), BF16),
        scratch_shapes=[pltpu.VMEM((tm, d), BF16)],
        compiler_params=_cparams(2),
        name="proj0",
    )(x_all, mod, w_bf, cos, sin)


def _ret_tables(log_decay):
    c = RET_CHUNK
    lg = log_decay.astype(F32)
    idx = jnp.arange(c, dtype=F32)
    diff = idx[:, None] - idx[None, :]
    lower = jnp.where(diff >= 0, jnp.exp(jnp.maximum(diff, 0.0)[None] * lg[0][:, None, None]), 0.0)
    upper = jnp.where(diff <= 0, jnp.exp(jnp.maximum(-diff, 0.0)[None] * lg[1][:, None, None]), 0.0)
    mask = lower + upper
    ones = jnp.ones((1, 1, RET_DIM), F32)
    kdec_f = jnp.exp((c - 1 - idx)[None, :] * lg[0][:, None])[:, :, None] * ones
    qdec_f = jnp.exp((idx + 1)[None, :] * lg[0][:, None])[:, :, None] * ones
    kdec_b = jnp.exp(idx[None, :] * lg[1][:, None])[:, :, None] * ones
    qdec_b = jnp.exp((c - idx)[None, :] * lg[1][:, None])[:, :, None] * ones
    g_chunk = jnp.exp(c * lg)
    return mask, kdec_f, qdec_f, kdec_b, qdec_b, g_chunk


def _ret_bwd_state_kernel(gc_ref, k_ref, v_ref, kdec_ref, sb_ref, s_ref):
    h = pl.program_id(0)
    s = pl.program_id(1)

    @pl.when(s == 0)
    def _():
        s_ref[...] = jnp.zeros_like(s_ref)

    sb_ref[0, 0] = s_ref[...].astype(BF16)
    kd = (k_ref[...].astype(F32) * kdec_ref[0]).astype(BF16)
    s_ref[...] = gc_ref[1, h] * s_ref[...] + _dot_tn(kd, v_ref[...])


def _ret_out_kernel(gc_ref, q_ref, k_ref, v_ref, sb_ref, mask_ref, qdf_ref, qdb_ref, kdf_ref, o_ref, s_ref):
    h = pl.program_id(0)
    s = pl.program_id(1)

    @pl.when(s == 0)
    def _():
        s_ref[...] = jnp.zeros_like(s_ref)

    q = q_ref[...]
    k = k_ref[...]
    v = v_ref[...]
    qf = q.astype(F32)
    w = (_dot_nt(q, k) * mask_ref[0]).astype(BF16)
    o = _dot(w, v)
    o = o + _dot((qf * qdf_ref[0]).astype(BF16), s_ref[...].astype(BF16))
    o = o + _dot((qf * qdb_ref[0]).astype(BF16), sb_ref[0, 0])
    mu = jnp.mean(o, axis=-1, keepdims=True)
    d = o - mu
    var = jnp.mean(d * d, axis=-1, keepdims=True)
    o_ref[...] = (d * lax.rsqrt(var + LN_EPS)).astype(BF16)
    kd = (k.astype(F32) * kdf_ref[0]).astype(BF16)
    s_ref[...] = gc_ref[0, h] * s_ref[...] + _dot_tn(kd, v)


def _retention(p0, log_decay, n_lat, n_ctx):
    m = p0.shape[0]
    c = RET_CHUNK
    nc = m // c
    nlc = n_lat // c
    hh = RET_HEADS
    mask, kdec_f, qdec_f, kdec_b, qdec_b, g_chunk = _ret_tables(log_decay)
    smem = pl.BlockSpec(memory_space=pltpu.SMEM)

    def bchunk(s):
        return nc - 1 - s

    sb = pl.pallas_call(
        _ret_bwd_state_kernel,
        grid=(hh, nc),
        in_specs=[
            smem,
            pl.BlockSpec((c, RET_DIM), lambda h, s: (bchunk(s), hh + h)),
            pl.BlockSpec((c, RET_DIM), lambda h, s: (bchunk(s), 2 * hh + h)),
            pl.BlockSpec((1, c, RET_DIM), lambda h, s: (h, 0, 0)),
        ],
        out_specs=pl.BlockSpec((1, 1, RET_DIM, RET_DIM), lambda h, s: (h, bchunk(s), 0, 0)),
        out_shape=jax.ShapeDtypeStruct((hh, nc, RET_DIM, RET_DIM), BF16),
        scratch_shapes=[pltpu.VMEM((RET_DIM, RET_DIM), F32)],
        compiler_params=_cparams(2),
        name="ret_bwd_state",
    )(g_chunk, p0, p0, kdec_b)

    def fchunk(s):
        return (s + nlc) % nc

    return pl.pallas_call(
        _ret_out_kernel,
        grid=(hh, nc),
        in_specs=[
            smem,
            pl.BlockSpec((c, RET_DIM), lambda h, s: (fchunk(s), h)),
            pl.BlockSpec((c, RET_DIM), lambda h, s: (fchunk(s), hh + h)),
            pl.BlockSpec((c, RET_DIM), lambda h, s: (fchunk(s), 2 * hh + h)),
            pl.BlockSpec((1, 1, RET_DIM, RET_DIM), lambda h, s: (h, fchunk(s), 0, 0)),
            pl.BlockSpec((1, c, c), lambda h, s: (h, 0, 0)),
            pl.BlockSpec((1, c, RET_DIM), lambda h, s: (h, 0, 0)),
            pl.BlockSpec((1, c, RET_DIM), lambda h, s: (h, 0, 0)),
            pl.BlockSpec((1, c, RET_DIM), lambda h, s: (h, 0, 0)),
        ],
        out_specs=pl.BlockSpec((c, RET_DIM), lambda h, s: (fchunk(s), h)),
        out_shape=jax.ShapeDtypeStruct((m, hh * RET_DIM), BF16),
        scratch_shapes=[pltpu.VMEM((RET_DIM, RET_DIM), F32)],
        compiler_params=_cparams(2),
        name="ret_out",
    )(g_chunk, p0, p0, p0, sb, mask, qdec_f, qdec_b, kdec_f)


def _s5_tables(a_re, a_im, log_dt, b_re, b_im, c_re, c_im, d_skip, n_scan):
    f32 = F32
    st = S5_STEP
    g_, p_, s_ = S5_GROUPS, S5_STATE, S5_GROUP
    b_re, b_im, c_re, c_im = (t.astype(f32) for t in (b_re, b_im, c_re, c_im))
    ws, vs, aps, ks = [], [], [], []
    for direction in range(2):
        are, aim = a_re[direction].astype(f32), a_im[direction].astype(f32)
        dt = jnp.exp(log_dt[direction].astype(f32))[:, None]
        zr, zi = are * dt, aim * dt
        mag = jnp.exp(zr)
        ab_re, ab_im = mag * jnp.cos(zi), mag * jnp.sin(zi)
        den = jnp.square(are) + jnp.square(aim)
        nr, ni = ab_re - 1.0, ab_im
        f_re = (nr * are + ni * aim) / den
        f_im = (ni * are - nr * aim) / den
        bb_re = f_re[..., None] * b_re - f_im[..., None] * b_im
        bb_im = f_re[..., None] * b_im + f_im[..., None] * b_re
        pr, pi = [jnp.ones_like(ab_re)], [jnp.zeros_like(ab_im)]
        for _ in range(st):
            pr.append(pr[-1] * ab_re - pi[-1] * ab_im)
            pi.append(pr[-2] * ab_im + pi[-1] * ab_re)
        pw_re, pw_im = jnp.stack(pr), jnp.stack(pi)
        ca_re = c_re[None] * pw_re[:, :, None, :] - c_im[None] * pw_im[:, :, None, :]
        ca_im = c_re[None] * pw_im[:, :, None, :] + c_im[None] * pw_re[:, :, None, :]
        kk = (jnp.einsum('tgip,gpj->tgij', ca_re[:st], bb_re, precision='highest')
              - jnp.einsum('tgip,gpj->tgij', ca_im[:st], bb_im, precision='highest'))
        ks.append(kk)
        e = (st - 1 - jnp.arange(st)) if direction == 0 else jnp.arange(st)
        w_re = pw_re[e][:, :, :, None] * bb_re[None] - pw_im[e][:, :, :, None] * bb_im[None]
        w_im = pw_re[e][:, :, :, None] * bb_im[None] + pw_im[e][:, :, :, None] * bb_re[None]
        w = jnp.concatenate([w_re, w_im], axis=2)
        ws.append(jnp.transpose(w, (1, 0, 3, 2)).reshape(g_, st * s_, 2 * p_))
        e2 = (jnp.arange(st) + 1) if direction == 0 else (st - jnp.arange(st))
        v = jnp.concatenate([ca_re[e2], -ca_im[e2]], axis=3)
        vs.append(jnp.transpose(v, (1, 3, 0, 2)).reshape(g_, 2 * p_, st * s_))
        qr, qi = pw_re[st], pw_im[st]
        rows = []
        for _ in range(n_scan):
            rows.append(jnp.stack([jnp.concatenate([qr, qr], -1), jnp.concatenate([-qi, qi], -1)], axis=1))
            qr, qi = qr * qr - qi * qi, 2.0 * qr * qi
        aps.append(jnp.stack(rows, axis=1))
    tt = jnp.arange(st)
    lag = tt[None, :] - tt[:, None]
    kf = ks[0][jnp.clip(lag, 0, st - 1)]
    kb = ks[1][jnp.clip(-lag, 0, st - 1)]
    tm = jnp.where((lag >= 0)[:, :, None, None, None], kf, 0.0) + jnp.where((lag <= 0)[:, :, None, None, None], kb, 0.0)
    tm = jnp.transpose(tm, (2, 0, 4, 1, 3)).reshape(g_, st * s_, st * s_)
    dsk = d_skip.astype(f32).reshape(g_, s_)
    tm = tm + jnp.eye(st * s_, dtype=f32)[None] * jnp.tile(dsk, (1, st))[:, None, :]
    return tm, jnp.stack(ws), jnp.stack(vs), jnp.stack(aps)


def _s5_kernel(u_ref, t_ref, w_ref, v_ref, a_ref, y_ref, uf_ref, ut_ref, yg_ref, ha_ref, hb_ref,
               *, ncs, nls, n_scan):
    st = S5_STEP
    nt = nls + ncs
    groups = LANES // S5_GROUP
    tw = st * S5_GROUP
    shift = S5_GROUP.bit_length() - 1
    lane_shift = LANES.bit_length() - 1
    uf_ref[...] = u_ref[...].astype(F32)
    for t in range(st):
        ut_ref[t // 2, :, (t % 2) * LANES:(t % 2 + 1) * LANES] = uf_ref[pl.ds(t, nt, stride=st), :].astype(BF16)
    src = lax.broadcasted_iota(I32, (2 * LANES, tw), 0)
    pos = lax.broadcasted_iota(I32, (2 * LANES, tw), 1)
    for g in range(groups):
        base = ((src & (LANES - 1)) == g * S5_GROUP + (pos & (S5_GROUP - 1)))
        ug = jnp.zeros((nt, tw), F32)
        for p in range(st // 2):
            sel = jnp.where(base & ((pos >> shift) == 2 * p + (src >> lane_shift)), 1.0, 0.0).astype(BF16)
            ug = ug + _dot(ut_ref[p], sel)
        ug = ug.astype(BF16)
        useq = jnp.concatenate([ug[nls:nt], ug[0:nls], ug[nls:nt]], axis=0)
        yg_ref[g] = _s5_group(useq, t_ref, w_ref, v_ref, a_ref, ha_ref, hb_ref, g, ncs, nls, n_scan).astype(BF16)
    dst = lax.broadcasted_iota(I32, (tw, 2 * LANES), 1)
    pos_t = lax.broadcasted_iota(I32, (tw, 2 * LANES), 0)
    for p in range(st // 2):
        rows2 = jnp.zeros((nt, 2 * LANES), F32)
        for g in range(groups):
            hit = (((dst & (LANES - 1)) == g * S5_GROUP + (pos_t & (S5_GROUP - 1)))
                   & ((pos_t >> shift) == 2 * p + (dst >> lane_shift)))
            rows2 = rows2 + _dot(yg_ref[g], jnp.where(hit, 1.0, 0.0).astype(BF16))
        uf_ref[pl.ds(2 * p, nt, stride=st), :] = rows2[:, 0:LANES]
        uf_ref[pl.ds(2 * p + 1, nt, stride=st), :] = rows2[:, LANES:2 * LANES]
    y_ref[...] = uf_ref[...].astype(BF16)


def _s5_group(u, t_ref, w_ref, v_ref, a_ref, ha_ref, hb_ref, g, ncs, nls, n_scan):
    n = ncs + nls + ncs

    def mm(x, mat):
        mh, ml = _split(mat)
        return _dot(x, mh) + _dot(x, ml)

    def mm3(x, mat):
        xh, xl = _split(x)
        mh, ml = _split(mat)
        return _dot(xh, mh) + _dot(xl, mh) + _dot(xh, ml)

    rows = lax.broadcasted_iota(I32, (n, 1), 0)

    def scan(z, direction):
        bufs = (ha_ref, hb_ref)
        bufs[0][...] = z
        for kk in range(n_scan):
            src, dst = bufs[kk % 2], bufs[(kk + 1) % 2]
            sft = 1 << kk
            a1 = a_ref[direction, g, kk, 0:1, :]
            a2 = a_ref[direction, g, kk, 1:2, :]
            if sft >= n:
                dst[...] = src[...]
                continue
            if sft % 8 == 0:
                if direction == 0:
                    prev = src[0:n - sft, :]
                    dst[0:sft, :] = src[0:sft, :]
                    dst[sft:n, :] = src[sft:n, :] + a1 * prev + a2 * pltpu.roll(prev, S5_STATE, 1)
                else:
                    nxt = src[sft:n, :]
                    dst[n - sft:n, :] = src[n - sft:n, :]
                    dst[0:n - sft, :] = src[0:n - sft, :] + a1 * nxt + a2 * pltpu.roll(nxt, S5_STATE, 1)
            else:
                cur = src[...]
                if direction == 0:
                    sh = jnp.where(rows >= sft, pltpu.roll(cur, sft, 0), 0.0)
                else:
                    sh = jnp.where(rows < n - sft, pltpu.roll(cur, n - sft, 0), 0.0)
                dst[...] = cur + a1 * sh + a2 * pltpu.roll(sh, S5_STATE, 1)
        return bufs[n_scan % 2][...]

    y = mm(u, t_ref[g])
    hf = scan(mm(u, w_ref[0, g]), 0)
    hf_prev = jnp.where(rows >= 1, pltpu.roll(hf, 1, 0), 0.0)
    yf = mm3(hf_prev, v_ref[0, g])
    hb = scan(mm(u, w_ref[1, g]), 1)
    hb_next = jnp.where(rows < n - 1, pltpu.roll(hb, n - 1, 0), 0.0)
    yb = mm3(hb_next, v_ref[1, g])
    return jnp.concatenate([(y + yf + yb)[ncs:ncs + nls, :], (y + yf)[0:ncs, :] + yb[ncs + nls:n, :]], axis=0)


def _s5(p0, col0, tabs, n_lat, n_ctx):
    m = p0.shape[0]
    st = S5_STEP
    nls, ncs = n_lat // st, n_ctx // st
    n = ncs + nls + ncs
    nt = nls + ncs
    n_scan = max(1, (n - 1).bit_length())
    t_mat, w_mat, v_mat, apow = tabs(n_scan)
    tw = st * S5_GROUP
    gpl = LANES // S5_GROUP
    blk0 = col0 // LANES
    return pl.pallas_call(
        functools.partial(_s5_kernel, ncs=ncs, nls=nls, n_scan=n_scan),
        grid=(S5_CHANNELS // LANES,),
        in_specs=[
            pl.BlockSpec((m, LANES), lambda j: (0, blk0 + j)),
            pl.BlockSpec((gpl, tw, tw), lambda j: (j, 0, 0)),
            pl.BlockSpec((2, gpl, tw, 2 * S5_STATE), lambda j: (0, j, 0, 0)),
            pl.BlockSpec((2, gpl, 2 * S5_STATE, tw), lambda j: (0, j, 0, 0)),
            pl.BlockSpec((2, gpl, n_scan, 2, 2 * S5_STATE), lambda j: (0, j, 0, 0, 0)),
        ],
        out_specs=pl.BlockSpec((m, LANES), lambda j: (0, j)),
        out_shape=jax.ShapeDtypeStruct((m, S5_CHANNELS), BF16),
        scratch_shapes=[pltpu.VMEM((m, LANES), F32), pltpu.VMEM((st // 2, nt, 2 * LANES), BF16),
                        pltpu.VMEM((gpl, nt, tw), BF16),
                        pltpu.VMEM((n, 2 * S5_STATE), F32), pltpu.VMEM((n, 2 * S5_STATE), F32)],
        compiler_params=_cparams(1),
        name="s5",
    )(p0, t_mat, w_mat, v_mat, apow)


def _post_mix(y, x_ref, mod_ref, ln_ref, rw_ref, xo_ref, xp_ref, lg_ref, row0, n_lat):
    tm = y.shape[0]
    g1 = _select_rows(mod_ref, 2, row0, tm, n_lat)
    xn = _layer_norm(DEEPNORM_ALPHA * x_ref[...] + g1 * y, ln_ref[0:1, :], ln_ref[1:2, :])
    xo_ref[...] = xn
    sc2 = _select_rows(mod_ref, 4, row0, tm, n_lat)
    sh2 = _select_rows(mod_ref, 3, row0, tm, n_lat)
    x2 = xn * (1.0 + sc2) + sh2
    xp_ref[...] = _pack_bf16_pairs(x2)
    xh, xl = _split(x2)
    rh, rl = _split(rw_ref[...])
    lg_ref[...] = _dot_nt(rh, xh) + _dot_nt(rh, xl) + _dot_nt(rl, xh)


def _finish0_kernel(r_ref, g_ref, s_ref, x_ref, mod_ref, ln_ref, wglu_ref, bglu_ref, wout_ref, rw_ref,
                    xo_ref, xp_ref, lg_ref, *, tm, n_lat):
    row0 = pl.program_id(0) * tm
    ret = (r_ref[...].astype(F32) * _silu(g_ref[...].astype(F32))).astype(BF16)
    z = _gelu_tanh(s_ref[...].astype(F32))
    zb = z.astype(BF16)
    gate = _sigmoid(_dot(zb, wglu_ref[...]) + bglu_ref[...])
    s5o = (z * gate).astype(BF16)
    half = ret.shape[1]
    y = _dot(ret, wout_ref[0:half, :]) + _dot(s5o, wout_ref[half:, :])
    _post_mix(y, x_ref, mod_ref, ln_ref, rw_ref, xo_ref, xp_ref, lg_ref, row0, n_lat)


def _const_spec(shape):
    nd = len(shape)
    return pl.BlockSpec(shape, lambda i: (0,) * nd)


def _finish0(rn, p0, s5y, x_all, mod, ln, wglu, bglu, wout, rw, n_lat):
    m, d = x_all.shape
    tm = ROW_TILE
    half = d // 2
    outs = pl.pallas_call(
        functools.partial(_finish0_kernel, tm=tm, n_lat=n_lat),
        grid=(m // tm,),
        in_specs=[
            pl.BlockSpec((tm, half), lambda i: (i, 0)),
            pl.BlockSpec((tm, half), lambda i: (i, 3)),
            pl.BlockSpec((tm, half), lambda i: (i, 0)),
            pl.BlockSpec((tm, d), lambda i: (i, 0)),
            _const_spec((2, 6, d)),
            _const_spec((2, d)),
            _const_spec((half, half)),
            _const_spec((1, half)),
            _const_spec((d, d)),
            _const_spec((N_EXPERTS, d)),
        ],
        out_specs=[
            pl.BlockSpec((tm, d), lambda i: (i, 0)),
            pl.BlockSpec((tm, half), lambda i: (i, 0)),
            pl.BlockSpec((N_EXPERTS, tm), lambda i: (0, i)),
        ],
        out_shape=[
            jax.ShapeDtypeStruct((m, d), F32),
            jax.ShapeDtypeStruct((m, half), U32),
            jax.ShapeDtypeStruct((N_EXPERTS, m), F32),
        ],
        compiler_params=_cparams(1),
        name="finish0",
    )(rn, p0, s5y, x_all, mod, ln, wglu, bglu, wout, rw)
    return outs


def _finish1_kernel(og_ref, od_ref, x_ref, mod_ref, ln_ref, wout_ref, rw_ref, xo_ref, xp_ref, lg_ref, *, tm, n_lat):
    row0 = pl.program_id(0) * tm
    half = og_ref.shape[1]
    y = _dot(og_ref[...], wout_ref[0:half, :]) + _dot(od_ref[...], wout_ref[half:, :])
    _post_mix(y, x_ref, mod_ref, ln_ref, rw_ref, xo_ref, xp_ref, lg_ref, row0, n_lat)


def _finish1(og, od, x_all, mod, ln, wout, rw, n_lat):
    d = x_all.shape[1]
    m = og.shape[0]
    tm = ROW_TILE
    half = d // 2
    return pl.pallas_call(
        functools.partial(_finish1_kernel, tm=tm, n_lat=n_lat),
        grid=(m // tm,),
        in_specs=[
            pl.BlockSpec((tm, half), lambda i: (i, 0)),
            pl.BlockSpec((tm, half), lambda i: (i, 0)),
            pl.BlockSpec((tm, d), lambda i: (i, 0)),
            _const_spec((2, 6, d)),
            _const_spec((2, d)),
            _const_spec((d, d)),
            _const_spec((N_EXPERTS, d)),
        ],
        out_specs=[
            pl.BlockSpec((tm, d), lambda i: (i, 0)),
            pl.BlockSpec((tm, half), lambda i: (i, 0)),
            pl.BlockSpec((N_EXPERTS, tm), lambda i: (0, i)),
        ],
        out_shape=[
            jax.ShapeDtypeStruct((m, d), F32),
            jax.ShapeDtypeStruct((m, half), U32),
            jax.ShapeDtypeStruct((N_EXPERTS, m), F32),
        ],
        compiler_params=_cparams(1),
        name="finish1",
    )(og, od, x_all, mod, ln, wout, rw)


def _route_kernel(lg_ref, bias_ref, e_ref, r_ref, w_ref, cnt_ref, carry_ref, *, tb):
    ne = N_EXPERTS
    gsz = ne // N_EXPERT_GROUPS
    neg = -jnp.inf

    @pl.when(pl.program_id(0) == 0)
    def _():
        carry_ref[...] = jnp.zeros_like(carry_ref)

    s = _sigmoid(lg_ref[...])
    sel = s + bias_ref[...]
    gs = []
    for g in range(N_EXPERT_GROUPS):
        blk = sel[g * gsz:(g + 1) * gsz, :]
        m1 = jnp.max(blk, axis=0, keepdims=True)
        n_eq = jnp.sum(jnp.where(blk == m1, 1.0, 0.0), axis=0, keepdims=True)
        m2 = jnp.max(jnp.where(blk < m1, blk, neg), axis=0, keepdims=True)
        gs.append(m1 + jnp.where(n_eq >= 2.0, m1, m2))
    masked = []
    for g in range(N_EXPERT_GROUPS):
        ahead = jnp.zeros_like(gs[g])
        for h in range(N_EXPERT_GROUPS):
            if h == g:
                continue
            beats = ((gs[h] > gs[g]) | (gs[h] == gs[g])) if h < g else (gs[h] > gs[g])
            ahead = ahead + jnp.where(beats, 1.0, 0.0)
        masked.append(jnp.where(ahead < float(TOPK_GROUPS), sel[g * gsz:(g + 1) * gsz, :], neg))
    selm = jnp.concatenate(masked, axis=0)
    eid = lax.broadcasted_iota(I32, (ne, 1), 0)
    ahead = jnp.zeros_like(selm)
    for e in range(ne):
        row = selm[e:e + 1, :]
        beats = (row > selm) | ((row == selm) & (eid > e))
        ahead = ahead + jnp.where(beats, 1.0, 0.0)
    chosen = ahead < float(TOP_K)
    member = jnp.where(chosen, 1.0, 0.0)
    ssel = jnp.where(chosen, s, 0.0)
    wd = ssel / jnp.sum(ssel, axis=0, keepdims=True) * ROUTED_SCALE
    mb = member.astype(BF16)
    ti = lax.broadcasted_iota(I32, (tb, tb), 0)
    tj = lax.broadcasted_iota(I32, (tb, tb), 1)
    tri = jnp.where(ti < tj, 1.0, 0.0).astype(BF16)
    rank = carry_ref[:, 0:1] + _dot(mb, tri)
    carry_ref[...] = carry_ref[...] + jnp.sum(member, axis=1, keepdims=True)
    cnt_ref[...] = carry_ref[...]
    ei = lax.broadcasted_iota(I32, (ne, ne), 0)
    ej = lax.broadcasted_iota(I32, (ne, ne), 1)
    low = jnp.where(ej < ei, 1.0, 0.0).astype(BF16)
    slot = _dot(low, mb)
    eidf = eid.astype(F32)
    for k in range(TOP_K):
        hit = chosen & (slot == float(k))
        e_ref[k:k + 1, :] = jnp.sum(jnp.where(hit, eidf, 0.0), axis=0, keepdims=True).astype(I32)
        r_ref[k:k + 1, :] = jnp.sum(jnp.where(hit, rank, 0.0), axis=0, keepdims=True).astype(I32)
        w_ref[k:k + 1, :] = jnp.sum(jnp.where(hit, wd, 0.0), axis=0, keepdims=True)


def _route(logits_t, router_bias, tm):
    e, t = logits_t.shape
    tb = 512 if t % 512 == 0 else 256
    eidx, rank, wts, counts = pl.pallas_call(
        functools.partial(_route_kernel, tb=tb),
        grid=(t // tb,),
        in_specs=[
            pl.BlockSpec((e, tb), lambda i: (0, i)),
            pl.BlockSpec((e, 1), lambda i: (0, 0)),
        ],
        out_specs=[
            pl.BlockSpec((TOP_K, tb), lambda i: (0, i)),
            pl.BlockSpec((TOP_K, tb), lambda i: (0, i)),
            pl.BlockSpec((TOP_K, tb), lambda i: (0, i)),
            pl.BlockSpec((e, LANES), lambda i: (0, 0)),
        ],
        out_shape=[
            jax.ShapeDtypeStruct((TOP_K, t), I32),
            jax.ShapeDtypeStruct((TOP_K, t), I32),
            jax.ShapeDtypeStruct((TOP_K, t), F32),
            jax.ShapeDtypeStruct((e, LANES), F32),
        ],
        scratch_shapes=[pltpu.VMEM((e, LANES), F32)],
        compiler_params=_cparams(1),
        name="moe_route",
    )(logits_t, router_bias.astype(F32).reshape(e, 1))
    counts = counts[:, 0].astype(I32)
    padded = (counts + tm - 1) // tm * tm
    pad_end = jnp.cumsum(padded)
    pad_start = pad_end - padded
    ids = jnp.arange(e, dtype=I32)
    start_of = jnp.sum(jnp.where(eidx[:, :, None] == ids, pad_start, 0), axis=-1)
    dest = (start_of + rank).T
    n_blocks = (t * TOP_K + e * (tm - 1)) // tm
    blk_start = jnp.arange(n_blocks, dtype=I32) * tm
    blk_expert = jnp.minimum(jnp.sum((blk_start[:, None] >= pad_end[None, :]).astype(I32), axis=1), e - 1)
    n_used = (pad_end[-1] // tm).astype(I32).reshape(1)
    return dest, wts.T, blk_expert, n_used, n_blocks, pad_end.astype(I32), padded.astype(I32)


def _dispatch_kernel(pad_end_ref, padded_ref, dest_ref, x_ref, xs_ref, zero_ref, sem, zsem, *, tb, tm):
    @pl.when(pl.program_id(0) == 0)
    def _():
        zero_ref[...] = jnp.zeros_like(zero_ref)

        def fill(e):
            first = pl.multiple_of(pad_end_ref[e] - tm, tm)
            return pltpu.make_async_copy(zero_ref, xs_ref.at[pl.ds(first, tm)], zsem)

        for e in range(N_EXPERTS):
            @pl.when(padded_ref[e] > 0)
            def _():
                fill(e).start()
        for e in range(N_EXPERTS):
            @pl.when(padded_ref[e] > 0)
            def _():
                fill(e).wait()

    def row_copy(r, k):
        d = dest_ref[0, 0, r * TOP_K + k]
        return pltpu.make_async_copy(x_ref.at[pl.ds(r, 1)], xs_ref.at[pl.ds(d, 1)], sem)

    def start(r, carry):
        for k in range(TOP_K):
            row_copy(r, k).start(priority=k % 2)
        return carry

    def wait(r, carry):
        for k in range(TOP_K):
            row_copy(r, k).wait()
        return carry

    lax.fori_loop(0, tb, start, 0)
    lax.fori_loop(0, tb, wait, 0)


def _dispatch(xp, dest, pad_end, padded, n_pad, tm):
    t, w = xp.shape
    tb = ROW_TILE
    smem = pl.BlockSpec(memory_space=pltpu.SMEM)
    return pl.pallas_call(
        functools.partial(_dispatch_kernel, tb=tb, tm=tm),
        grid=(t // tb,),
        in_specs=[
            smem,
            smem,
            pl.BlockSpec((1, 1, tb * TOP_K), lambda i: (i, 0, 0), memory_space=pltpu.SMEM),
            pl.BlockSpec((tb, w), lambda i: (i, 0)),
        ],
        out_specs=pl.BlockSpec(memory_space=pl.ANY),
        out_shape=jax.ShapeDtypeStruct((n_pad, w), U32),
        scratch_shapes=[pltpu.VMEM((tm, w), U32), pltpu.SemaphoreType.DMA(()), pltpu.SemaphoreType.DMA(())],
        compiler_params=_cparams(1),
        name="moe_dispatch",
    )(pad_end, padded, dest.reshape(t // tb, 1, tb * TOP_K), xp)


def _expert_kernel(be_ref, nu_ref, xs_ref, wg_ref, wu_ref, wd_ref, y_ref, wgb_ref, wub_ref, wdb_ref):
    b = pl.program_id(0)
    e = be_ref[b]
    prev = be_ref[jnp.maximum(b - 1, 0)]

    @pl.when((b == 0) | (e != prev))
    def _():
        wgb_ref[...] = wg_ref[0, 0].astype(BF16)
        wub_ref[...] = wu_ref[0, 0].astype(BF16)
        wdb_ref[...] = wd_ref[0, 0].astype(BF16)

    @pl.when(b < nu_ref[0])
    def _():
        lo, hi = _unpack_bf16_pairs(xs_ref[...])
        lo = lo.astype(BF16)
        hi = hi.astype(BF16)
        half = lo.shape[1]
        gate = _dot(lo, wgb_ref[0:half, :]) + _dot(hi, wgb_ref[half:, :])
        up = _dot(lo, wub_ref[0:half, :]) + _dot(hi, wub_ref[half:, :])
        hid = (_silu(gate) * up).astype(BF16)
        y_ref[...] = _pack_bf16_pairs(_dot(hid, wdb_ref[...]))

    @pl.when(b >= nu_ref[0])
    def _():
        y_ref[...] = jnp.zeros_like(y_ref)


def _experts(xs, blk_expert, n_used, w_gate, w_up, w_down, layer):
    n_pad, w = xs.shape
    tm = MOE_TM
    d, hdn = w_gate.shape[2], w_gate.shape[3]
    return pl.pallas_call(
        _expert_kernel,
        grid_spec=pltpu.PrefetchScalarGridSpec(
            num_scalar_prefetch=2,
            grid=(n_pad // tm,),
            in_specs=[
                pl.BlockSpec((tm, w), lambda b, be, nu: (b, 0)),
                pl.BlockSpec((1, 1, d, hdn), lambda b, be, nu: (layer, be[b], 0, 0)),
                pl.BlockSpec((1, 1, d, hdn), lambda b, be, nu: (layer, be[b], 0, 0)),
                pl.BlockSpec((1, 1, hdn, d), lambda b, be, nu: (layer, be[b], 0, 0)),
            ],
            out_specs=pl.BlockSpec((tm, w), lambda b, be, nu: (b, 0)),
            scratch_shapes=[pltpu.VMEM((d, hdn), BF16), pltpu.VMEM((d, hdn), BF16), pltpu.VMEM((hdn, d), BF16)],
        ),
        out_shape=jax.ShapeDtypeStruct((n_pad, w), U32),
        compiler_params=_cparams(1),
        name="moe_experts",
    )(blk_expert, n_used, xs, w_gate, w_up, w_down)


def _combine_kernel(dest_ref, xp_ref, ws_ref, x_ref, mod_ref, ln_ref, sg_ref, su_ref, sd_ref, y_hbm,
                    xo_ref, ybuf, sem, *, tb, n_lat):
    row0 = pl.program_id(0) * tb

    def row_copy(r, k):
        d = dest_ref[0, 0, r * TOP_K + k]
        return pltpu.make_async_copy(y_hbm.at[pl.ds(d, 1)], ybuf.at[pl.ds(k * tb + r, 1)], sem)

    def start(r, carry):
        for k in range(TOP_K):
            row_copy(r, k).start(priority=k % 2)
        return carry

    def wait(r, carry):
        for k in range(TOP_K):
            row_copy(r, k).wait()
        return carry

    lax.fori_loop(0, tb, start, 0)
    lo, hi = _unpack_bf16_pairs(xp_ref[...])
    lo = lo.astype(BF16)
    hi = hi.astype(BF16)
    half = lo.shape[1]
    gate = _dot(lo, sg_ref[0:half, :]) + _dot(hi, sg_ref[half:, :])
    up = _dot(lo, su_ref[0:half, :]) + _dot(hi, su_ref[half:, :])
    f = _dot((_silu(gate) * up).astype(BF16), sd_ref[...])
    lax.fori_loop(0, tb, wait, 0)
    acc_lo = jnp.zeros((tb, half), F32)
    acc_hi = jnp.zeros((tb, half), F32)
    for k in range(TOP_K):
        ylo, yhi = _unpack_bf16_pairs(ybuf[k * tb:(k + 1) * tb, :])
        wk = ws_ref[:, k:k + 1]
        acc_lo = acc_lo + wk * ylo
        acc_hi = acc_hi + wk * yhi
    f = f + jnp.concatenate([acc_lo, acc_hi], axis=1)
    g2 = _select_rows(mod_ref, 5, row0, tb, n_lat)
    xo_ref[...] = _layer_norm(DEEPNORM_ALPHA * x_ref[...] + g2 * f, ln_ref[0:1, :], ln_ref[1:2, :])


def _combine(dest, xp, wsel, x_all, mod, ln, sg, su, sd, y, n_lat):
    t, w = xp.shape
    d = x_all.shape[1]
    tb = ROW_TILE
    hdn = sg.shape[1]
    return pl.pallas_call(
        functools.partial(_combine_kernel, tb=tb, n_lat=n_lat),
        grid=(t // tb,),
        in_specs=[
            pl.BlockSpec((1, 1, tb * TOP_K), lambda i: (i, 0, 0), memory_space=pltpu.SMEM),
            pl.BlockSpec((tb, w), lambda i: (i, 0)),
            pl.BlockSpec((tb, TOP_K), lambda i: (i, 0)),
            pl.BlockSpec((tb, d), lambda i: (i, 0)),
            _const_spec((2, 6, d)),
            _const_spec((2, d)),
            _const_spec((d, hdn)),
            _const_spec((d, hdn)),
            _const_spec((hdn, d)),
            pl.BlockSpec(memory_space=pl.ANY),
        ],
        out_specs=pl.BlockSpec((tb, d), lambda i: (i, 0)),
        out_shape=jax.ShapeDtypeStruct((t, d), F32),
        scratch_shapes=[pltpu.VMEM((TOP_K * tb, w), U32), pltpu.SemaphoreType.DMA(())],
        compiler_params=_cparams(1),
        name="moe_combine",
    )(dest.reshape(t // tb, 1, tb * TOP_K), xp, wsel, x_all, mod, ln, sg, su, sd, y)


def _moe(xp, logits, x_res, mod, ln, router_bias, w_gate, w_up, w_down, layer, sg, su, sd, n_lat):
    dest, wsel, blk_expert, n_used, n_blocks, pad_end, padded = _route(logits, router_bias, MOE_TM)
    xs = _dispatch(xp, dest, pad_end, padded, n_blocks * MOE_TM, MOE_TM)
    y = _experts(xs, blk_expert, n_used, w_gate, w_up, w_down, layer)
    return _combine(dest, xp, wsel, x_res, mod, ln, sg.astype(BF16), su.astype(BF16), sd.astype(BF16), y, n_lat)


def _rot_lanes(xs, cos, sin):
    q = HEAD_DIM // 4
    lane = lax.broadcasted_iota(I32, xs.shape, 1)
    partner = jnp.where((lane % (2 * q)) < q, pltpu.roll(xs, HEAD_DIM - q, 1), pltpu.roll(xs, q, 1))
    return xs * cos + partner * sin


def _projq_kernel(x_ref, mod_ref, w_ref, cos_ref, sin_ref, o_ref, xm_ref, *, tm, n_lat):
    i = pl.program_id(0)
    j = pl.program_id(1)

    @pl.when(j == 0)
    def _():
        xm_ref[...] = _modulate_rows(x_ref[...], mod_ref, 0, 1, i * tm, n_lat).astype(BF16)

    acc = _dot(xm_ref[...], w_ref[...])
    n_heads = acc.shape[1] // HEAD_DIM

    @pl.when(j == 0)
    def _():
        for c in range(n_heads):
            xs = acc[:, c * HEAD_DIM:(c + 1) * HEAD_DIM]
            xs = xs * lax.rsqrt(jnp.mean(xs * xs, axis=-1, keepdims=True) + LN_EPS)
            o_ref[:, c * HEAD_DIM:(c + 1) * HEAD_DIM] = _rot_lanes(xs, cos_ref[0], sin_ref[0]).astype(BF16)

    @pl.when(j == 1)
    def _():
        for c in range(n_heads):
            xs = acc[:, c * HEAD_DIM:(c + 1) * HEAD_DIM]
            o_ref[:, c * HEAD_DIM:(c + 1) * HEAD_DIM] = _rot_lanes(xs, cos_ref[1], sin_ref[1]).astype(BF16)


def _projq(x_all, mod, w_bf, cos2, sin2, n_lat):
    d = x_all.shape[1]
    n = w_bf.shape[1]
    tm = 512 if n_lat % 512 == 0 else 256
    tn = n // 2
    return pl.pallas_call(
        functools.partial(_projq_kernel, tm=tm, n_lat=n_lat),
        grid=(n_lat // tm, 2),
        in_specs=[
            pl.BlockSpec((tm, d), lambda i, j: (i, 0)),
            pl.BlockSpec((2, 6, d), lambda i, j: (0, 0, 0)),
            pl.BlockSpec((d, tn), lambda i, j: (0, j)),
            pl.BlockSpec((2, tm, HEAD_DIM), lambda i, j: (0, i, 0)),
            pl.BlockSpec((2, tm, HEAD_DIM), lambda i, j: (0, i, 0)),
        ],
        out_specs=pl.BlockSpec((tm, tn), lambda i, j: (i, j)),
        out_shape=jax.ShapeDtypeStruct((n_lat, n), BF16),
        scratch_shapes=[pltpu.VMEM((tm, d), BF16)],
        compiler_params=_cparams(2),
        name="proj1_q",
    )(x_all, mod, w_bf, cos2, sin2)


def _projkv_kernel(x_ref, mod_ref, wv_ref, wkt_ref, cos_ref, sin_ref, v_ref, kt_ref, *, tm, n_lat):
    i = pl.program_id(0)
    xm = _modulate_rows(x_ref[...], mod_ref, 0, 1, i * tm, n_lat).astype(BF16)
    v_ref[...] = _dot(xm, wv_ref[...]).astype(BF16)
    kt = _dot_nt(wkt_ref[...], xm)
    q = HEAD_DIM // 4
    n_gk = GQA_KV_HEADS * HEAD_DIM
    for c in range(kt.shape[0] // HEAD_DIM):
        xs = kt[c * HEAD_DIM:(c + 1) * HEAD_DIM, :]
        t = 0 if c * HEAD_DIM < n_gk else 1
        if t == 0:
            xs = xs * lax.rsqrt(jnp.mean(xs * xs, axis=0, keepdims=True) + LN_EPS)
        partner = jnp.concatenate([xs[q:2 * q], xs[0:q], xs[3 * q:4 * q], xs[2 * q:3 * q]], axis=0)
        kt_ref[c * HEAD_DIM:(c + 1) * HEAD_DIM, :] = (xs * cos_ref[t] + partner * sin_ref[t]).astype(BF16)


def _projkv(x_all, mod, wv_bf, wkt_bf, cos_t, sin_t, n_lat):
    m, d = x_all.shape
    nv = wv_bf.shape[1]
    nk = wkt_bf.shape[0]
    tm = 640 if m % 640 == 0 else 256
    return pl.pallas_call(
        functools.partial(_projkv_kernel, tm=tm, n_lat=n_lat),
        grid=(m // tm,),
        in_specs=[
            pl.BlockSpec((tm, d), lambda i: (i, 0)),
            _const_spec((2, 6, d)),
            _const_spec((d, nv)),
            _const_spec((nk, d)),
            pl.BlockSpec((2, HEAD_DIM, tm), lambda i: (0, 0, i)),
            pl.BlockSpec((2, HEAD_DIM, tm), lambda i: (0, 0, i)),
        ],
        out_specs=[
            pl.BlockSpec((tm, nv), lambda i: (i, 0)),
            pl.BlockSpec((nk, tm), lambda i: (0, i)),
        ],
        out_shape=[jax.ShapeDtypeStruct((m, nv), BF16), jax.ShapeDtypeStruct((nk, m), BF16)],
        compiler_params=_cparams(1),
        name="proj1_kv",
    )(x_all, mod, wv_bf, wkt_bf, cos_t, sin_t)


def _flash(qs, kt_ref, v_ref, scratch, tk, stack):
    (qs_ref, s_refs, pm_refs, m_ref, l_ref, acc_ref) = scratch
    rows = qs_ref.shape[0]
    lk = v_ref.shape[0]
    n = lk // tk
    rs = rows // stack
    part = rows // len(qs)
    for a, q_part in enumerate(qs):
        qs_ref[a * part:(a + 1) * part, :] = q_part
    m_ref[...] = jnp.full(m_ref.shape, -jnp.inf, F32)
    l_ref[...] = jnp.zeros(l_ref.shape, F32)
    acc_ref[...] = jnp.zeros(acc_ref.shape, F32)

    n_grp = tk // LANES
    dv = acc_ref.shape[1]
    rb = FLASH_BLOCK if rows % FLASH_BLOCK == 0 else rows
    n_rb = rows // rb
    ps = min(rb, rs)

    def score_rows(j, r, s_ref, pm_ref):
        off = pl.multiple_of(j * tk, tk)
        for piece in range(rb // ps):
            r2 = pl.multiple_of(r + piece * ps, ps)
            ka = 0 if stack == 1 else pl.multiple_of((r2 // rs) * HEAD_DIM, HEAD_DIM)
            s = _dot(qs_ref[pl.ds(r2, ps), :], kt_ref[pl.ds(ka, HEAD_DIM), pl.ds(off, tk)])
            s_ref[pl.ds(r2, ps), :] = s
            pm = s[:, 0:LANES]
            for c in range(1, n_grp):
                pm = jnp.maximum(pm, s[:, c * LANES:(c + 1) * LANES])
            pm_ref[pl.ds(r2, ps), :] = pm

    sb = FLASH_SUB if rb % FLASH_SUB == 0 else rb

    def attend_rows(j, r, s_ref, pm_ref):
        off = pl.multiple_of(j * tk, tk)
        blk = pl.ds(r, sb)
        m_prev = m_ref[blk, :]
        m_next = jnp.maximum(m_prev, jnp.max(pm_ref[blk, :], axis=1, keepdims=True))
        alpha = jnp.exp2(m_prev - m_next)
        m_ref[blk, :] = m_next
        pieces = []
        for q in range(sb // FLASH_ROWS):
            sub = pl.ds(pl.multiple_of(r + q * FLASH_ROWS, FLASH_ROWS), FLASH_ROWS)
            mb = m_next[q * FLASH_ROWS:(q + 1) * FLASH_ROWS]
            lsum = alpha[q * FLASH_ROWS:(q + 1) * FLASH_ROWS] * l_ref[sub, :]
            row = []
            for c in range(n_grp):
                pc = jnp.exp2(s_ref[sub, c * LANES:(c + 1) * LANES] - mb)
                lsum = lsum + pc
                row.append(pc.astype(BF16))
            l_ref[sub, :] = lsum
            pieces.append(jnp.concatenate(row, axis=1))
        pv = _dot(jnp.concatenate(pieces, axis=0), v_ref[pl.ds(off, tk), :])
        for c in range(dv // LANES):
            cols = slice(c * LANES, (c + 1) * LANES)
            acc_ref[blk, cols] = alpha * acc_ref[blk, cols] + pv[:, cols]

    def step_parity(j, cur, with_scores):
        nxt = 1 - cur

        def body(i, carry):
            r = pl.multiple_of(i * rb, rb)
            if with_scores:
                score_rows(j + 1, r, s_refs[nxt], pm_refs[nxt])
            for h in range(rb // sb):
                attend_rows(j, pl.multiple_of(r + h * sb, sb), s_refs[cur], pm_refs[cur])
            return carry

        lax.fori_loop(0, n_rb, body, 0)

    def first_scores(i, carry):
        score_rows(0, pl.multiple_of(i * rb, rb), s_refs[0], pm_refs[0])
        return carry

    lax.fori_loop(0, n_rb, first_scores, 0)

    def pair(i, carry):
        j = 2 * i
        step_parity(j, 0, True)
        step_parity(j + 1, 1, True)
        return carry

    lax.fori_loop(0, (n - 1) // 2, pair, 0)
    if (n - 1) % 2:
        step_parity(n - 2, (n - 2) % 2, True)
    step_parity(n - 1, (n - 1) % 2, False)
    return acc_ref[...] / jnp.sum(l_ref[...], axis=1, keepdims=True)


def _flash_scratch(rows, tk, dv):
    stat = pltpu.VMEM((rows, LANES), F32)
    return [pltpu.VMEM((rows, HEAD_DIM), BF16),
            pltpu.VMEM((rows, tk), F32), pltpu.VMEM((rows, tk), F32), stat, stat,
            stat, stat, pltpu.VMEM((rows, dv), F32)]


def _unpack_flash_scratch(refs):
    qs_ref, s0, s1, pm0, pm1, m_ref, l_ref, acc_ref = refs
    return (qs_ref, (s0, s1), (pm0, pm1), m_ref, l_ref, acc_ref)


def _gqa_kernel(q_ref, kt_ref, v_ref, o_ref, *scratch, tk):
    tq = q_ref.shape[0]
    qs = [q_ref[:, a * HEAD_DIM:(a + 1) * HEAD_DIM] for a in range(GQA_GROUP)]
    o = _flash(qs, kt_ref, v_ref, _unpack_flash_scratch(scratch), tk, 1)
    for a in range(GQA_GROUP):
        o_ref[:, a * HEAD_DIM:(a + 1) * HEAD_DIM] = o[a * tq:(a + 1) * tq].astype(BF16)


def _diff_kernel(lam_ref, q_ref, kt_ref, v_ref, g_ref, o_ref, *scratch, tk, out_scale):
    tq = q_ref.shape[0]
    qs = [q_ref[:, 0:HEAD_DIM], q_ref[:, HEAD_DIM:2 * HEAD_DIM]]
    o = _flash(qs, kt_ref, v_ref, _unpack_flash_scratch(scratch), tk, 2)
    od = o[0:tq] - lam_ref[0] * o[tq:2 * tq]
    od = od * lax.rsqrt(jnp.mean(od * od, axis=-1, keepdims=True) + LN_EPS)
    o_ref[...] = (od * g_ref[...] * out_scale).astype(BF16)


def _key_chunk(lk):
    for tk in (1280, 1024, 640, 512, 256, 128):
        if lk % tk == 0 and lk // tk >= 2:
            return tk
    return lk


def _attention(q1, kt, v1, lam, subln_g, lambda_init, n_lat):
    lk = v1.shape[0]
    tk = _key_chunk(lk)
    tq = 256
    gw = GQA_GROUP * HEAD_DIM
    n_gv = GQA_KV_HEADS * HEAD_DIM
    once = pl.Buffered(1)
    og = pl.pallas_call(
        functools.partial(_gqa_kernel, tk=tk),
        grid=(GQA_KV_HEADS, n_lat // tq),
        in_specs=[
            pl.BlockSpec((tq, gw), lambda g, i: (i, g)),
            pl.BlockSpec((HEAD_DIM, lk), lambda g, i: (g, 0), pipeline_mode=once),
            pl.BlockSpec((lk, HEAD_DIM), lambda g, i: (0, g), pipeline_mode=once),
        ],
        out_specs=pl.BlockSpec((tq, gw), lambda g, i: (i, g)),
        out_shape=jax.ShapeDtypeStruct((n_lat, GQA_HEADS * HEAD_DIM), BF16),
        scratch_shapes=_flash_scratch(GQA_GROUP * tq, tk, HEAD_DIM),
        compiler_params=_cparams(2),
        name="gqa_attention",
    )(q1, kt, v1)
    q_off = GQA_HEADS * HEAD_DIM // DIFF_V_DIM
    kv_off = n_gv // DIFF_V_DIM
    tq = 512 if n_lat % 512 == 0 else 256
    od = pl.pallas_call(
        functools.partial(_diff_kernel, tk=tk, out_scale=1.0 - lambda_init),
        grid=(DIFF_HEADS, n_lat // tq),
        in_specs=[
            pl.BlockSpec(memory_space=pltpu.SMEM),
            pl.BlockSpec((tq, DIFF_V_DIM), lambda h, i: (i, q_off + h)),
            pl.BlockSpec((DIFF_V_DIM, lk), lambda h, i: (kv_off + h, 0), pipeline_mode=once),
            pl.BlockSpec((lk, DIFF_V_DIM), lambda h, i: (0, kv_off + h), pipeline_mode=once),
            pl.BlockSpec((1, DIFF_V_DIM), lambda h, i: (0, 0)),
        ],
        out_specs=pl.BlockSpec((tq, DIFF_V_DIM), lambda h, i: (i, h)),
        out_shape=jax.ShapeDtypeStruct((n_lat, DIFF_HEADS * DIFF_V_DIM), BF16),
        scratch_shapes=_flash_scratch(2 * tq, tk, DIFF_V_DIM),
        compiler_params=_cparams(2),
        name="diff_attention",
    )(lam, q1, kt, v1, subln_g.reshape(1, DIFF_V_DIM).astype(F32))
    return og, od


def _router_operand(router_w):
    return router_w.astype(F32).T


def kernel(x, c, ctx, c_ctx, mod_w, mod_b, ln_g, ln_b, ev_w_in, ev_w_out, ret_log_decay, s5_a_re, s5_a_im, s5_log_dt, s5_b_re, s5_b_im, s5_c_re, s5_c_im, s5_d, s5_w_glu, s5_b_glu, od_w_in, od_w_out, qk_norm_g, diff_lambda, diff_subln_g, router_w, router_bias, exp_w_gate, exp_w_up, exp_w_down, sh_w_gate, sh_w_up, sh_w_down):
    b_, n_lat, d = x.shape
    n_ctx = ctx.shape[1]
    assert b_ == 1 and d == D_MODEL and n_lat % RET_CHUNK == 0 and n_ctx % RET_CHUNK == 0
    m_all = n_lat + n_ctx
    x_all = jnp.concatenate([x[0], ctx[0]], axis=0)
    mods = _modulation(c, c_ctx, mod_w, mod_b)

    cos0, sin0 = _rope_tables(n_lat, n_ctx, RET_DIM)
    p0 = _proj0(x_all, mods[0], ev_w_in[0].astype(BF16), cos0, sin0, n_lat)
    rn = _retention(p0, ret_log_decay[0], n_lat, n_ctx)
    s5_tabs = functools.partial(_s5_tables, s5_a_re[0], s5_a_im[0], s5_log_dt[0], s5_b_re[0], s5_b_im[0],
                                s5_c_re[0], s5_c_im[0], s5_d[0])
    s5y = _s5(p0, 4 * RET_HEADS * RET_DIM, s5_tabs, n_lat, n_ctx)
    ln0 = jnp.stack([ln_g[0], ln_b[0]], axis=1)
    x_all, xp, logits = _finish0(rn, p0, s5y, x_all, mods[0], ln0[0], s5_w_glu[0].astype(BF16),
                                 s5_b_glu[0].reshape(1, -1).astype(F32), ev_w_out[0].astype(BF16),
                                 _router_operand(router_w[0]), n_lat)
    x_all = _moe(xp, logits, x_all, mods[0], ln0[1], router_bias[0], exp_w_gate, exp_w_up, exp_w_down, 0,
                 sh_w_gate[0], sh_w_up[0], sh_w_down[0], n_lat)

    i = 1
    lambda_init = 0.8 - 0.6 * math.exp(-0.3 * i)
    gq_w = GQA_HEADS * HEAD_DIM
    gk_w = GQA_KV_HEADS * HEAD_DIM
    dq_w = DIFF_HEADS * 2 * HEAD_DIM
    cuts = [gq_w, gq_w + gk_w, gq_w + 2 * gk_w, gq_w + 2 * gk_w + dq_w, gq_w + 2 * gk_w + 2 * dq_w]
    w_in = od_w_in[0]
    w_gq, w_gk, w_gv, w_dq, w_dk, w_dv = (w_in[:, a:b] for a, b in zip([0] + cuts, cuts + [w_in.shape[1]]))
    cos1, sin1 = _rope_tables(n_lat, n_ctx, HEAD_DIM)
    qscale = HEAD_DIM ** -0.5 * LOG2E
    gq_gain = qk_norm_g[0, 0].astype(F32)
    gk_gain = qk_norm_g[0, 1].astype(F32)
    quarter = HEAD_DIM // 4

    def partner_gain(g):
        return jnp.concatenate([g[quarter:2 * quarter], g[0:quarter], g[3 * quarter:], g[2 * quarter:3 * quarter]])

    cos_q = jnp.stack([cos1[:n_lat] * gq_gain[None, :], cos1[:n_lat]]) * qscale
    sin_q = jnp.stack([sin1[:n_lat] * partner_gain(gq_gain)[None, :], sin1[:n_lat]]) * qscale
    cos_k = jnp.stack([cos1.T * gk_gain[:, None], cos1.T])
    sin_k = jnp.stack([sin1.T * partner_gain(gk_gain)[:, None], sin1.T])
    q1 = _projq(x_all, mods[1], jnp.concatenate([w_gq, w_dq], axis=1).astype(BF16), cos_q, sin_q, n_lat)
    v1, kt = _projkv(x_all, mods[1], jnp.concatenate([w_gv, w_dv], axis=1).astype(BF16),
                     jnp.concatenate([w_gk, w_dk], axis=1).T.astype(BF16), cos_k, sin_k, n_lat)
    lf = diff_lambda[0].astype(F32)
    lam = (jnp.exp(jnp.sum(lf[0] * lf[1])) - jnp.exp(jnp.sum(lf[2] * lf[3])) + lambda_init).reshape(1)
    og, od = _attention(q1, kt, v1, lam, diff_subln_g[0], lambda_init, n_lat)
    ln1 = jnp.stack([ln_g[1], ln_b[1]], axis=1)
    x_lat, xp, logits = _finish1(og, od, x_all, mods[1], ln1[0], od_w_out[0].astype(BF16),
                                 _router_operand(router_w[1]), n_lat)
    x_lat = _moe(xp, logits, x_lat, mods[1], ln1[1], router_bias[1], exp_w_gate, exp_w_up, exp_w_down, 1,
                 sh_w_gate[1], sh_w_up[1], sh_w_down[1], n_lat)
    return x_lat[None]
```

```python
import functools
import math

import jax
import jax.numpy as jnp
from jax import lax
from jax.experimental import pallas as pl
from jax.experimental.pallas import tpu as pltpu

F32 = jnp.float32
BF16 = jnp.bfloat16
U32 = jnp.uint32
I32 = jnp.int32

D_MODEL = 2048
DEPTH = 2
GRID_W = 64
ROPE_BASE = 10000.0
LN_EPS = 1e-6
DEEPNORM_ALPHA = (2 * DEPTH) ** 0.25
RET_HEADS = 4
RET_DIM = 256
S5_CHANNELS = D_MODEL // 2
S5_GROUP = 16
S5_GROUPS = S5_CHANNELS // S5_GROUP
S5_STATE = 64
HEAD_DIM = 128
GQA_HEADS = 8
GQA_KV_HEADS = 2
GQA_GROUP = GQA_HEADS // GQA_KV_HEADS
DIFF_HEADS = 4
DIFF_V_DIM = 2 * HEAD_DIM
N_EXPERTS = 64
TOP_K = 8
N_EXPERT_GROUPS = 8
TOPK_GROUPS = 4
EXPERT_HIDDEN = 512
ROUTED_SCALE = 2.5
LOG2E = 1.4426950408889634

LANES = 128
VMEM_LIMIT = 56 * 1024 * 1024

RET_CHUNK = 256
S5_STEP = 16
MOE_TM = 256
ROW_TILE = 256
COMBINE_CHUNK = 16
COMBINE_ROWS = ROW_TILE * TOP_K + 2 * N_EXPERTS * COMBINE_CHUNK
COMBINE_PIECE = 512
FLASH_ROWS = 64
FLASH_BLOCK = 1024
FLASH_SUB = 256


def _cparams(n_axes):
    return pltpu.CompilerParams(dimension_semantics=("arbitrary",) * n_axes, vmem_limit_bytes=VMEM_LIMIT)


def _dot(a, b):
    return jnp.dot(a, b, preferred_element_type=F32)


def _dot_nt(a, b):
    return lax.dot_general(a, b, (((1,), (1,)), ((), ())), preferred_element_type=F32)


def _dot_tn(a, b):
    return lax.dot_general(a, b, (((0,), (0,)), ((), ())), preferred_element_type=F32)


def _split(a):
    hi = a.astype(BF16)
    lo = (a - hi.astype(F32)).astype(BF16)
    return hi, lo


def _dot3(a, b):
    ah, al = _split(a)
    bh, bl = _split(b)
    return _dot(ah, bh) + _dot(al, bh) + _dot(ah, bl)


def _sigmoid(x):
    return 1.0 / (1.0 + jnp.exp(-x))


def _silu(x):
    return x * _sigmoid(x)


def _gelu_tanh(x):
    return 0.5 * x * (1.0 + jnp.tanh(math.sqrt(2.0 / math.pi) * (x + 0.044715 * (x * x * x))))


def _layer_norm(h, g, b):
    mu = jnp.mean(h, axis=-1, keepdims=True)
    d = h - mu
    var = jnp.mean(d * d, axis=-1, keepdims=True)
    return d * lax.rsqrt(var + LN_EPS) * g + b


def _pack_bf16_pairs(x):
    n = x.shape[1] // 2
    lo = lax.bitcast_convert_type(x[:, :n].astype(BF16).astype(F32), U32)
    hi = lax.bitcast_convert_type(x[:, n:].astype(BF16).astype(F32), U32)
    return (lo >> 16) | (hi & jnp.uint32(0xFFFF0000))


def _unpack_bf16_pairs(w):
    lo = lax.bitcast_convert_type(w << 16, F32)
    hi = lax.bitcast_convert_type(w & jnp.uint32(0xFFFF0000), F32)
    return lo, hi


def _modulate_rows(x, mod_ref, shift_row, scale_row, row0, n_lat):
    rows = row0 + lax.broadcasted_iota(I32, (x.shape[0], 1), 0)
    is_ctx = rows >= n_lat
    sc = jnp.where(is_ctx, mod_ref[1, scale_row:scale_row + 1, :], mod_ref[0, scale_row:scale_row + 1, :])
    sh = jnp.where(is_ctx, mod_ref[1, shift_row:shift_row + 1, :], mod_ref[0, shift_row:shift_row + 1, :])
    return x * (1.0 + sc) + sh


def _select_rows(mod_ref, row, row0, n_rows, n_lat):
    rows = row0 + lax.broadcasted_iota(I32, (n_rows, 1), 0)
    return jnp.where(rows >= n_lat, mod_ref[1, row:row + 1, :], mod_ref[0, row:row + 1, :])


def _mod_kernel(v_ref, w_ref, b_ref, o_ref):
    v = v_ref[...]
    o_ref[0] = _dot3(_silu(v), w_ref[0]) + b_ref[0]


def _modulation(c, c_ctx, mod_w, mod_b):
    depth, d, n = mod_w.shape
    v = jnp.concatenate([c[:1], c_ctx[None], jnp.zeros((6, d), F32)], axis=0)
    tn = 512
    out = pl.pallas_call(
        _mod_kernel,
        grid=(depth, n // tn),
        in_specs=[
            pl.BlockSpec((8, d), lambda i, j: (0, 0)),
            pl.BlockSpec((1, d, tn), lambda i, j: (i, 0, j)),
            pl.BlockSpec((1, 1, tn), lambda i, j: (i, 0, j)),
        ],
        out_specs=pl.BlockSpec((1, 8, tn), lambda i, j: (i, 0, j)),
        out_shape=jax.ShapeDtypeStruct((depth, 8, n), F32),
        compiler_params=_cparams(2),
        name="modulation",
    )(v, mod_w, mod_b.reshape(depth, 1, n))
    return out[:, :2].reshape(depth, 2, 6, d)


def _rope_tables(n_lat, n_ctx, head_dim):
    q = head_dim // 4
    t = jnp.arange(n_lat, dtype=I32)
    row = (t // GRID_W).astype(F32)
    col = (t % GRID_W).astype(F32)
    freqs = ROPE_BASE ** (-jnp.arange(q, dtype=F32) / q)
    ar = row[:, None] * freqs[None, :]
    ac = col[:, None] * freqs[None, :]
    cos = jnp.concatenate([jnp.cos(ar), jnp.cos(ar), jnp.cos(ac), jnp.cos(ac)], axis=-1)
    sin = jnp.concatenate([-jnp.sin(ar), jnp.sin(ar), -jnp.sin(ac), jnp.sin(ac)], axis=-1)
    cos = jnp.concatenate([cos, jnp.ones((n_ctx, head_dim), F32)], axis=0)
    sin = jnp.concatenate([sin, jnp.zeros((n_ctx, head_dim), F32)], axis=0)
    return cos, sin


def _proj0_kernel(x_ref, mod_ref, w_ref, cos_ref, sin_ref, o_ref, xm_ref, *, tm, n_lat):
    i = pl.program_id(0)
    j = pl.program_id(1)

    @pl.when(j == 0)
    def _():
        xm_ref[...] = _modulate_rows(x_ref[...], mod_ref, 0, 1, i * tm, n_lat).astype(BF16)

    acc = _dot(xm_ref[...], w_ref[...])

    @pl.when(j >= 2)
    def _():
        o_ref[...] = acc.astype(BF16)

    @pl.when(j < 2)
    def _():
        scale = jnp.where(j == 0, RET_DIM ** -0.5, 1.0).astype(F32)
        for c in range(acc.shape[1] // LANES):
            xs = acc[:, c * LANES:(c + 1) * LANES]
            t0 = (c % 2) * LANES
            rot = xs * cos_ref[:, t0:t0 + LANES] + pltpu.roll(xs, LANES // 2, 1) * sin_ref[:, t0:t0 + LANES]
            o_ref[:, c * LANES:(c + 1) * LANES] = (rot * scale).astype(BF16)


def _proj0(x_all, mod, w_bf, cos, sin, n_lat):
    m, d = x_all.shape
    n = w_bf.shape[1]
    tm = 640 if m % 640 == 0 else 256
    tn = 1024
    return pl.pallas_call(
        functools.partial(_proj0_kernel, tm=tm, n_lat=n_lat),
        grid=(m // tm, n // tn),
        in_specs=[
            pl.BlockSpec((tm, d), lambda i, j: (i, 0)),
            pl.BlockSpec((2, 6, d), lambda i, j: (0, 0, 0)),
            pl.BlockSpec((d, tn), lambda i, j: (0, j)),
            pl.BlockSpec((tm, RET_DIM), lambda i, j: (i, 0)),
            pl.BlockSpec((tm, RET_DIM), lambda i, j: (i, 0)),
        ],
        out_specs=pl.BlockSpec((tm, tn), lambda i, j: (i, j)),
        out_shape=jax.ShapeDtypeStruct((m, n), BF16),
        scratch_shapes=[pltpu.VMEM((tm, d), BF16)],
        compiler_params=_cparams(2),
        name="proj0",
    )(x_all, mod, w_bf, cos, sin)


def _ret_tables(log_decay):
    c = RET_CHUNK
    lg = log_decay.astype(F32)
    idx = jnp.arange(c, dtype=F32)
    diff = idx[:, None] - idx[None, :]
    lower = jnp.where(diff >= 0, jnp.exp(jnp.maximum(diff, 0.0)[None] * lg[0][:, None, None]), 0.0)
    upper = jnp.where(diff <= 0, jnp.exp(jnp.maximum(-diff, 0.0)[None] * lg[1][:, None, None]), 0.0)
    mask = lower + upper
    ones = jnp.ones((1, 1, RET_DIM), F32)
    kdec_f = jnp.exp((c - 1 - idx)[None, :] * lg[0][:, None])[:, :, None] * ones
    qdec_f = jnp.exp((idx + 1)[None, :] * lg[0][:, None])[:, :, None] * ones
    kdec_b = jnp.exp(idx[None, :] * lg[1][:, None])[:, :, None] * ones
    qdec_b = jnp.exp((c - idx)[None, :] * lg[1][:, None])[:, :, None] * ones
    g_chunk = jnp.exp(c * lg)
    return mask, kdec_f, qdec_f, kdec_b, qdec_b, g_chunk


def _ret_bwd_state_kernel(gc_ref, k_ref, v_ref, kdec_ref, sb_ref, s_ref):
    h = pl.program_id(0)
    s = pl.program_id(1)

    @pl.when(s == 0)
    def _():
        s_ref[...] = jnp.zeros_like(s_ref)

    sb_ref[0, 0] = s_ref[...].astype(BF16)
    kd = (k_ref[...].astype(F32) * kdec_ref[0]).astype(BF16)
    s_ref[...] = gc_ref[1, h] * s_ref[...] + _dot_tn(kd, v_ref[...])


def _ret_out_kernel(gc_ref, q_ref, k_ref, v_ref, sb_ref, mask_ref, qdf_ref, qdb_ref, kdf_ref, o_ref, s_ref):
    h = pl.program_id(0)
    s = pl.program_id(1)

    @pl.when(s == 0)
    def _():
        s_ref[...] = jnp.zeros_like(s_ref)

    q = q_ref[...]
    k = k_ref[...]
    v = v_ref[...]
    qf = q.astype(F32)
    w = (_dot_nt(q, k) * mask_ref[0]).astype(BF16)
    o = _dot(w, v)
    o = o + _dot((qf * qdf_ref[0]).astype(BF16), s_ref[...].astype(BF16))
    o = o + _dot((qf * qdb_ref[0]).astype(BF16), sb_ref[0, 0])
    mu = jnp.mean(o, axis=-1, keepdims=True)
    d = o - mu
    var = jnp.mean(d * d, axis=-1, keepdims=True)
    o_ref[...] = (d * lax.rsqrt(var + LN_EPS)).astype(BF16)
    kd = (k.astype(F32) * kdf_ref[0]).astype(BF16)
    s_ref[...] = gc_ref[0, h] * s_ref[...] + _dot_tn(kd, v)


def _retention(p0, log_decay, n_lat, n_ctx):
    m = p0.shape[0]
    c = RET_CHUNK
    nc = m // c
    nlc = n_lat // c
    hh = RET_HEADS
    mask, kdec_f, qdec_f, kdec_b, qdec_b, g_chunk = _ret_tables(log_decay)
    smem = pl.BlockSpec(memory_space=pltpu.SMEM)

    def bchunk(s):
        return nc - 1 - s

    sb = pl.pallas_call(
        _ret_bwd_state_kernel,
        grid=(hh, nc),
        in_specs=[
            smem,
            pl.BlockSpec((c, RET_DIM), lambda h, s: (bchunk(s), hh + h)),
            pl.BlockSpec((c, RET_DIM), lambda h, s: (bchunk(s), 2 * hh + h)),
            pl.BlockSpec((1, c, RET_DIM), lambda h, s: (h, 0, 0)),
        ],
        out_specs=pl.BlockSpec((1, 1, RET_DIM, RET_DIM), lambda h, s: (h, bchunk(s), 0, 0)),
        out_shape=jax.ShapeDtypeStruct((hh, nc, RET_DIM, RET_DIM), BF16),
        scratch_shapes=[pltpu.VMEM((RET_DIM, RET_DIM), F32)],
        compiler_params=_cparams(2),
        name="ret_bwd_state",
    )(g_chunk, p0, p0, kdec_b)

    def fchunk(s):
        return (s + nlc) % nc

    return pl.pallas_call(
        _ret_out_kernel,
        grid=(hh, nc),
        in_specs=[
            smem,
            pl.BlockSpec((c, RET_DIM), lambda h, s: (fchunk(s), h)),
            pl.BlockSpec((c, RET_DIM), lambda h, s: (fchunk(s), hh + h)),
            pl.BlockSpec((c, RET_DIM), lambda h, s: (fchunk(s), 2 * hh + h)),
            pl.BlockSpec((1, 1, RET_DIM, RET_DIM), lambda h, s: (h, fchunk(s), 0, 0)),
            pl.BlockSpec((1, c, c), lambda h, s: (h, 0, 0)),
            pl.BlockSpec((1, c, RET_DIM), lambda h, s: (h, 0, 0)),
            pl.BlockSpec((1, c, RET_DIM), lambda h, s: (h, 0, 0)),
            pl.BlockSpec((1, c, RET_DIM), lambda h, s: (h, 0, 0)),
        ],
        out_specs=pl.BlockSpec((c, RET_DIM), lambda h, s: (fchunk(s), h)),
        out_shape=jax.ShapeDtypeStruct((m, hh * RET_DIM), BF16),
        scratch_shapes=[pltpu.VMEM((RET_DIM, RET_DIM), F32)],
        compiler_params=_cparams(2),
        name="ret_out",
    )(g_chunk, p0, p0, p0, sb, mask, qdec_f, qdec_b, kdec_f)


def _s5_tables(a_re, a_im, log_dt, b_re, b_im, c_re, c_im, d_skip, n_scan):
    f32 = F32
    st = S5_STEP
    g_, p_, s_ = S5_GROUPS, S5_STATE, S5_GROUP
    b_re, b_im, c_re, c_im = (t.astype(f32) for t in (b_re, b_im, c_re, c_im))
    ws, vs, aps, ks = [], [], [], []
    for direction in range(2):
        are, aim = a_re[direction].astype(f32), a_im[direction].astype(f32)
        dt = jnp.exp(log_dt[direction].astype(f32))[:, None]
        zr, zi = are * dt, aim * dt
        mag = jnp.exp(zr)
        ab_re, ab_im = mag * jnp.cos(zi), mag * jnp.sin(zi)
        den = jnp.square(are) + jnp.square(aim)
        nr, ni = ab_re - 1.0, ab_im
        f_re = (nr * are + ni * aim) / den
        f_im = (ni * are - nr * aim) / den
        bb_re = f_re[..., None] * b_re - f_im[..., None] * b_im
        bb_im = f_re[..., None] * b_im + f_im[..., None] * b_re
        pr, pi = [jnp.ones_like(ab_re)], [jnp.zeros_like(ab_im)]
        for _ in range(st):
            pr.append(pr[-1] * ab_re - pi[-1] * ab_im)
            pi.append(pr[-2] * ab_im + pi[-1] * ab_re)
        pw_re, pw_im = jnp.stack(pr), jnp.stack(pi)
        ca_re = c_re[None] * pw_re[:, :, None, :] - c_im[None] * pw_im[:, :, None, :]
        ca_im = c_re[None] * pw_im[:, :, None, :] + c_im[None] * pw_re[:, :, None, :]
        kk = (jnp.einsum('tgip,gpj->tgij', ca_re[:st], bb_re, precision='highest')
              - jnp.einsum('tgip,gpj->tgij', ca_im[:st], bb_im, precision='highest'))
        ks.append(kk)
        e = (st - 1 - jnp.arange(st)) if direction == 0 else jnp.arange(st)
        w_re = pw_re[e][:, :, :, None] * bb_re[None] - pw_im[e][:, :, :, None] * bb_im[None]
        w_im = pw_re[e][:, :, :, None] * bb_im[None] + pw_im[e][:, :, :, None] * bb_re[None]
        w = jnp.concatenate([w_re, w_im], axis=2)
        ws.append(jnp.transpose(w, (1, 0, 3, 2)).reshape(g_, st * s_, 2 * p_))
        e2 = (jnp.arange(st) + 1) if direction == 0 else (st - jnp.arange(st))
        v = jnp.concatenate([ca_re[e2], -ca_im[e2]], axis=3)
        vs.append(jnp.transpose(v, (1, 3, 0, 2)).reshape(g_, 2 * p_, st * s_))
        qr, qi = pw_re[st], pw_im[st]
        rows = []
        for _ in range(n_scan):
            rows.append(jnp.stack([jnp.concatenate([qr, qr], -1), jnp.concatenate([-qi, qi], -1)], axis=1))
            qr, qi = qr * qr - qi * qi, 2.0 * qr * qi
        aps.append(jnp.stack(rows, axis=1))
    tt = jnp.arange(st)
    lag = tt[None, :] - tt[:, None]
    kf = ks[0][jnp.clip(lag, 0, st - 1)]
    kb = ks[1][jnp.clip(-lag, 0, st - 1)]
    tm = jnp.where((lag >= 0)[:, :, None, None, None], kf, 0.0) + jnp.where((lag <= 0)[:, :, None, None, None], kb, 0.0)
    tm = jnp.transpose(tm, (2, 0, 4, 1, 3)).reshape(g_, st * s_, st * s_)
    dsk = d_skip.astype(f32).reshape(g_, s_)
    tm = tm + jnp.eye(st * s_, dtype=f32)[None] * jnp.tile(dsk, (1, st))[:, None, :]
    return tm, jnp.stack(ws), jnp.stack(vs), jnp.stack(aps)


def _s5_kernel(u_ref, t_ref, w_ref, v_ref, a_ref, y_ref, uf_ref, ut_ref, yg_ref, ha_ref, hb_ref,
               *, ncs, nls, n_scan):
    st = S5_STEP
    nt = nls + ncs
    groups = LANES // S5_GROUP
    tw = st * S5_GROUP
    shift = S5_GROUP.bit_length() - 1
    lane_shift = LANES.bit_length() - 1
    uf_ref[...] = u_ref[...].astype(F32)
    for t in range(st):
        ut_ref[t // 2, :, (t % 2) * LANES:(t % 2 + 1) * LANES] = uf_ref[pl.ds(t, nt, stride=st), :].astype(BF16)
    src = lax.broadcasted_iota(I32, (2 * LANES, tw), 0)
    pos = lax.broadcasted_iota(I32, (2 * LANES, tw), 1)
    for g in range(groups):
        base = ((src & (LANES - 1)) == g * S5_GROUP + (pos & (S5_GROUP - 1)))
        ug = jnp.zeros((nt, tw), F32)
        for p in range(st // 2):
            sel = jnp.where(base & ((pos >> shift) == 2 * p + (src >> lane_shift)), 1.0, 0.0).astype(BF16)
            ug = ug + _dot(ut_ref[p], sel)
        ug = ug.astype(BF16)
        useq = jnp.concatenate([ug[nls:nt], ug[0:nls], ug[nls:nt]], axis=0)
        yg_ref[g] = _s5_group(useq, t_ref, w_ref, v_ref, a_ref, ha_ref, hb_ref, g, ncs, nls, n_scan).astype(BF16)
    dst = lax.broadcasted_iota(I32, (tw, 2 * LANES), 1)
    pos_t = lax.broadcasted_iota(I32, (tw, 2 * LANES), 0)
    for p in range(st // 2):
        rows2 = jnp.zeros((nt, 2 * LANES), F32)
        for g in range(groups):
            hit = (((dst & (LANES - 1)) == g * S5_GROUP + (pos_t & (S5_GROUP - 1)))
                   & ((pos_t >> shift) == 2 * p + (dst >> lane_shift)))
            rows2 = rows2 + _dot(yg_ref[g], jnp.where(hit, 1.0, 0.0).astype(BF16))
        uf_ref[pl.ds(2 * p, nt, stride=st), :] = rows2[:, 0:LANES]
        uf_ref[pl.ds(2 * p + 1, nt, stride=st), :] = rows2[:, LANES:2 * LANES]
    y_ref[...] = uf_ref[...].astype(BF16)


def _s5_group(u, t_ref, w_ref, v_ref, a_ref, ha_ref, hb_ref, g, ncs, nls, n_scan):
    n = ncs + nls + ncs

    def mm(x, mat):
        mh, ml = _split(mat)
        return _dot(x, mh) + _dot(x, ml)

    def mm3(x, mat):
        xh, xl = _split(x)
        mh, ml = _split(mat)
        return _dot(xh, mh) + _dot(xl, mh) + _dot(xh, ml)

    rows = lax.broadcasted_iota(I32, (n, 1), 0)

    def scan(z, direction):
        bufs = (ha_ref, hb_ref)
        bufs[0][...] = z
        for kk in range(n_scan):
            src, dst = bufs[kk % 2], bufs[(kk + 1) % 2]
            sft = 1 << kk
            a1 = a_ref[direction, g, kk, 0:1, :]
            a2 = a_ref[direction, g, kk, 1:2, :]
            if sft >= n:
                dst[...] = src[...]
                continue
            if sft % 8 == 0:
                if direction == 0:
                    prev = src[0:n - sft, :]
                    dst[0:sft, :] = src[0:sft, :]
                    dst[sft:n, :] = src[sft:n, :] + a1 * prev + a2 * pltpu.roll(prev, S5_STATE, 1)
                else:
                    nxt = src[sft:n, :]
                    dst[n - sft:n, :] = src[n - sft:n, :]
                    dst[0:n - sft, :] = src[0:n - sft, :] + a1 * nxt + a2 * pltpu.roll(nxt, S5_STATE, 1)
            else:
                cur = src[...]
                if direction == 0:
                    sh = jnp.where(rows >= sft, pltpu.roll(cur, sft, 0), 0.0)
                else:
                    sh = jnp.where(rows < n - sft, pltpu.roll(cur, n - sft, 0), 0.0)
                dst[...] = cur + a1 * sh + a2 * pltpu.roll(sh, S5_STATE, 1)
        return bufs[n_scan % 2][...]

    y = mm(u, t_ref[g])
    hf = scan(mm(u, w_ref[0, g]), 0)
    hf_prev = jnp.where(rows >= 1, pltpu.roll(hf, 1, 0), 0.0)
    yf = mm3(hf_prev, v_ref[0, g])
    hb = scan(mm(u, w_ref[1, g]), 1)
    hb_next = jnp.where(rows < n - 1, pltpu.roll(hb, n - 1, 0), 0.0)
    yb = mm3(hb_next, v_ref[1, g])
    return jnp.concatenate([(y + yf + yb)[ncs:ncs + nls, :], (y + yf)[0:ncs, :] + yb[ncs + nls:n, :]], axis=0)


def _s5(p0, col0, tabs, n_lat, n_ctx):
    m = p0.shape[0]
    st = S5_STEP
    nls, ncs = n_lat // st, n_ctx // st
    n = ncs + nls + ncs
    nt = nls + ncs
    n_scan = max(1, (n - 1).bit_length())
    t_mat, w_mat, v_mat, apow = tabs(n_scan)
    tw = st * S5_GROUP
    gpl = LANES // S5_GROUP
    blk0 = col0 // LANES
    return pl.pallas_call(
        functools.partial(_s5_kernel, ncs=ncs, nls=nls, n_scan=n_scan),
        grid=(S5_CHANNELS // LANES,),
        in_specs=[
            pl.BlockSpec((m, LANES), lambda j: (0, blk0 + j)),
            pl.BlockSpec((gpl, tw, tw), lambda j: (j, 0, 0)),
            pl.BlockSpec((2, gpl, tw, 2 * S5_STATE), lambda j: (0, j, 0, 0)),
            pl.BlockSpec((2, gpl, 2 * S5_STATE, tw), lambda j: (0, j, 0, 0)),
            pl.BlockSpec((2, gpl, n_scan, 2, 2 * S5_STATE), lambda j: (0, j, 0, 0, 0)),
        ],
        out_specs=pl.BlockSpec((m, LANES), lambda j: (0, j)),
        out_shape=jax.ShapeDtypeStruct((m, S5_CHANNELS), BF16),
        scratch_shapes=[pltpu.VMEM((m, LANES), F32), pltpu.VMEM((st // 2, nt, 2 * LANES), BF16),
                        pltpu.VMEM((gpl, nt, tw), BF16),
                        pltpu.VMEM((n, 2 * S5_STATE), F32), pltpu.VMEM((n, 2 * S5_STATE), F32)],
        compiler_params=_cparams(1),
        name="s5",
    )(p0, t_mat, w_mat, v_mat, apow)


def _post_mix(y, x_ref, mod_ref, ln_ref, rw_ref, xo_ref, xp_ref, lg_ref, row0, n_lat):
    tm = y.shape[0]
    g1 = _select_rows(mod_ref, 2, row0, tm, n_lat)
    xn = _layer_norm(DEEPNORM_ALPHA * x_ref[...] + g1 * y, ln_ref[0:1, :], ln_ref[1:2, :])
    xo_ref[...] = xn
    sc2 = _select_rows(mod_ref, 4, row0, tm, n_lat)
    sh2 = _select_rows(mod_ref, 3, row0, tm, n_lat)
    x2 = xn * (1.0 + sc2) + sh2
    xp_ref[...] = _pack_bf16_pairs(x2)
    xh, xl = _split(x2)
    rh, rl = _split(rw_ref[...])
    lg_ref[...] = _dot_nt(rh, xh) + _dot_nt(rh, xl) + _dot_nt(rl, xh)


def _finish0_kernel(r_ref, g_ref, s_ref, x_ref, mod_ref, ln_ref, wglu_ref, bglu_ref, wout_ref, rw_ref,
                    xo_ref, xp_ref, lg_ref, *, tm, n_lat):
    row0 = pl.program_id(0) * tm
    ret = (r_ref[...].astype(F32) * _silu(g_ref[...].astype(F32))).astype(BF16)
    z = _gelu_tanh(s_ref[...].astype(F32))
    zb = z.astype(BF16)
    gate = _sigmoid(_dot(zb, wglu_ref[...]) + bglu_ref[...])
    s5o = (z * gate).astype(BF16)
    half = ret.shape[1]
    y = _dot(ret, wout_ref[0:half, :]) + _dot(s5o, wout_ref[half:, :])
    _post_mix(y, x_ref, mod_ref, ln_ref, rw_ref, xo_ref, xp_ref, lg_ref, row0, n_lat)


def _const_spec(shape):
    nd = len(shape)
    return pl.BlockSpec(shape, lambda i: (0,) * nd)


def _finish0(rn, p0, s5y, x_all, mod, ln, wglu, bglu, wout, rw, n_lat):
    m, d = x_all.shape
    tm = ROW_TILE
    half = d // 2
    outs = pl.pallas_call(
        functools.partial(_finish0_kernel, tm=tm, n_lat=n_lat),
        grid=(m // tm,),
        in_specs=[
            pl.BlockSpec((tm, half), lambda i: (i, 0)),
            pl.BlockSpec((tm, half), lambda i: (i, 3)),
            pl.BlockSpec((tm, half), lambda i: (i, 0)),
            pl.BlockSpec((tm, d), lambda i: (i, 0)),
            _const_spec((2, 6, d)),
            _const_spec((2, d)),
            _const_spec((half, half)),
            _const_spec((1, half)),
            _const_spec((d, d)),
            _const_spec((N_EXPERTS, d)),
        ],
        out_specs=[
            pl.BlockSpec((tm, d), lambda i: (i, 0)),
            pl.BlockSpec((tm, half), lambda i: (i, 0)),
            pl.BlockSpec((N_EXPERTS, tm), lambda i: (0, i)),
        ],
        out_shape=[
            jax.ShapeDtypeStruct((m, d), F32),
            jax.ShapeDtypeStruct((m, half), U32),
            jax.ShapeDtypeStruct((N_EXPERTS, m), F32),
        ],
        compiler_params=_cparams(1),
        name="finish0",
    )(rn, p0, s5y, x_all, mod, ln, wglu, bglu, wout, rw)
    return outs


def _finish1_kernel(og_ref, od_ref, x_ref, mod_ref, ln_ref, wout_ref, rw_ref, xo_ref, xp_ref, lg_ref, *, tm, n_lat):
    row0 = pl.program_id(0) * tm
    half = og_ref.shape[1]
    y = _dot(og_ref[...], wout_ref[0:half, :]) + _dot(od_ref[...], wout_ref[half:, :])
    _post_mix(y, x_ref, mod_ref, ln_ref, rw_ref, xo_ref, xp_ref, lg_ref, row0, n_lat)


def _finish1(og, od, x_all, mod, ln, wout, rw, n_lat):
    d = x_all.shape[1]
    m = og.shape[0]
    tm = ROW_TILE
    half = d // 2
    return pl.pallas_call(
        functools.partial(_finish1_kernel, tm=tm, n_lat=n_lat),
        grid=(m // tm,),
        in_specs=[
            pl.BlockSpec((tm, half), lambda i: (i, 0)),
            pl.BlockSpec((tm, half), lambda i: (i, 0)),
            pl.BlockSpec((tm, d), lambda i: (i, 0)),
            _const_spec((2, 6, d)),
            _const_spec((2, d)),
            _const_spec((d, d)),
            _const_spec((N_EXPERTS, d)),
        ],
        out_specs=[
            pl.BlockSpec((tm, d), lambda i: (i, 0)),
            pl.BlockSpec((tm, half), lambda i: (i, 0)),
            pl.BlockSpec((N_EXPERTS, tm), lambda i: (0, i)),
        ],
        out_shape=[
            jax.ShapeDtypeStruct((m, d), F32),
            jax.ShapeDtypeStruct((m, half), U32),
            jax.ShapeDtypeStruct((N_EXPERTS, m), F32),
        ],
        compiler_params=_cparams(1),
        name="finish1",
    )(og, od, x_all, mod, ln, wout, rw)


def _route_kernel(lg_ref, bias_ref, e_ref, r_ref, w_ref, cnt_ref, base_ref, blk_ref, carry_ref, *, tb):
    ne = N_EXPERTS
    gsz = ne // N_EXPERT_GROUPS
    neg = -jnp.inf

    @pl.when(pl.program_id(0) == 0)
    def _():
        carry_ref[...] = jnp.zeros_like(carry_ref)

    s = _sigmoid(lg_ref[...])
    sel = s + bias_ref[...]
    gs = []
    for g in range(N_EXPERT_GROUPS):
        blk = sel[g * gsz:(g + 1) * gsz, :]
        m1 = jnp.max(blk, axis=0, keepdims=True)
        n_eq = jnp.sum(jnp.where(blk == m1, 1.0, 0.0), axis=0, keepdims=True)
        m2 = jnp.max(jnp.where(blk < m1, blk, neg), axis=0, keepdims=True)
        gs.append(m1 + jnp.where(n_eq >= 2.0, m1, m2))
    masked = []
    for g in range(N_EXPERT_GROUPS):
        ahead = jnp.zeros_like(gs[g])
        for h in range(N_EXPERT_GROUPS):
            if h == g:
                continue
            beats = ((gs[h] > gs[g]) | (gs[h] == gs[g])) if h < g else (gs[h] > gs[g])
            ahead = ahead + jnp.where(beats, 1.0, 0.0)
        masked.append(jnp.where(ahead < float(TOPK_GROUPS), sel[g * gsz:(g + 1) * gsz, :], neg))
    selm = jnp.concatenate(masked, axis=0)
    eid = lax.broadcasted_iota(I32, (ne, 1), 0)
    ahead = jnp.zeros_like(selm)
    for e in range(ne):
        row = selm[e:e + 1, :]
        beats = (row > selm) | ((row == selm) & (eid > e))
        ahead = ahead + jnp.where(beats, 1.0, 0.0)
    chosen = ahead < float(TOP_K)
    member = jnp.where(chosen, 1.0, 0.0)
    ssel = jnp.where(chosen, s, 0.0)
    wd = ssel / jnp.sum(ssel, axis=0, keepdims=True) * ROUTED_SCALE
    mb = member.astype(BF16)
    ti = lax.broadcasted_iota(I32, (tb, tb), 0)
    tj = lax.broadcasted_iota(I32, (tb, tb), 1)
    tri = jnp.where(ti < tj, 1.0, 0.0).astype(BF16)
    rank = carry_ref[:, 0:1] + _dot(mb, tri)
    in_block = jnp.sum(member, axis=1, keepdims=True)
    base_ref[0] = carry_ref[...]
    blk_ref[0] = jnp.zeros_like(carry_ref) + in_block
    carry_ref[...] = carry_ref[...] + in_block
    cnt_ref[...] = carry_ref[...]
    ei = lax.broadcasted_iota(I32, (ne, ne), 0)
    ej = lax.broadcasted_iota(I32, (ne, ne), 1)
    low = jnp.where(ej < ei, 1.0, 0.0).astype(BF16)
    slot = _dot(low, mb)
    eidf = eid.astype(F32)
    for k in range(TOP_K):
        hit = chosen & (slot == float(k))
        e_ref[k:k + 1, :] = jnp.sum(jnp.where(hit, eidf, 0.0), axis=0, keepdims=True).astype(I32)
        r_ref[k:k + 1, :] = jnp.sum(jnp.where(hit, rank, 0.0), axis=0, keepdims=True).astype(I32)
        w_ref[k:k + 1, :] = jnp.sum(jnp.where(hit, wd, 0.0), axis=0, keepdims=True)


def _route(logits_t, router_bias, tm):
    e, t = logits_t.shape
    tb = ROW_TILE
    eidx, rank, wts, counts, base, in_block = pl.pallas_call(
        functools.partial(_route_kernel, tb=tb),
        grid=(t // tb,),
        in_specs=[
            pl.BlockSpec((e, tb), lambda i: (0, i)),
            pl.BlockSpec((e, 1), lambda i: (0, 0)),
        ],
        out_specs=[
            pl.BlockSpec((TOP_K, tb), lambda i: (0, i)),
            pl.BlockSpec((TOP_K, tb), lambda i: (0, i)),
            pl.BlockSpec((TOP_K, tb), lambda i: (0, i)),
            pl.BlockSpec((e, LANES), lambda i: (0, 0)),
            pl.BlockSpec((1, e, LANES), lambda i: (i, 0, 0)),
            pl.BlockSpec((1, e, LANES), lambda i: (i, 0, 0)),
        ],
        out_shape=[
            jax.ShapeDtypeStruct((TOP_K, t), I32),
            jax.ShapeDtypeStruct((TOP_K, t), I32),
            jax.ShapeDtypeStruct((TOP_K, t), F32),
            jax.ShapeDtypeStruct((e, LANES), F32),
            jax.ShapeDtypeStruct((t // tb, e, LANES), F32),
            jax.ShapeDtypeStruct((t // tb, e, LANES), F32),
        ],
        scratch_shapes=[pltpu.VMEM((e, LANES), F32)],
        compiler_params=_cparams(1),
        name="moe_route",
    )(logits_t, router_bias.astype(F32).reshape(e, 1))
    counts = counts[:, 0].astype(I32)
    padded = (counts + tm - 1) // tm * tm
    pad_end = jnp.cumsum(padded)
    pad_start = pad_end - padded
    ids = jnp.arange(e, dtype=I32)
    start_of = jnp.sum(jnp.where(eidx[:, :, None] == ids, pad_start, 0), axis=-1)
    dest = (start_of + rank).T
    n_blocks = (t * TOP_K + e * (tm - 1)) // tm
    blk_start = jnp.arange(n_blocks, dtype=I32) * tm
    blk_expert = jnp.minimum(jnp.sum((blk_start[:, None] >= pad_end[None, :]).astype(I32), axis=1), e - 1)
    n_used = (pad_end[-1] // tm).astype(I32).reshape(1)
    plan = _combine_plan(dest, eidx.T, base[:, :, 0].astype(I32), in_block[:, :, 0].astype(I32), pad_start, tb)
    return dest, wts.T, blk_expert, n_used, n_blocks, pad_end.astype(I32), padded.astype(I32), plan


def _dispatch_kernel(pad_end_ref, padded_ref, dest_ref, x_ref, xs_ref, zero_ref, sem, zsem, *, tb, tm):
    @pl.when(pl.program_id(0) == 0)
    def _():
        zero_ref[...] = jnp.zeros_like(zero_ref)

        def fill(e):
            first = pl.multiple_of(pad_end_ref[e] - tm, tm)
            return pltpu.make_async_copy(zero_ref, xs_ref.at[pl.ds(first, tm)], zsem)

        for e in range(N_EXPERTS):
            @pl.when(padded_ref[e] > 0)
            def _():
                fill(e).start()
        for e in range(N_EXPERTS):
            @pl.when(padded_ref[e] > 0)
            def _():
                fill(e).wait()

    def row_copy(r, k):
        d = dest_ref[0, 0, r * TOP_K + k]
        return pltpu.make_async_copy(x_ref.at[pl.ds(r, 1)], xs_ref.at[pl.ds(d, 1)], sem)

    def start(r, carry):
        for k in range(TOP_K):
            row_copy(r, k).start(priority=k % 2)
        return carry

    def wait(r, carry):
        for k in range(TOP_K):
            row_copy(r, k).wait()
        return carry

    lax.fori_loop(0, tb, start, 0)
    lax.fori_loop(0, tb, wait, 0)


def _dispatch(xp, dest, pad_end, padded, n_pad, tm):
    t, w = xp.shape
    tb = ROW_TILE
    smem = pl.BlockSpec(memory_space=pltpu.SMEM)
    return pl.pallas_call(
        functools.partial(_dispatch_kernel, tb=tb, tm=tm),
        grid=(t // tb,),
        in_specs=[
            smem,
            smem,
            pl.BlockSpec((1, 1, tb * TOP_K), lambda i: (i, 0, 0), memory_space=pltpu.SMEM),
            pl.BlockSpec((tb, w), lambda i: (i, 0)),
        ],
        out_specs=pl.BlockSpec(memory_space=pl.ANY),
        out_shape=jax.ShapeDtypeStruct((n_pad, w), U32),
        scratch_shapes=[pltpu.VMEM((tm, w), U32), pltpu.SemaphoreType.DMA(()), pltpu.SemaphoreType.DMA(())],
        compiler_params=_cparams(1),
        name="moe_dispatch",
    )(pad_end, padded, dest.reshape(t // tb, 1, tb * TOP_K), xp)


def _expert_kernel(be_ref, nu_ref, xs_ref, wg_ref, wu_ref, wd_ref, y_ref, wgb_ref, wub_ref, wdb_ref):
    b = pl.program_id(0)
    e = be_ref[b]
    prev = be_ref[jnp.maximum(b - 1, 0)]

    @pl.when((b == 0) | (e != prev))
    def _():
        wgb_ref[...] = wg_ref[0, 0].astype(BF16)
        wub_ref[...] = wu_ref[0, 0].astype(BF16)
        wdb_ref[...] = wd_ref[0, 0].astype(BF16)

    @pl.when(b < nu_ref[0])
    def _():
        lo, hi = _unpack_bf16_pairs(xs_ref[...])
        lo = lo.astype(BF16)
        hi = hi.astype(BF16)
        half = lo.shape[1]
        gate = _dot(lo, wgb_ref[0:half, :]) + _dot(hi, wgb_ref[half:, :])
        up = _dot(lo, wub_ref[0:half, :]) + _dot(hi, wub_ref[half:, :])
        hid = (_silu(gate) * up).astype(BF16)
        y_ref[...] = _pack_bf16_pairs(_dot(hid, wdb_ref[...]))

    @pl.when(b >= nu_ref[0])
    def _():
        y_ref[...] = jnp.zeros_like(y_ref)


def _experts(xs, blk_expert, n_used, w_gate, w_up, w_down, layer):
    n_pad, w = xs.shape
    tm = MOE_TM
    d, hdn = w_gate.shape[2], w_gate.shape[3]
    return pl.pallas_call(
        _expert_kernel,
        grid_spec=pltpu.PrefetchScalarGridSpec(
            num_scalar_prefetch=2,
            grid=(n_pad // tm,),
            in_specs=[
                pl.BlockSpec((tm, w), lambda b, be, nu: (b, 0)),
                pl.BlockSpec((1, 1, d, hdn), lambda b, be, nu: (layer, be[b], 0, 0)),
                pl.BlockSpec((1, 1, d, hdn), lambda b, be, nu: (layer, be[b], 0, 0)),
                pl.BlockSpec((1, 1, hdn, d), lambda b, be, nu: (layer, be[b], 0, 0)),
            ],
            out_specs=pl.BlockSpec((tm, w), lambda b, be, nu: (b, 0)),
            scratch_shapes=[pltpu.VMEM((d, hdn), BF16), pltpu.VMEM((d, hdn), BF16), pltpu.VMEM((hdn, d), BF16)],
        ),
        out_shape=jax.ShapeDtypeStruct((n_pad, w), U32),
        compiler_params=_cparams(1),
        name="moe_experts",
    )(blk_expert, n_used, xs, w_gate, w_up, w_down)


def _combine_kernel(tot_ref, src_ref, xp_ref, ws_ref, col_ref, x_ref, mod_ref, ln_ref, sg_ref, su_ref, sd_ref,
                    y_hbm, xo_ref, rbuf, sem, *, tb, n_lat):
    i = pl.program_id(0)
    row0 = i * tb
    ch = COMBINE_CHUNK
    n_rows = rbuf.shape[0]

    @pl.when(i == 0)
    def _():
        rbuf[...] = jnp.zeros_like(rbuf)

    total = tot_ref[i]

    def chunk_copy(c):
        first = pl.multiple_of(src_ref[0, 0, c] * ch, ch)
        return pltpu.make_async_copy(y_hbm.at[pl.ds(first, ch)], rbuf.at[pl.ds(pl.multiple_of(c * ch, ch), ch)], sem)

    def start(c, carry):
        chunk_copy(c).start()
        return carry

    def wait(c, carry):
        chunk_copy(c).wait()
        return carry

    lax.fori_loop(0, total, start, 0)
    lo, hi = _unpack_bf16_pairs(xp_ref[...])
    lo = lo.astype(BF16)
    hi = hi.astype(BF16)
    half = lo.shape[1]
    gate = _dot(lo, sg_ref[0:half, :]) + _dot(hi, sg_ref[half:, :])
    up = _dot(lo, su_ref[0:half, :]) + _dot(hi, su_ref[half:, :])
    f = _dot((_silu(gate) * up).astype(BF16), sd_ref[...])
    lax.fori_loop(0, total, wait, 0)
    cols = col_ref[...]
    ws = ws_ref[...]
    acc_lo = jnp.zeros((tb, half), F32)
    acc_hi = jnp.zeros((tb, half), F32)
    piece = COMBINE_PIECE
    for p in range(n_rows // piece):
        pos = p * piece + lax.broadcasted_iota(I32, (tb, piece), 1)
        g = jnp.zeros((tb, piece), F32)
        for k in range(TOP_K):
            g = jnp.where(pos == cols[:, k:k + 1], ws[:, k:k + 1], g)
        gb = g.astype(BF16)
        ylo, yhi = _unpack_bf16_pairs(rbuf[p * piece:(p + 1) * piece, :])
        acc_lo = acc_lo + _dot(gb, ylo.astype(BF16))
        acc_hi = acc_hi + _dot(gb, yhi.astype(BF16))
    f = f + jnp.concatenate([acc_lo, acc_hi], axis=1)
    g2 = _select_rows(mod_ref, 5, row0, tb, n_lat)
    xo_ref[...] = _layer_norm(DEEPNORM_ALPHA * x_ref[...] + g2 * f, ln_ref[0:1, :], ln_ref[1:2, :])


def _combine_plan(dest, eidx, base, cnt, pad_start, tb):
    t, k = dest.shape
    e = pad_start.shape[0]
    ch = COMBINE_CHUNK
    nb = t // tb
    cmax = COMBINE_ROWS // ch
    run0 = pad_start[None, :] + base
    first = run0 // ch
    last = (run0 + cnt - 1) // ch
    nch = jnp.where(cnt > 0, last - first + 1, 0)
    end = jnp.cumsum(nch, axis=1)
    off = end - nch
    total = end[:, -1].astype(I32)
    c = jnp.arange(cmax, dtype=I32)
    e_of_c = jnp.minimum(jnp.sum((end[:, None, :] <= c[None, :, None]).astype(I32), axis=-1), e - 1)
    onehot_c = e_of_c[:, :, None] == jnp.arange(e, dtype=I32)
    src = jnp.sum(jnp.where(onehot_c, (first - off)[:, None, :], 0), axis=-1) + c[None, :]
    src = jnp.where(c[None, :] < total[:, None], src, 0).astype(I32)
    onehot_t = eidx.reshape(nb, tb, k)[..., None] == jnp.arange(e, dtype=I32)
    shift = jnp.sum(jnp.where(onehot_t, ((off - first) * ch)[:, None, None, :], 0), axis=-1)
    col = (dest.reshape(nb, tb, k) + shift).reshape(t, k).astype(I32)
    return total, src.reshape(nb, 1, cmax), col


def _combine(plan, xp, wsel, x_all, mod, ln, sg, su, sd, y, n_lat):
    t, w = xp.shape
    d = x_all.shape[1]
    tb = ROW_TILE
    hdn = sg.shape[1]
    total, src, col = plan
    return pl.pallas_call(
        functools.partial(_combine_kernel, tb=tb, n_lat=n_lat),
        grid=(t // tb,),
        in_specs=[
            pl.BlockSpec(memory_space=pltpu.SMEM),
            pl.BlockSpec((1, 1, src.shape[2]), lambda i: (i, 0, 0), memory_space=pltpu.SMEM),
            pl.BlockSpec((tb, w), lambda i: (i, 0)),
            pl.BlockSpec((tb, TOP_K), lambda i: (i, 0)),
            pl.BlockSpec((tb, TOP_K), lambda i: (i, 0)),
            pl.BlockSpec((tb, d), lambda i: (i, 0)),
            _const_spec((2, 6, d)),
            _const_spec((2, d)),
            _const_spec((d, hdn)),
            _const_spec((d, hdn)),
            _const_spec((hdn, d)),
            pl.BlockSpec(memory_space=pl.ANY),
        ],
        out_specs=pl.BlockSpec((tb, d), lambda i: (i, 0)),
        out_shape=jax.ShapeDtypeStruct((t, d), F32),
        scratch_shapes=[pltpu.VMEM((COMBINE_ROWS, w), U32), pltpu.SemaphoreType.DMA(())],
        compiler_params=_cparams(1),
        name="moe_combine",
    )(total, src, xp, wsel, col, x_all, mod, ln, sg, su, sd, y)


def _moe(xp, logits, x_res, mod, ln, router_bias, w_gate, w_up, w_down, layer, sg, su, sd, n_lat):
    dest, wsel, blk_expert, n_used, n_blocks, pad_end, padded, plan = _route(logits, router_bias, MOE_TM)
    xs = _dispatch(xp, dest, pad_end, padded, n_blocks * MOE_TM, MOE_TM)
    y = _experts(xs, blk_expert, n_used, w_gate, w_up, w_down, layer)
    return _combine(plan, xp, wsel, x_res, mod, ln, sg.astype(BF16), su.astype(BF16), sd.astype(BF16), y, n_lat)


def _rot_lanes(xs, cos, sin):
    q = HEAD_DIM // 4
    lane = lax.broadcasted_iota(I32, xs.shape, 1)
    partner = jnp.where((lane % (2 * q)) < q, pltpu.roll(xs, HEAD_DIM - q, 1), pltpu.roll(xs, q, 1))
    return xs * cos + partner * sin


def _projq_kernel(x_ref, mod_ref, w_ref, cos_ref, sin_ref, o_ref, xm_ref, *, tm, n_lat):
    i = pl.program_id(0)
    j = pl.program_id(1)

    @pl.when(j == 0)
    def _():
        xm_ref[...] = _modulate_rows(x_ref[...], mod_ref, 0, 1, i * tm, n_lat).astype(BF16)

    acc = _dot(xm_ref[...], w_ref[...])
    n_heads = acc.shape[1] // HEAD_DIM

    @pl.when(j == 0)
    def _():
        for c in range(n_heads):
            xs = acc[:, c * HEAD_DIM:(c + 1) * HEAD_DIM]
            xs = xs * lax.rsqrt(jnp.mean(xs * xs, axis=-1, keepdims=True) + LN_EPS)
            o_ref[:, c * HEAD_DIM:(c + 1) * HEAD_DIM] = _rot_lanes(xs, cos_ref[0], sin_ref[0]).astype(BF16)

    @pl.when(j == 1)
    def _():
        for c in range(n_heads):
            xs = acc[:, c * HEAD_DIM:(c + 1) * HEAD_DIM]
            o_ref[:, c * HEAD_DIM:(c + 1) * HEAD_DIM] = _rot_lanes(xs, cos_ref[1], sin_ref[1]).astype(BF16)


def _projq(x_all, mod, w_bf, cos2, sin2, n_lat):
    d = x_all.shape[1]
    n = w_bf.shape[1]
    tm = 512 if n_lat % 512 == 0 else 256
    tn = n // 2
    return pl.pallas_call(
        functools.partial(_projq_kernel, tm=tm, n_lat=n_lat),
        grid=(n_lat // tm, 2),
        in_specs=[
            pl.BlockSpec((tm, d), lambda i, j: (i, 0)),
            pl.BlockSpec((2, 6, d), lambda i, j: (0, 0, 0)),
            pl.BlockSpec((d, tn), lambda i, j: (0, j)),
            pl.BlockSpec((2, tm, HEAD_DIM), lambda i, j: (0, i, 0)),
            pl.BlockSpec((2, tm, HEAD_DIM), lambda i, j: (0, i, 0)),
        ],
        out_specs=pl.BlockSpec((tm, tn), lambda i, j: (i, j)),
        out_shape=jax.ShapeDtypeStruct((n_lat, n), BF16),
        scratch_shapes=[pltpu.VMEM((tm, d), BF16)],
        compiler_params=_cparams(2),
        name="proj1_q",
    )(x_all, mod, w_bf, cos2, sin2)


def _projkv_kernel(x_ref, mod_ref, wv_ref, wkt_ref, cos_ref, sin_ref, v_ref, kt_ref, *, tm, n_lat):
    i = pl.program_id(0)
    xm = _modulate_rows(x_ref[...], mod_ref, 0, 1, i * tm, n_lat).astype(BF16)
    v_ref[...] = _dot(xm, wv_ref[...]).astype(BF16)
    kt = _dot_nt(wkt_ref[...], xm)
    q = HEAD_DIM // 4
    n_gk = GQA_KV_HEADS * HEAD_DIM
    for c in range(kt.shape[0] // HEAD_DIM):
        xs = kt[c * HEAD_DIM:(c + 1) * HEAD_DIM, :]
        t = 0 if c * HEAD_DIM < n_gk else 1
        if t == 0:
            xs = xs * lax.rsqrt(jnp.mean(xs * xs, axis=0, keepdims=True) + LN_EPS)
        partner = jnp.concatenate([xs[q:2 * q], xs[0:q], xs[3 * q:4 * q], xs[2 * q:3 * q]], axis=0)
        kt_ref[c * HEAD_DIM:(c + 1) * HEAD_DIM, :] = (xs * cos_ref[t] + partner * sin_ref[t]).astype(BF16)


def _projkv(x_all, mod, wv_bf, wkt_bf, cos_t, sin_t, n_lat):
    m, d = x_all.shape
    nv = wv_bf.shape[1]
    nk = wkt_bf.shape[0]
    tm = 640 if m % 640 == 0 else 256
    return pl.pallas_call(
        functools.partial(_projkv_kernel, tm=tm, n_lat=n_lat),
        grid=(m // tm,),
        in_specs=[
            pl.BlockSpec((tm, d), lambda i: (i, 0)),
            _const_spec((2, 6, d)),
            _const_spec((d, nv)),
            _const_spec((nk, d)),
            pl.BlockSpec((2, HEAD_DIM, tm), lambda i: (0, 0, i)),
            pl.BlockSpec((2, HEAD_DIM, tm), lambda i: (0, 0, i)),
        ],
        out_specs=[
            pl.BlockSpec((tm, nv), lambda i: (i, 0)),
            pl.BlockSpec((nk, tm), lambda i: (0, i)),
        ],
        out_shape=[jax.ShapeDtypeStruct((m, nv), BF16), jax.ShapeDtypeStruct((nk, m), BF16)],
        compiler_params=_cparams(1),
        name="proj1_kv",
    )(x_all, mod, wv_bf, wkt_bf, cos_t, sin_t)


def _flash(qs, kt_ref, v_ref, scratch, tk, stack):
    (qs_ref, s_refs, pm_refs, m_ref, l_ref, acc_ref) = scratch
    rows = qs_ref.shape[0]
    lk = v_ref.shape[0]
    n = lk // tk
    rs = rows // stack
    part = rows // len(qs)
    for a, q_part in enumerate(qs):
        qs_ref[a * part:(a + 1) * part, :] = q_part
    m_ref[...] = jnp.full(m_ref.shape, -jnp.inf, F32)
    l_ref[...] = jnp.zeros(l_ref.shape, F32)
    acc_ref[...] = jnp.zeros(acc_ref.shape, F32)

    n_grp = tk // LANES
    dv = acc_ref.shape[1]
    rb = FLASH_BLOCK if rows % FLASH_BLOCK == 0 else rows
    n_rb = rows // rb
    ps = min(rb, rs)

    def score_rows(j, r, s_ref, pm_ref):
        off = pl.multiple_of(j * tk, tk)
        for piece in range(rb // ps):
            r2 = pl.multiple_of(r + piece * ps, ps)
            ka = 0 if stack == 1 else pl.multiple_of((r2 // rs) * HEAD_DIM, HEAD_DIM)
            s = _dot(qs_ref[pl.ds(r2, ps), :], kt_ref[pl.ds(ka, HEAD_DIM), pl.ds(off, tk)])
            s_ref[pl.ds(r2, ps), :] = s
            pm = s[:, 0:LANES]
            for c in range(1, n_grp):
                pm = jnp.maximum(pm, s[:, c * LANES:(c + 1) * LANES])
            pm_ref[pl.ds(r2, ps), :] = pm

    sb = FLASH_SUB if rb % FLASH_SUB == 0 else rb

    def attend_rows(j, r, s_ref, pm_ref):
        off = pl.multiple_of(j * tk, tk)
        blk = pl.ds(r, sb)
        m_prev = m_ref[blk, :]
        m_next = jnp.maximum(m_prev, jnp.max(pm_ref[blk, :], axis=1, keepdims=True))
        alpha = jnp.exp2(m_prev - m_next)
        m_ref[blk, :] = m_next
        pieces = []
        for q in range(sb // FLASH_ROWS):
            sub = pl.ds(pl.multiple_of(r + q * FLASH_ROWS, FLASH_ROWS), FLASH_ROWS)
            mb = m_next[q * FLASH_ROWS:(q + 1) * FLASH_ROWS]
            lsum = alpha[q * FLASH_ROWS:(q + 1) * FLASH_ROWS] * l_ref[sub, :]
            row = []
            for c in range(n_grp):
                pc = jnp.exp2(s_ref[sub, c * LANES:(c + 1) * LANES] - mb)
                lsum = lsum + pc
                row.append(pc.astype(BF16))
            l_ref[sub, :] = lsum
            pieces.append(jnp.concatenate(row, axis=1))
        pv = _dot(jnp.concatenate(pieces, axis=0), v_ref[pl.ds(off, tk), :])
        for c in range(dv // LANES):
            cols = slice(c * LANES, (c + 1) * LANES)
            acc_ref[blk, cols] = alpha * acc_ref[blk, cols] + pv[:, cols]

    def step_parity(j, cur, with_scores):
        nxt = 1 - cur

        def body(i, carry):
            r = pl.multiple_of(i * rb, rb)
            if with_scores:
                score_rows(j + 1, r, s_refs[nxt], pm_refs[nxt])
            for h in range(rb // sb):
                attend_rows(j, pl.multiple_of(r + h * sb, sb), s_refs[cur], pm_refs[cur])
            return carry

        lax.fori_loop(0, n_rb, body, 0)

    def first_scores(i, carry):
        score_rows(0, pl.multiple_of(i * rb, rb), s_refs[0], pm_refs[0])
        return carry

    lax.fori_loop(0, n_rb, first_scores, 0)

    def pair(i, carry):
        j = 2 * i
        step_parity(j, 0, True)
        step_parity(j + 1, 1, True)
        return carry

    lax.fori_loop(0, (n - 1) // 2, pair, 0)
    if (n - 1) % 2:
        step_parity(n - 2, (n - 2) % 2, True)
    step_parity(n - 1, (n - 1) % 2, False)
    return acc_ref[...] / jnp.sum(l_ref[...], axis=1, keepdims=True)


def _flash_scratch(rows, tk, dv):
    stat = pltpu.VMEM((rows, LANES), F32)
    return [pltpu.VMEM((rows, HEAD_DIM), BF16),
            pltpu.VMEM((rows, tk), F32), pltpu.VMEM((rows, tk), F32), stat, stat,
            stat, stat, pltpu.VMEM((rows, dv), F32)]


def _unpack_flash_scratch(refs):
    qs_ref, s0, s1, pm0, pm1, m_ref, l_ref, acc_ref = refs
    return (qs_ref, (s0, s1), (pm0, pm1), m_ref, l_ref, acc_ref)


def _gqa_kernel(q_ref, kt_ref, v_ref, o_ref, *scratch, tk):
    tq = q_ref.shape[0]
    qs = [q_ref[:, a * HEAD_DIM:(a + 1) * HEAD_DIM] for a in range(GQA_GROUP)]
    o = _flash(qs, kt_ref, v_ref, _unpack_flash_scratch(scratch), tk, 1)
    for a in range(GQA_GROUP):
        o_ref[:, a * HEAD_DIM:(a + 1) * HEAD_DIM] = o[a * tq:(a + 1) * tq].astype(BF16)


def _diff_kernel(lam_ref, q_ref, kt_ref, v_ref, g_ref, o_ref, *scratch, tk, out_scale):
    tq = q_ref.shape[0]
    qs = [q_ref[:, 0:HEAD_DIM], q_ref[:, HEAD_DIM:2 * HEAD_DIM]]
    o = _flash(qs, kt_ref, v_ref, _unpack_flash_scratch(scratch), tk, 2)
    od = o[0:tq] - lam_ref[0] * o[tq:2 * tq]
    od = od * lax.rsqrt(jnp.mean(od * od, axis=-1, keepdims=True) + LN_EPS)
    o_ref[...] = (od * g_ref[...] * out_scale).astype(BF16)


def _key_chunk(lk):
    for tk in (1280, 1024, 640, 512, 256, 128):
        if lk % tk == 0 and lk // tk >= 2:
            return tk
    return lk


def _attention(q1, kt, v1, lam, subln_g, lambda_init, n_lat):
    lk = v1.shape[0]
    tk = _key_chunk(lk)
    tq = 256
    gw = GQA_GROUP * HEAD_DIM
    n_gv = GQA_KV_HEADS * HEAD_DIM
    once = pl.Buffered(1)
    og = pl.pallas_call(
        functools.partial(_gqa_kernel, tk=tk),
        grid=(GQA_KV_HEADS, n_lat // tq),
        in_specs=[
            pl.BlockSpec((tq, gw), lambda g, i: (i, g)),
            pl.BlockSpec((HEAD_DIM, lk), lambda g, i: (g, 0), pipeline_mode=once),
            pl.BlockSpec((lk, HEAD_DIM), lambda g, i: (0, g), pipeline_mode=once),
        ],
        out_specs=pl.BlockSpec((tq, gw), lambda g, i: (i, g)),
        out_shape=jax.ShapeDtypeStruct((n_lat, GQA_HEADS * HEAD_DIM), BF16),
        scratch_shapes=_flash_scratch(GQA_GROUP * tq, tk, HEAD_DIM),
        compiler_params=_cparams(2),
        name="gqa_attention",
    )(q1, kt, v1)
    q_off = GQA_HEADS * HEAD_DIM // DIFF_V_DIM
    kv_off = n_gv // DIFF_V_DIM
    tq = 512 if n_lat % 512 == 0 else 256
    od = pl.pallas_call(
        functools.partial(_diff_kernel, tk=tk, out_scale=1.0 - lambda_init),
        grid=(DIFF_HEADS, n_lat // tq),
        in_specs=[
            pl.BlockSpec(memory_space=pltpu.SMEM),
            pl.BlockSpec((tq, DIFF_V_DIM), lambda h, i: (i, q_off + h)),
            pl.BlockSpec((DIFF_V_DIM, lk), lambda h, i: (kv_off + h, 0), pipeline_mode=once),
            pl.BlockSpec((lk, DIFF_V_DIM), lambda h, i: (0, kv_off + h), pipeline_mode=once),
            pl.BlockSpec((1, DIFF_V_DIM), lambda h, i: (0, 0)),
        ],
        out_specs=pl.BlockSpec((tq, DIFF_V_DIM), lambda h, i: (i, h)),
        out_shape=jax.ShapeDtypeStruct((n_lat, DIFF_HEADS * DIFF_V_DIM), BF16),
        scratch_shapes=_flash_scratch(2 * tq, tk, DIFF_V_DIM),
        compiler_params=_cparams(2),
        name="diff_attention",
    )(lam, q1, kt, v1, subln_g.reshape(1, DIFF_V_DIM).astype(F32))
    return og, od


def _router_operand(router_w):
    return router_w.astype(F32).T


def kernel(x, c, ctx, c_ctx, mod_w, mod_b, ln_g, ln_b, ev_w_in, ev_w_out, ret_log_decay, s5_a_re, s5_a_im, s5_log_dt, s5_b_re, s5_b_im, s5_c_re, s5_c_im, s5_d, s5_w_glu, s5_b_glu, od_w_in, od_w_out, qk_norm_g, diff_lambda, diff_subln_g, router_w, router_bias, exp_w_gate, exp_w_up, exp_w_down, sh_w_gate, sh_w_up, sh_w_down):
    b_, n_lat, d = x.shape
    n_ctx = ctx.shape[1]
    assert b_ == 1 and d == D_MODEL and n_lat % RET_CHUNK == 0 and n_ctx % RET_CHUNK == 0
    m_all = n_lat + n_ctx
    x_all = jnp.concatenate([x[0], ctx[0]], axis=0)
    mods = _modulation(c, c_ctx, mod_w, mod_b)

    cos0, sin0 = _rope_tables(n_lat, n_ctx, RET_DIM)
    p0 = _proj0(x_all, mods[0], ev_w_in[0].astype(BF16), cos0, sin0, n_lat)
    rn = _retention(p0, ret_log_decay[0], n_lat, n_ctx)
    s5_tabs = functools.partial(_s5_tables, s5_a_re[0], s5_a_im[0], s5_log_dt[0], s5_b_re[0], s5_b_im[0],
                                s5_c_re[0], s5_c_im[0], s5_d[0])
    s5y = _s5(p0, 4 * RET_HEADS * RET_DIM, s5_tabs, n_lat, n_ctx)
    ln0 = jnp.stack([ln_g[0], ln_b[0]], axis=1)
    x_all, xp, logits = _finish0(rn, p0, s5y, x_all, mods[0], ln0[0], s5_w_glu[0].astype(BF16),
                                 s5_b_glu[0].reshape(1, -1).astype(F32), ev_w_out[0].astype(BF16),
                                 _router_operand(router_w[0]), n_lat)
    x_all = _moe(xp, logits, x_all, mods[0], ln0[1], router_bias[0], exp_w_gate, exp_w_up, exp_w_down, 0,
                 sh_w_gate[0], sh_w_up[0], sh_w_down[0], n_lat)

    i = 1
    lambda_init = 0.8 - 0.6 * math.exp(-0.3 * i)
    gq_w = GQA_HEADS * HEAD_DIM
    gk_w = GQA_KV_HEADS * HEAD_DIM
    dq_w = DIFF_HEADS * 2 * HEAD_DIM
    cuts = [gq_w, gq_w + gk_w, gq_w + 2 * gk_w, gq_w + 2 * gk_w + dq_w, gq_w + 2 * gk_w + 2 * dq_w]
    w_in = od_w_in[0]
    w_gq, w_gk, w_gv, w_dq, w_dk, w_dv = (w_in[:, a:b] for a, b in zip([0] + cuts, cuts + [w_in.shape[1]]))
    cos1, sin1 = _rope_tables(n_lat, n_ctx, HEAD_DIM)
    qscale = HEAD_DIM ** -0.5 * LOG2E
    gq_gain = qk_norm_g[0, 0].astype(F32)
    gk_gain = qk_norm_g[0, 1].astype(F32)
    quarter = HEAD_DIM // 4

    def partner_gain(g):
        return jnp.concatenate([g[quarter:2 * quarter], g[0:quarter], g[3 * quarter:], g[2 * quarter:3 * quarter]])

    cos_q = jnp.stack([cos1[:n_lat] * gq_gain[None, :], cos1[:n_lat]]) * qscale
    sin_q = jnp.stack([sin1[:n_lat] * partner_gain(gq_gain)[None, :], sin1[:n_lat]]) * qscale
    cos_k = jnp.stack([cos1.T * gk_gain[:, None], cos1.T])
    sin_k = jnp.stack([sin1.T * partner_gain(gk_gain)[:, None], sin1.T])
    q1 = _projq(x_all, mods[1], jnp.concatenate([w_gq, w_dq], axis=1).astype(BF16), cos_q, sin_q, n_lat)
    v1, kt = _projkv(x_all, mods[1], jnp.concatenate([w_gv, w_dv], axis=1).astype(BF16),
                     jnp.concatenate([w_gk, w_dk], axis=1).T.astype(BF16), cos_k, sin_k, n_lat)
    lf = diff_lambda[0].astype(F32)
    lam = (jnp.exp(jnp.sum(lf[0] * lf[1])) - jnp.exp(jnp.sum(lf[2] * lf[3])) + lambda_init).reshape(1)
    og, od = _attention(q1, kt, v1, lam, diff_subln_g[0], lambda_init, n_lat)
    ln1 = jnp.stack([ln_g[1], ln_b[1]], axis=1)
    x_lat, xp, logits = _finish1(og, od, x_all, mods[1], ln1[0], od_w_out[0].astype(BF16),
                                 _router_operand(router_w[1]), n_lat)
    x_lat = _moe(xp, logits, x_lat, mods[1], ln1[1], router_bias[1], exp_w_gate, exp_w_up, exp_w_down, 1,
                 sh_w_gate[1], sh_w_up[1], sh_w_down[1], n_lat)
    return x_lat[None]
```

```python
import functools
import math

import jax
import jax.numpy as jnp
from jax import lax
from jax.experimental import pallas as pl
from jax.experimental.pallas import tpu as pltpu

F32 = jnp.float32
BF16 = jnp.bfloat16
U32 = jnp.uint32
I32 = jnp.int32

D_MODEL = 2048
DEPTH = 2
GRID_W = 64
ROPE_BASE = 10000.0
LN_EPS = 1e-6
DEEPNORM_ALPHA = (2 * DEPTH) ** 0.25
RET_HEADS = 4
RET_DIM = 256
S5_CHANNELS = D_MODEL // 2
S5_GROUP = 16
S5_GROUPS = S5_CHANNELS // S5_GROUP
S5_STATE = 64
HEAD_DIM = 128
GQA_HEADS = 8
GQA_KV_HEADS = 2
GQA_GROUP = GQA_HEADS // GQA_KV_HEADS
DIFF_HEADS = 4
DIFF_V_DIM = 2 * HEAD_DIM
N_EXPERTS = 64
TOP_K = 8
N_EXPERT_GROUPS = 8
TOPK_GROUPS = 4
EXPERT_HIDDEN = 512
ROUTED_SCALE = 2.5
LOG2E = 1.4426950408889634

LANES = 128
VMEM_LIMIT = 56 * 1024 * 1024

RET_CHUNK = 256
S5_STEP = 16
MOE_TM = 256
ROW_TILE = 256
COMBINE_CHUNK = 16
COMBINE_ROWS = ROW_TILE * TOP_K + 2 * N_EXPERTS * COMBINE_CHUNK
COMBINE_PIECE = 512
FLASH_ROWS = 64
FLASH_BLOCK = 1024
FLASH_SUB = 256


def _cparams(n_axes):
    return pltpu.CompilerParams(dimension_semantics=("arbitrary",) * n_axes, vmem_limit_bytes=VMEM_LIMIT)


def _dot(a, b):
    return jnp.dot(a, b, preferred_element_type=F32)


def _dot_nt(a, b):
    return lax.dot_general(a, b, (((1,), (1,)), ((), ())), preferred_element_type=F32)


def _dot_tn(a, b):
    return lax.dot_general(a, b, (((0,), (0,)), ((), ())), preferred_element_type=F32)


def _split(a):
    hi = a.astype(BF16)
    lo = (a - hi.astype(F32)).astype(BF16)
    return hi, lo


def _dot3(a, b):
    ah, al = _split(a)
    bh, bl = _split(b)
    return _dot(ah, bh) + _dot(al, bh) + _dot(ah, bl)


def _sigmoid(x):
    return 1.0 / (1.0 + jnp.exp(-x))


def _silu(x):
    return x * _sigmoid(x)


def _gelu_tanh(x):
    return 0.5 * x * (1.0 + jnp.tanh(math.sqrt(2.0 / math.pi) * (x + 0.044715 * (x * x * x))))


def _layer_norm(h, g, b):
    mu = jnp.mean(h, axis=-1, keepdims=True)
    d = h - mu
    var = jnp.mean(d * d, axis=-1, keepdims=True)
    return d * lax.rsqrt(var + LN_EPS) * g + b


def _pack_bf16_pairs(x):
    n = x.shape[1] // 2
    lo = lax.bitcast_convert_type(x[:, :n].astype(BF16).astype(F32), U32)
    hi = lax.bitcast_convert_type(x[:, n:].astype(BF16).astype(F32), U32)
    return (lo >> 16) | (hi & jnp.uint32(0xFFFF0000))


def _unpack_bf16_pairs(w):
    lo = lax.bitcast_convert_type(w << 16, F32)
    hi = lax.bitcast_convert_type(w & jnp.uint32(0xFFFF0000), F32)
    return lo, hi


def _modulate_rows(x, mod_ref, shift_row, scale_row, row0, n_lat):
    rows = row0 + lax.broadcasted_iota(I32, (x.shape[0], 1), 0)
    is_ctx = rows >= n_lat
    sc = jnp.where(is_ctx, mod_ref[1, scale_row:scale_row + 1, :], mod_ref[0, scale_row:scale_row + 1, :])
    sh = jnp.where(is_ctx, mod_ref[1, shift_row:shift_row + 1, :], mod_ref[0, shift_row:shift_row + 1, :])
    return x * (1.0 + sc) + sh


def _select_rows(mod_ref, row, row0, n_rows, n_lat):
    rows = row0 + lax.broadcasted_iota(I32, (n_rows, 1), 0)
    return jnp.where(rows >= n_lat, mod_ref[1, row:row + 1, :], mod_ref[0, row:row + 1, :])


def _mod_kernel(v_ref, w_ref, b_ref, o_ref):
    v = v_ref[...]
    o_ref[0] = _dot3(_silu(v), w_ref[0]) + b_ref[0]


def _modulation(c, c_ctx, mod_w, mod_b):
    depth, d, n = mod_w.shape
    v = jnp.concatenate([c[:1], c_ctx[None], jnp.zeros((6, d), F32)], axis=0)
    tn = 512
    out = pl.pallas_call(
        _mod_kernel,
        grid=(depth, n // tn),
        in_specs=[
            pl.BlockSpec((8, d), lambda i, j: (0, 0)),
            pl.BlockSpec((1, d, tn), lambda i, j: (i, 0, j)),
            pl.BlockSpec((1, 1, tn), lambda i, j: (i, 0, j)),
        ],
        out_specs=pl.BlockSpec((1, 8, tn), lambda i, j: (i, 0, j)),
        out_shape=jax.ShapeDtypeStruct((depth, 8, n), F32),
        compiler_params=_cparams(2),
        name="modulation",
    )(v, mod_w, mod_b.reshape(depth, 1, n))
    return out[:, :2].reshape(depth, 2, 6, d)


def _rope_tables(n_lat, n_ctx, head_dim):
    q = head_dim // 4
    t = jnp.arange(n_lat, dtype=I32)
    row = (t // GRID_W).astype(F32)
    col = (t % GRID_W).astype(F32)
    freqs = ROPE_BASE ** (-jnp.arange(q, dtype=F32) / q)
    ar = row[:, None] * freqs[None, :]
    ac = col[:, None] * freqs[None, :]
    cos = jnp.concatenate([jnp.cos(ar), jnp.cos(ar), jnp.cos(ac), jnp.cos(ac)], axis=-1)
    sin = jnp.concatenate([-jnp.sin(ar), jnp.sin(ar), -jnp.sin(ac), jnp.sin(ac)], axis=-1)
    cos = jnp.concatenate([cos, jnp.ones((n_ctx, head_dim), F32)], axis=0)
    sin = jnp.concatenate([sin, jnp.zeros((n_ctx, head_dim), F32)], axis=0)
    return cos, sin


def _proj0_kernel(x_ref, mod_ref, w_ref, cos_ref, sin_ref, o_ref, xm_ref, *, tm, n_lat):
    i = pl.program_id(0)
    j = pl.program_id(1)

    @pl.when(j == 0)
    def _():
        xm_ref[...] = _modulate_rows(x_ref[...], mod_ref, 0, 1, i * tm, n_lat).astype(BF16)

    acc = _dot(xm_ref[...], w_ref[...])

    @pl.when(j >= 2)
    def _():
        o_ref[...] = acc.astype(BF16)

    @pl.when(j < 2)
    def _():
        scale = jnp.where(j == 0, RET_DIM ** -0.5, 1.0).astype(F32)
        for c in range(acc.shape[1] // LANES):
            xs = acc[:, c * LANES:(c + 1) * LANES]
            t0 = (c % 2) * LANES
            rot = xs * cos_ref[:, t0:t0 + LANES] + pltpu.roll(xs, LANES // 2, 1) * sin_ref[:, t0:t0 + LANES]
            o_ref[:, c * LANES:(c + 1) * LANES] = (rot * scale).astype(BF16)


def _proj0(x_all, mod, w_bf, cos, sin, n_lat):
    m, d = x_all.shape
    n = w_bf.shape[1]
    tm = 640 if m % 640 == 0 else 256
    tn = 1024
    return pl.pallas_call(
        functools.partial(_proj0_kernel, tm=tm, n_lat=n_lat),
        grid=(m // tm, n // tn),
        in_specs=[
            pl.BlockSpec((tm, d), lambda i, j: (i, 0)),
            pl.BlockSpec((2, 6, d), lambda i, j: (0, 0, 0)),
            pl.BlockSpec((d, tn), lambda i, j: (0, j)),
            pl.BlockSpec((tm, RET_DIM), lambda i, j: (i, 0)),
            pl.BlockSpec((tm, RET_DIM), lambda i, j: (i, 0)),
        ],
        out_specs=pl.BlockSpec((tm, tn), lambda i, j: (i, j)),
        out_shape=jax.ShapeDtypeStruct((m, n), BF16),
        scratch_shapes=[pltpu.VMEM((tm, d), BF16)],
        compiler_params=_cparams(2),
        name="proj0",
    )(x_all, mod, w_bf, cos, sin)


def _ret_tables(log_decay):
    c = RET_CHUNK
    lg = log_decay.astype(F32)
    idx = jnp.arange(c, dtype=F32)
    diff = idx[:, None] - idx[None, :]
    lower = jnp.where(diff >= 0, jnp.exp(jnp.maximum(diff, 0.0)[None] * lg[0][:, None, None]), 0.0)
    upper = jnp.where(diff <= 0, jnp.exp(jnp.maximum(-diff, 0.0)[None] * lg[1][:, None, None]), 0.0)
    mask = lower + upper
    ones = jnp.ones((1, 1, RET_DIM), F32)
    kdec_f = jnp.exp((c - 1 - idx)[None, :] * lg[0][:, None])[:, :, None] * ones
    qdec_f = jnp.exp((idx + 1)[None, :] * lg[0][:, None])[:, :, None] * ones
    kdec_b = jnp.exp(idx[None, :] * lg[1][:, None])[:, :, None] * ones
    qdec_b = jnp.exp((c - idx)[None, :] * lg[1][:, None])[:, :, None] * ones
    g_chunk = jnp.exp(c * lg)
    return mask, kdec_f, qdec_f, kdec_b, qdec_b, g_chunk


def _ret_bwd_state_kernel(gc_ref, k_ref, v_ref, kdec_ref, sb_ref, s_ref):
    h = pl.program_id(0)
    s = pl.program_id(1)

    @pl.when(s == 0)
    def _():
        s_ref[...] = jnp.zeros_like(s_ref)

    sb_ref[0, 0] = s_ref[...].astype(BF16)
    kd = (k_ref[...].astype(F32) * kdec_ref[0]).astype(BF16)
    s_ref[...] = gc_ref[1, h] * s_ref[...] + _dot_tn(kd, v_ref[...])


def _ret_out_kernel(gc_ref, q_ref, k_ref, v_ref, sb_ref, mask_ref, qdf_ref, qdb_ref, kdf_ref, o_ref, s_ref):
    h = pl.program_id(0)
    s = pl.program_id(1)

    @pl.when(s == 0)
    def _():
        s_ref[...] = jnp.zeros_like(s_ref)

    q = q_ref[...]
    k = k_ref[...]
    v = v_ref[...]
    qf = q.astype(F32)
    w = (_dot_nt(q, k) * mask_ref[0]).astype(BF16)
    o = _dot(w, v)
    o = o + _dot((qf * qdf_ref[0]).astype(BF16), s_ref[...].astype(BF16))
    o = o + _dot((qf * qdb_ref[0]).astype(BF16), sb_ref[0, 0])
    mu = jnp.mean(o, axis=-1, keepdims=True)
    d = o - mu
    var = jnp.mean(d * d, axis=-1, keepdims=True)
    o_ref[...] = (d * lax.rsqrt(var + LN_EPS)).astype(BF16)
    kd = (k.astype(F32) * kdf_ref[0]).astype(BF16)
    s_ref[...] = gc_ref[0, h] * s_ref[...] + _dot_tn(kd, v)


def _retention(p0, log_decay, n_lat, n_ctx):
    m = p0.shape[0]
    c = RET_CHUNK
    nc = m // c
    nlc = n_lat // c
    hh = RET_HEADS
    mask, kdec_f, qdec_f, kdec_b, qdec_b, g_chunk = _ret_tables(log_decay)
    smem = pl.BlockSpec(memory_space=pltpu.SMEM)

    def bchunk(s):
        return nc - 1 - s

    sb = pl.pallas_call(
        _ret_bwd_state_kernel,
        grid=(hh, nc),
        in_specs=[
            smem,
            pl.BlockSpec((c, RET_DIM), lambda h, s: (bchunk(s), hh + h)),
            pl.BlockSpec((c, RET_DIM), lambda h, s: (bchunk(s), 2 * hh + h)),
            pl.BlockSpec((1, c, RET_DIM), lambda h, s: (h, 0, 0)),
        ],
        out_specs=pl.BlockSpec((1, 1, RET_DIM, RET_DIM), lambda h, s: (h, bchunk(s), 0, 0)),
        out_shape=jax.ShapeDtypeStruct((hh, nc, RET_DIM, RET_DIM), BF16),
        scratch_shapes=[pltpu.VMEM((RET_DIM, RET_DIM), F32)],
        compiler_params=_cparams(2),
        name="ret_bwd_state",
    )(g_chunk, p0, p0, kdec_b)

    def fchunk(s):
        return (s + nlc) % nc

    return pl.pallas_call(
        _ret_out_kernel,
        grid=(hh, nc),
        in_specs=[
            smem,
            pl.BlockSpec((c, RET_DIM), lambda h, s: (fchunk(s), h)),
            pl.BlockSpec((c, RET_DIM), lambda h, s: (fchunk(s), hh + h)),
            pl.BlockSpec((c, RET_DIM), lambda h, s: (fchunk(s), 2 * hh + h)),
            pl.BlockSpec((1, 1, RET_DIM, RET_DIM), lambda h, s: (h, fchunk(s), 0, 0)),
            pl.BlockSpec((1, c, c), lambda h, s: (h, 0, 0)),
            pl.BlockSpec((1, c, RET_DIM), lambda h, s: (h, 0, 0)),
            pl.BlockSpec((1, c, RET_DIM), lambda h, s: (h, 0, 0)),
            pl.BlockSpec((1, c, RET_DIM), lambda h, s: (h, 0, 0)),
        ],
        out_specs=pl.BlockSpec((c, RET_DIM), lambda h, s: (fchunk(s), h)),
        out_shape=jax.ShapeDtypeStruct((m, hh * RET_DIM), BF16),
        scratch_shapes=[pltpu.VMEM((RET_DIM, RET_DIM), F32)],
        compiler_params=_cparams(2),
        name="ret_out",
    )(g_chunk, p0, p0, p0, sb, mask, qdec_f, qdec_b, kdec_f)


def _s5_tables(a_re, a_im, log_dt, b_re, b_im, c_re, c_im, d_skip, n_scan):
    f32 = F32
    st = S5_STEP
    g_, p_, s_ = S5_GROUPS, S5_STATE, S5_GROUP
    b_re, b_im, c_re, c_im = (t.astype(f32) for t in (b_re, b_im, c_re, c_im))
    ws, vs, aps, ks = [], [], [], []
    for direction in range(2):
        are, aim = a_re[direction].astype(f32), a_im[direction].astype(f32)
        dt = jnp.exp(log_dt[direction].astype(f32))[:, None]
        zr, zi = are * dt, aim * dt
        mag = jnp.exp(zr)
        ab_re, ab_im = mag * jnp.cos(zi), mag * jnp.sin(zi)
        den = jnp.square(are) + jnp.square(aim)
        nr, ni = ab_re - 1.0, ab_im
        f_re = (nr * are + ni * aim) / den
        f_im = (ni * are - nr * aim) / den
        bb_re = f_re[..., None] * b_re - f_im[..., None] * b_im
        bb_im = f_re[..., None] * b_im + f_im[..., None] * b_re
        pr, pi = [jnp.ones_like(ab_re)], [jnp.zeros_like(ab_im)]
        for _ in range(st):
            pr.append(pr[-1] * ab_re - pi[-1] * ab_im)
            pi.append(pr[-2] * ab_im + pi[-1] * ab_re)
        pw_re, pw_im = jnp.stack(pr), jnp.stack(pi)
        ca_re = c_re[None] * pw_re[:, :, None, :] - c_im[None] * pw_im[:, :, None, :]
        ca_im = c_re[None] * pw_im[:, :, None, :] + c_im[None] * pw_re[:, :, None, :]
        kk = (jnp.einsum('tgip,gpj->tgij', ca_re[:st], bb_re, precision='highest')
              - jnp.einsum('tgip,gpj->tgij', ca_im[:st], bb_im, precision='highest'))
        ks.append(kk)
        e = (st - 1 - jnp.arange(st)) if direction == 0 else jnp.arange(st)
        w_re = pw_re[e][:, :, :, None] * bb_re[None] - pw_im[e][:, :, :, None] * bb_im[None]
        w_im = pw_re[e][:, :, :, None] * bb_im[None] + pw_im[e][:, :, :, None] * bb_re[None]
        w = jnp.concatenate([w_re, w_im], axis=2)
        ws.append(jnp.transpose(w, (1, 0, 3, 2)).reshape(g_, st * s_, 2 * p_))
        e2 = (jnp.arange(st) + 1) if direction == 0 else (st - jnp.arange(st))
        v = jnp.concatenate([ca_re[e2], -ca_im[e2]], axis=3)
        vs.append(jnp.transpose(v, (1, 3, 0, 2)).reshape(g_, 2 * p_, st * s_))
        qr, qi = pw_re[st], pw_im[st]
        rows = []
        for _ in range(n_scan):
            rows.append(jnp.stack([jnp.concatenate([qr, qr], -1), jnp.concatenate([-qi, qi], -1)], axis=1))
            qr, qi = qr * qr - qi * qi, 2.0 * qr * qi
        aps.append(jnp.stack(rows, axis=1))
    tt = jnp.arange(st)
    lag = tt[None, :] - tt[:, None]
    kf = ks[0][jnp.clip(lag, 0, st - 1)]
    kb = ks[1][jnp.clip(-lag, 0, st - 1)]
    tm = jnp.where((lag >= 0)[:, :, None, None, None], kf, 0.0) + jnp.where((lag <= 0)[:, :, None, None, None], kb, 0.0)
    tm = jnp.transpose(tm, (2, 0, 4, 1, 3)).reshape(g_, st * s_, st * s_)
    dsk = d_skip.astype(f32).reshape(g_, s_)
    tm = tm + jnp.eye(st * s_, dtype=f32)[None] * jnp.tile(dsk, (1, st))[:, None, :]
    return tm, jnp.stack(ws), jnp.stack(vs), jnp.stack(aps)


def _s5_kernel(u_ref, t_ref, w_ref, v_ref, a_ref, y_ref, uf_ref, ut_ref, yg_ref, ha_ref, hb_ref,
               *, ncs, nls, n_scan):
    st = S5_STEP
    nt = nls + ncs
    groups = LANES // S5_GROUP
    tw = st * S5_GROUP
    shift = S5_GROUP.bit_length() - 1
    lane_shift = LANES.bit_length() - 1
    uf_ref[...] = u_ref[...].astype(F32)
    for t in range(st):
        ut_ref[t // 2, :, (t % 2) * LANES:(t % 2 + 1) * LANES] = uf_ref[pl.ds(t, nt, stride=st), :].astype(BF16)
    src = lax.broadcasted_iota(I32, (2 * LANES, tw), 0)
    pos = lax.broadcasted_iota(I32, (2 * LANES, tw), 1)
    for g in range(groups):
        base = ((src & (LANES - 1)) == g * S5_GROUP + (pos & (S5_GROUP - 1)))
        ug = jnp.zeros((nt, tw), F32)
        for p in range(st // 2):
            sel = jnp.where(base & ((pos >> shift) == 2 * p + (src >> lane_shift)), 1.0, 0.0).astype(BF16)
            ug = ug + _dot(ut_ref[p], sel)
        ug = ug.astype(BF16)
        useq = jnp.concatenate([ug[nls:nt], ug[0:nls], ug[nls:nt]], axis=0)
        yg_ref[g] = _s5_group(useq, t_ref, w_ref, v_ref, a_ref, ha_ref, hb_ref, g, ncs, nls, n_scan).astype(BF16)
    dst = lax.broadcasted_iota(I32, (tw, 2 * LANES), 1)
    pos_t = lax.broadcasted_iota(I32, (tw, 2 * LANES), 0)
    for p in range(st // 2):
        rows2 = jnp.zeros((nt, 2 * LANES), F32)
        for g in range(groups):
            hit = (((dst & (LANES - 1)) == g * S5_GROUP + (pos_t & (S5_GROUP - 1)))
                   & ((pos_t >> shift) == 2 * p + (dst >> lane_shift)))
            rows2 = rows2 + _dot(yg_ref[g], jnp.where(hit, 1.0, 0.0).astype(BF16))
        uf_ref[pl.ds(2 * p, nt, stride=st), :] = rows2[:, 0:LANES]
        uf_ref[pl.ds(2 * p + 1, nt, stride=st), :] = rows2[:, LANES:2 * LANES]
    y_ref[...] = uf_ref[...].astype(BF16)


def _s5_group(u, t_ref, w_ref, v_ref, a_ref, ha_ref, hb_ref, g, ncs, nls, n_scan):
    n = ncs + nls + ncs

    def mm(x, mat):
        mh, ml = _split(mat)
        return _dot(x, mh) + _dot(x, ml)

    def mm3(x, mat):
        xh, xl = _split(x)
        mh, ml = _split(mat)
        return _dot(xh, mh) + _dot(xl, mh) + _dot(xh, ml)

    rows = lax.broadcasted_iota(I32, (n, 1), 0)

    def scan(z, direction):
        bufs = (ha_ref, hb_ref)
        bufs[0][...] = z
        for kk in range(n_scan):
            src, dst = bufs[kk % 2], bufs[(kk + 1) % 2]
            sft = 1 << kk
            a1 = a_ref[direction, g, kk, 0:1, :]
            a2 = a_ref[direction, g, kk, 1:2, :]
            if sft >= n:
                dst[...] = src[...]
                continue
            if sft % 8 == 0:
                if direction == 0:
                    prev = src[0:n - sft, :]
                    dst[0:sft, :] = src[0:sft, :]
                    dst[sft:n, :] = src[sft:n, :] + a1 * prev + a2 * pltpu.roll(prev, S5_STATE, 1)
                else:
                    nxt = src[sft:n, :]
                    dst[n - sft:n, :] = src[n - sft:n, :]
                    dst[0:n - sft, :] = src[0:n - sft, :] + a1 * nxt + a2 * pltpu.roll(nxt, S5_STATE, 1)
            else:
                cur = src[...]
                if direction == 0:
                    sh = jnp.where(rows >= sft, pltpu.roll(cur, sft, 0), 0.0)
                else:
                    sh = jnp.where(rows < n - sft, pltpu.roll(cur, n - sft, 0), 0.0)
                dst[...] = cur + a1 * sh + a2 * pltpu.roll(sh, S5_STATE, 1)
        return bufs[n_scan % 2][...]

    y = mm(u, t_ref[g])
    hf = scan(mm(u, w_ref[0, g]), 0)
    hf_prev = jnp.where(rows >= 1, pltpu.roll(hf, 1, 0), 0.0)
    yf = mm3(hf_prev, v_ref[0, g])
    hb = scan(mm(u, w_ref[1, g]), 1)
    hb_next = jnp.where(rows < n - 1, pltpu.roll(hb, n - 1, 0), 0.0)
    yb = mm3(hb_next, v_ref[1, g])
    return jnp.concatenate([(y + yf + yb)[ncs:ncs + nls, :], (y + yf)[0:ncs, :] + yb[ncs + nls:n, :]], axis=0)


def _s5(p0, col0, tabs, n_lat, n_ctx):
    m = p0.shape[0]
    st = S5_STEP
    nls, ncs = n_lat // st, n_ctx // st
    n = ncs + nls + ncs
    nt = nls + ncs
    n_scan = max(1, (n - 1).bit_length())
    t_mat, w_mat, v_mat, apow = tabs(n_scan)
    tw = st * S5_GROUP
    gpl = LANES // S5_GROUP
    blk0 = col0 // LANES
    return pl.pallas_call(
        functools.partial(_s5_kernel, ncs=ncs, nls=nls, n_scan=n_scan),
        grid=(S5_CHANNELS // LANES,),
        in_specs=[
            pl.BlockSpec((m, LANES), lambda j: (0, blk0 + j)),
            pl.BlockSpec((gpl, tw, tw), lambda j: (j, 0, 0)),
            pl.BlockSpec((2, gpl, tw, 2 * S5_STATE), lambda j: (0, j, 0, 0)),
            pl.BlockSpec((2, gpl, 2 * S5_STATE, tw), lambda j: (0, j, 0, 0)),
            pl.BlockSpec((2, gpl, n_scan, 2, 2 * S5_STATE), lambda j: (0, j, 0, 0, 0)),
        ],
        out_specs=pl.BlockSpec((m, LANES), lambda j: (0, j)),
        out_shape=jax.ShapeDtypeStruct((m, S5_CHANNELS), BF16),
        scratch_shapes=[pltpu.VMEM((m, LANES), F32), pltpu.VMEM((st // 2, nt, 2 * LANES), BF16),
                        pltpu.VMEM((gpl, nt, tw), BF16),
                        pltpu.VMEM((n, 2 * S5_STATE), F32), pltpu.VMEM((n, 2 * S5_STATE), F32)],
        compiler_params=_cparams(1),
        name="s5",
    )(p0, t_mat, w_mat, v_mat, apow)


def _post_mix(y, x_ref, mod_ref, ln_ref, rw_ref, xo_ref, xp_ref, lg_ref, row0, n_lat):
    tm = y.shape[0]
    g1 = _select_rows(mod_ref, 2, row0, tm, n_lat)
    xn = _layer_norm(DEEPNORM_ALPHA * x_ref[...] + g1 * y, ln_ref[0:1, :], ln_ref[1:2, :])
    xo_ref[...] = xn
    sc2 = _select_rows(mod_ref, 4, row0, tm, n_lat)
    sh2 = _select_rows(mod_ref, 3, row0, tm, n_lat)
    x2 = xn * (1.0 + sc2) + sh2
    xp_ref[...] = _pack_bf16_pairs(x2)
    xh, xl = _split(x2)
    rh, rl = _split(rw_ref[...])
    lg_ref[...] = _dot_nt(rh, xh) + _dot_nt(rh, xl) + _dot_nt(rl, xh)


def _finish0_kernel(r_ref, g_ref, s_ref, x_ref, mod_ref, ln_ref, wglu_ref, bglu_ref, wout_ref, rw_ref,
                    xo_ref, xp_ref, lg_ref, *, tm, n_lat):
    row0 = pl.program_id(0) * tm
    ret = (r_ref[...].astype(F32) * _silu(g_ref[...].astype(F32))).astype(BF16)
    z = _gelu_tanh(s_ref[...].astype(F32))
    zb = z.astype(BF16)
    gate = _sigmoid(_dot(zb, wglu_ref[...]) + bglu_ref[...])
    s5o = (z * gate).astype(BF16)
    half = ret.shape[1]
    y = _dot(ret, wout_ref[0:half, :]) + _dot(s5o, wout_ref[half:, :])
    _post_mix(y, x_ref, mod_ref, ln_ref, rw_ref, xo_ref, xp_ref, lg_ref, row0, n_lat)


def _const_spec(shape):
    nd = len(shape)
    return pl.BlockSpec(shape, lambda i: (0,) * nd)


def _finish0(rn, p0, s5y, x_all, mod, ln, wglu, bglu, wout, rw, n_lat):
    m, d = x_all.shape
    tm = ROW_TILE
    half = d // 2
    outs = pl.pallas_call(
        functools.partial(_finish0_kernel, tm=tm, n_lat=n_lat),
        grid=(m // tm,),
        in_specs=[
            pl.BlockSpec((tm, half), lambda i: (i, 0)),
            pl.BlockSpec((tm, half), lambda i: (i, 3)),
            pl.BlockSpec((tm, half), lambda i: (i, 0)),
            pl.BlockSpec((tm, d), lambda i: (i, 0)),
            _const_spec((2, 6, d)),
            _const_spec((2, d)),
            _const_spec((half, half)),
            _const_spec((1, half)),
            _const_spec((d, d)),
            _const_spec((N_EXPERTS, d)),
        ],
        out_specs=[
            pl.BlockSpec((tm, d), lambda i: (i, 0)),
            pl.BlockSpec((tm, half), lambda i: (i, 0)),
            pl.BlockSpec((N_EXPERTS, tm), lambda i: (0, i)),
        ],
        out_shape=[
            jax.ShapeDtypeStruct((m, d), F32),
            jax.ShapeDtypeStruct((m, half), U32),
            jax.ShapeDtypeStruct((N_EXPERTS, m), F32),
        ],
        compiler_params=_cparams(1),
        name="finish0",
    )(rn, p0, s5y, x_all, mod, ln, wglu, bglu, wout, rw)
    return outs


def _finish1_kernel(og_ref, od_ref, x_ref, mod_ref, ln_ref, wout_ref, rw_ref, xo_ref, xp_ref, lg_ref, *, tm, n_lat):
    row0 = pl.program_id(0) * tm
    half = og_ref.shape[1]
    y = _dot(og_ref[...], wout_ref[0:half, :]) + _dot(od_ref[...], wout_ref[half:, :])
    _post_mix(y, x_ref, mod_ref, ln_ref, rw_ref, xo_ref, xp_ref, lg_ref, row0, n_lat)


def _finish1(og, od, x_all, mod, ln, wout, rw, n_lat):
    d = x_all.shape[1]
    m = og.shape[0]
    tm = ROW_TILE
    half = d // 2
    return pl.pallas_call(
        functools.partial(_finish1_kernel, tm=tm, n_lat=n_lat),
        grid=(m // tm,),
        in_specs=[
            pl.BlockSpec((tm, half), lambda i: (i, 0)),
            pl.BlockSpec((tm, half), lambda i: (i, 0)),
            pl.BlockSpec((tm, d), lambda i: (i, 0)),
            _const_spec((2, 6, d)),
            _const_spec((2, d)),
            _const_spec((d, d)),
            _const_spec((N_EXPERTS, d)),
        ],
        out_specs=[
            pl.BlockSpec((tm, d), lambda i: (i, 0)),
            pl.BlockSpec((tm, half), lambda i: (i, 0)),
            pl.BlockSpec((N_EXPERTS, tm), lambda i: (0, i)),
        ],
        out_shape=[
            jax.ShapeDtypeStruct((m, d), F32),
            jax.ShapeDtypeStruct((m, half), U32),
            jax.ShapeDtypeStruct((N_EXPERTS, m), F32),
        ],
        compiler_params=_cparams(1),
        name="finish1",
    )(og, od, x_all, mod, ln, wout, rw)


def _route_kernel(lg_ref, bias_ref, e_ref, r_ref, w_ref, cnt_ref, base_ref, blk_ref, carry_ref, *, tb):
    ne = N_EXPERTS
    gsz = ne // N_EXPERT_GROUPS
    neg = -jnp.inf

    @pl.when(pl.program_id(0) == 0)
    def _():
        carry_ref[...] = jnp.zeros_like(carry_ref)

    s = _sigmoid(lg_ref[...])
    sel = s + bias_ref[...]
    gs = []
    for g in range(N_EXPERT_GROUPS):
        blk = sel[g * gsz:(g + 1) * gsz, :]
        m1 = jnp.max(blk, axis=0, keepdims=True)
        n_eq = jnp.sum(jnp.where(blk == m1, 1.0, 0.0), axis=0, keepdims=True)
        m2 = jnp.max(jnp.where(blk < m1, blk, neg), axis=0, keepdims=True)
        gs.append(m1 + jnp.where(n_eq >= 2.0, m1, m2))
    masked = []
    for g in range(N_EXPERT_GROUPS):
        ahead = jnp.zeros_like(gs[g])
        for h in range(N_EXPERT_GROUPS):
            if h == g:
                continue
            beats = ((gs[h] > gs[g]) | (gs[h] == gs[g])) if h < g else (gs[h] > gs[g])
            ahead = ahead + jnp.where(beats, 1.0, 0.0)
        masked.append(jnp.where(ahead < float(TOPK_GROUPS), sel[g * gsz:(g + 1) * gsz, :], neg))
    selm = jnp.concatenate(masked, axis=0)
    eid = lax.broadcasted_iota(I32, (ne, 1), 0)
    ahead = jnp.zeros_like(selm)
    for e in range(ne):
        row = selm[e:e + 1, :]
        beats = (row > selm) | ((row == selm) & (eid > e))
        ahead = ahead + jnp.where(beats, 1.0, 0.0)
    chosen = ahead < float(TOP_K)
    member = jnp.where(chosen, 1.0, 0.0)
    ssel = jnp.where(chosen, s, 0.0)
    wd = ssel / jnp.sum(ssel, axis=0, keepdims=True) * ROUTED_SCALE
    mb = member.astype(BF16)
    ti = lax.broadcasted_iota(I32, (tb, tb), 0)
    tj = lax.broadcasted_iota(I32, (tb, tb), 1)
    tri = jnp.where(ti < tj, 1.0, 0.0).astype(BF16)
    rank = carry_ref[:, 0:1] + _dot(mb, tri)
    in_block = jnp.sum(member, axis=1, keepdims=True)
    base_ref[0] = carry_ref[...]
    blk_ref[0] = jnp.zeros_like(carry_ref) + in_block
    carry_ref[...] = carry_ref[...] + in_block
    cnt_ref[...] = carry_ref[...]
    ei = lax.broadcasted_iota(I32, (ne, ne), 0)
    ej = lax.broadcasted_iota(I32, (ne, ne), 1)
    low = jnp.where(ej < ei, 1.0, 0.0).astype(BF16)
    slot = _dot(low, mb)
    eidf = eid.astype(F32)
    for k in range(TOP_K):
        hit = chosen & (slot == float(k))
        e_ref[k:k + 1, :] = jnp.sum(jnp.where(hit, eidf, 0.0), axis=0, keepdims=True).astype(I32)
        r_ref[k:k + 1, :] = jnp.sum(jnp.where(hit, rank, 0.0), axis=0, keepdims=True).astype(I32)
        w_ref[k:k + 1, :] = jnp.sum(jnp.where(hit, wd, 0.0), axis=0, keepdims=True)


def _route(logits_t, router_bias, tm):
    e, t = logits_t.shape
    tb = ROW_TILE
    eidx, rank, wts, counts, base, in_block = pl.pallas_call(
        functools.partial(_route_kernel, tb=tb),
        grid=(t // tb,),
        in_specs=[
            pl.BlockSpec((e, tb), lambda i: (0, i)),
            pl.BlockSpec((e, 1), lambda i: (0, 0)),
        ],
        out_specs=[
            pl.BlockSpec((TOP_K, tb), lambda i: (0, i)),
            pl.BlockSpec((TOP_K, tb), lambda i: (0, i)),
            pl.BlockSpec((TOP_K, tb), lambda i: (0, i)),
            pl.BlockSpec((e, LANES), lambda i: (0, 0)),
            pl.BlockSpec((1, e, LANES), lambda i: (i, 0, 0)),
            pl.BlockSpec((1, e, LANES), lambda i: (i, 0, 0)),
        ],
        out_shape=[
            jax.ShapeDtypeStruct((TOP_K, t), I32),
            jax.ShapeDtypeStruct((TOP_K, t), I32),
            jax.ShapeDtypeStruct((TOP_K, t), F32),
            jax.ShapeDtypeStruct((e, LANES), F32),
            jax.ShapeDtypeStruct((t // tb, e, LANES), F32),
            jax.ShapeDtypeStruct((t // tb, e, LANES), F32),
        ],
        scratch_shapes=[pltpu.VMEM((e, LANES), F32)],
        compiler_params=_cparams(1),
        name="moe_route",
    )(logits_t, router_bias.astype(F32).reshape(e, 1))
    counts = counts[:, 0].astype(I32)
    padded = (counts + tm - 1) // tm * tm
    pad_end = jnp.cumsum(padded)
    pad_start = pad_end - padded
    ids = jnp.arange(e, dtype=I32)
    start_of = jnp.sum(jnp.where(eidx[:, :, None] == ids, pad_start, 0), axis=-1)
    dest = (start_of + rank).T
    n_blocks = (t * TOP_K + e * (tm - 1)) // tm
    blk_start = jnp.arange(n_blocks, dtype=I32) * tm
    blk_expert = jnp.minimum(jnp.sum((blk_start[:, None] >= pad_end[None, :]).astype(I32), axis=1), e - 1)
    n_used = (pad_end[-1] // tm).astype(I32).reshape(1)
    plan = _combine_plan(dest, eidx.T, base[:, :, 0].astype(I32), in_block[:, :, 0].astype(I32), pad_start, tb)
    return dest, wts.T, blk_expert, n_used, n_blocks, pad_end.astype(I32), padded.astype(I32), plan


def _dispatch_kernel(pad_end_ref, padded_ref, dest_ref, x_ref, xs_ref, zero_ref, sem, zsem, *, tb, tm):
    @pl.when(pl.program_id(0) == 0)
    def _():
        zero_ref[...] = jnp.zeros_like(zero_ref)

        def fill(e):
            first = pl.multiple_of(pad_end_ref[e] - tm, tm)
            return pltpu.make_async_copy(zero_ref, xs_ref.at[pl.ds(first, tm)], zsem)

        for e in range(N_EXPERTS):
            @pl.when(padded_ref[e] > 0)
            def _():
                fill(e).start()
        for e in range(N_EXPERTS):
            @pl.when(padded_ref[e] > 0)
            def _():
                fill(e).wait()

    def row_copy(r, k):
        d = dest_ref[0, 0, r * TOP_K + k]
        return pltpu.make_async_copy(x_ref.at[pl.ds(r, 1)], xs_ref.at[pl.ds(d, 1)], sem)

    def start(r, carry):
        for k in range(TOP_K):
            row_copy(r, k).start(priority=k % 2)
        return carry

    def wait(r, carry):
        for k in range(TOP_K):
            row_copy(r, k).wait()
        return carry

    lax.fori_loop(0, tb, start, 0)
    lax.fori_loop(0, tb, wait, 0)


def _dispatch(xp, dest, pad_end, padded, n_pad, tm):
    t, w = xp.shape
    tb = ROW_TILE
    smem = pl.BlockSpec(memory_space=pltpu.SMEM)
    return pl.pallas_call(
        functools.partial(_dispatch_kernel, tb=tb, tm=tm),
        grid=(t // tb,),
        in_specs=[
            smem,
            smem,
            pl.BlockSpec((1, 1, tb * TOP_K), lambda i: (i, 0, 0), memory_space=pltpu.SMEM),
            pl.BlockSpec((tb, w), lambda i: (i, 0)),
        ],
        out_specs=pl.BlockSpec(memory_space=pl.ANY),
        out_shape=jax.ShapeDtypeStruct((n_pad, w), U32),
        scratch_shapes=[pltpu.VMEM((tm, w), U32), pltpu.SemaphoreType.DMA(()), pltpu.SemaphoreType.DMA(())],
        compiler_params=_cparams(1),
        name="moe_dispatch",
    )(pad_end, padded, dest.reshape(t // tb, 1, tb * TOP_K), xp)


def _expert_kernel(be_ref, nu_ref, xs_ref, wg_ref, wu_ref, wd_ref, y_ref, wgb_ref, wub_ref, wdb_ref):
    b = pl.program_id(0)
    e = be_ref[b]
    prev = be_ref[jnp.maximum(b - 1, 0)]

    @pl.when((b == 0) | (e != prev))
    def _():
        wgb_ref[...] = wg_ref[0, 0].astype(BF16)
        wub_ref[...] = wu_ref[0, 0].astype(BF16)
        wdb_ref[...] = wd_ref[0, 0].astype(BF16)

    @pl.when(b < nu_ref[0])
    def _():
        lo, hi = _unpack_bf16_pairs(xs_ref[...])
        lo = lo.astype(BF16)
        hi = hi.astype(BF16)
        half = lo.shape[1]
        gate = _dot(lo, wgb_ref[0:half, :]) + _dot(hi, wgb_ref[half:, :])
        up = _dot(lo, wub_ref[0:half, :]) + _dot(hi, wub_ref[half:, :])
        hid = (_silu(gate) * up).astype(BF16)
        y_ref[...] = _dot(hid, wdb_ref[...]).astype(BF16)

    @pl.when(b >= nu_ref[0])
    def _():
        y_ref[...] = jnp.zeros_like(y_ref)


def _experts(xs, blk_expert, n_used, w_gate, w_up, w_down, layer):
    n_pad, w = xs.shape
    tm = MOE_TM
    d, hdn = w_gate.shape[2], w_gate.shape[3]
    return pl.pallas_call(
        _expert_kernel,
        grid_spec=pltpu.PrefetchScalarGridSpec(
            num_scalar_prefetch=2,
            grid=(n_pad // tm,),
            in_specs=[
                pl.BlockSpec((tm, w), lambda b, be, nu: (b, 0)),
                pl.BlockSpec((1, 1, d, hdn), lambda b, be, nu: (layer, be[b], 0, 0)),
                pl.BlockSpec((1, 1, d, hdn), lambda b, be, nu: (layer, be[b], 0, 0)),
                pl.BlockSpec((1, 1, hdn, d), lambda b, be, nu: (layer, be[b], 0, 0)),
            ],
            out_specs=pl.BlockSpec((tm, d), lambda b, be, nu: (b, 0)),
            scratch_shapes=[pltpu.VMEM((d, hdn), BF16), pltpu.VMEM((d, hdn), BF16), pltpu.VMEM((hdn, d), BF16)],
        ),
        out_shape=jax.ShapeDtypeStruct((n_pad, d), BF16),
        compiler_params=_cparams(1),
        name="moe_experts",
    )(blk_expert, n_used, xs, w_gate, w_up, w_down)


def _combine_kernel(tot_ref, src_ref, xp_ref, ws_ref, col_ref, x_ref, mod_ref, ln_ref, sg_ref, su_ref, sd_ref,
                    y_hbm, xo_ref, rbuf, sem, *, tb, n_lat):
    i = pl.program_id(0)
    row0 = i * tb
    ch = COMBINE_CHUNK
    n_rows = rbuf.shape[0]

    @pl.when(i == 0)
    def _():
        rbuf[...] = jnp.zeros_like(rbuf)

    total = tot_ref[i]

    def chunk_copy(c):
        first = pl.multiple_of(src_ref[0, 0, c] * ch, ch)
        return pltpu.make_async_copy(y_hbm.at[pl.ds(first, ch)], rbuf.at[pl.ds(pl.multiple_of(c * ch, ch), ch)], sem)

    def start(c, carry):
        chunk_copy(c).start()
        return carry

    def wait(c, carry):
        chunk_copy(c).wait()
        return carry

    lax.fori_loop(0, total, start, 0)
    lo, hi = _unpack_bf16_pairs(xp_ref[...])
    lo = lo.astype(BF16)
    hi = hi.astype(BF16)
    half = lo.shape[1]
    gate = _dot(lo, sg_ref[0:half, :]) + _dot(hi, sg_ref[half:, :])
    up = _dot(lo, su_ref[0:half, :]) + _dot(hi, su_ref[half:, :])
    f = _dot((_silu(gate) * up).astype(BF16), sd_ref[...])
    lax.fori_loop(0, total, wait, 0)
    cols = col_ref[...]
    ws = ws_ref[...]
    piece = COMBINE_PIECE
    for p in range(n_rows // piece):
        pos = p * piece + lax.broadcasted_iota(I32, (tb, piece), 1)
        g = jnp.zeros((tb, piece), F32)
        for k in range(TOP_K):
            g = jnp.where(pos == cols[:, k:k + 1], ws[:, k:k + 1], g)
        f = f + _dot(g.astype(BF16), rbuf[p * piece:(p + 1) * piece, :])
    g2 = _select_rows(mod_ref, 5, row0, tb, n_lat)
    xo_ref[...] = _layer_norm(DEEPNORM_ALPHA * x_ref[...] + g2 * f, ln_ref[0:1, :], ln_ref[1:2, :])


def _combine_plan(dest, eidx, base, cnt, pad_start, tb):
    t, k = dest.shape
    e = pad_start.shape[0]
    ch = COMBINE_CHUNK
    nb = t // tb
    cmax = COMBINE_ROWS // ch
    run0 = pad_start[None, :] + base
    first = run0 // ch
    last = (run0 + cnt - 1) // ch
    nch = jnp.where(cnt > 0, last - first + 1, 0)
    end = jnp.cumsum(nch, axis=1)
    off = end - nch
    total = end[:, -1].astype(I32)
    c = jnp.arange(cmax, dtype=I32)
    e_of_c = jnp.minimum(jnp.sum((end[:, None, :] <= c[None, :, None]).astype(I32), axis=-1), e - 1)
    onehot_c = e_of_c[:, :, None] == jnp.arange(e, dtype=I32)
    src = jnp.sum(jnp.where(onehot_c, (first - off)[:, None, :], 0), axis=-1) + c[None, :]
    src = jnp.where(c[None, :] < total[:, None], src, 0).astype(I32)
    onehot_t = eidx.reshape(nb, tb, k)[..., None] == jnp.arange(e, dtype=I32)
    shift = jnp.sum(jnp.where(onehot_t, ((off - first) * ch)[:, None, None, :], 0), axis=-1)
    col = (dest.reshape(nb, tb, k) + shift).reshape(t, k).astype(I32)
    return total, src.reshape(nb, 1, cmax), col


def _combine(plan, xp, wsel, x_all, mod, ln, sg, su, sd, y, n_lat):
    t, w = xp.shape
    d = x_all.shape[1]
    tb = ROW_TILE
    hdn = sg.shape[1]
    total, src, col = plan
    return pl.pallas_call(
        functools.partial(_combine_kernel, tb=tb, n_lat=n_lat),
        grid=(t // tb,),
        in_specs=[
            pl.BlockSpec(memory_space=pltpu.SMEM),
            pl.BlockSpec((1, 1, src.shape[2]), lambda i: (i, 0, 0), memory_space=pltpu.SMEM),
            pl.BlockSpec((tb, w), lambda i: (i, 0)),
            pl.BlockSpec((tb, TOP_K), lambda i: (i, 0)),
            pl.BlockSpec((tb, TOP_K), lambda i: (i, 0)),
            pl.BlockSpec((tb, d), lambda i: (i, 0)),
            _const_spec((2, 6, d)),
            _const_spec((2, d)),
            _const_spec((d, hdn)),
            _const_spec((d, hdn)),
            _const_spec((hdn, d)),
            pl.BlockSpec(memory_space=pl.ANY),
        ],
        out_specs=pl.BlockSpec((tb, d), lambda i: (i, 0)),
        out_shape=jax.ShapeDtypeStruct((t, d), F32),
        scratch_shapes=[pltpu.VMEM((COMBINE_ROWS, d), BF16), pltpu.SemaphoreType.DMA(())],
        compiler_params=_cparams(1),
        name="moe_combine",
    )(total, src, xp, wsel, col, x_all, mod, ln, sg, su, sd, y)


def _moe(xp, logits, x_res, mod, ln, router_bias, w_gate, w_up, w_down, layer, sg, su, sd, n_lat):
    dest, wsel, blk_expert, n_used, n_blocks, pad_end, padded, plan = _route(logits, router_bias, MOE_TM)
    xs = _dispatch(xp, dest, pad_end, padded, n_blocks * MOE_TM, MOE_TM)
    y = _experts(xs, blk_expert, n_used, w_gate, w_up, w_down, layer)
    return _combine(plan, xp, wsel, x_res, mod, ln, sg.astype(BF16), su.astype(BF16), sd.astype(BF16), y, n_lat)


def _rot_lanes(xs, cos, sin):
    q = HEAD_DIM // 4
    lane = lax.broadcasted_iota(I32, xs.shape, 1)
    partner = jnp.where((lane % (2 * q)) < q, pltpu.roll(xs, HEAD_DIM - q, 1), pltpu.roll(xs, q, 1))
    return xs * cos + partner * sin


def _projq_kernel(x_ref, mod_ref, w_ref, cos_ref, sin_ref, o_ref, xm_ref, *, tm, n_lat):
    i = pl.program_id(0)
    j = pl.program_id(1)

    @pl.when(j == 0)
    def _():
        xm_ref[...] = _modulate_rows(x_ref[...], mod_ref, 0, 1, i * tm, n_lat).astype(BF16)

    acc = _dot(xm_ref[...], w_ref[...])
    n_heads = acc.shape[1] // HEAD_DIM

    @pl.when(j == 0)
    def _():
        for c in range(n_heads):
            xs = acc[:, c * HEAD_DIM:(c + 1) * HEAD_DIM]
            xs = xs * lax.rsqrt(jnp.mean(xs * xs, axis=-1, keepdims=True) + LN_EPS)
            o_ref[:, c * HEAD_DIM:(c + 1) * HEAD_DIM] = _rot_lanes(xs, cos_ref[0], sin_ref[0]).astype(BF16)

    @pl.when(j == 1)
    def _():
        for c in range(n_heads):
            xs = acc[:, c * HEAD_DIM:(c + 1) * HEAD_DIM]
            o_ref[:, c * HEAD_DIM:(c + 1) * HEAD_DIM] = _rot_lanes(xs, cos_ref[1], sin_ref[1]).astype(BF16)


def _projq(x_all, mod, w_bf, cos2, sin2, n_lat):
    d = x_all.shape[1]
    n = w_bf.shape[1]
    tm = 512 if n_lat % 512 == 0 else 256
    tn = n // 2
    return pl.pallas_call(
        functools.partial(_projq_kernel, tm=tm, n_lat=n_lat),
        grid=(n_lat // tm, 2),
        in_specs=[
            pl.BlockSpec((tm, d), lambda i, j: (i, 0)),
            pl.BlockSpec((2, 6, d), lambda i, j: (0, 0, 0)),
            pl.BlockSpec((d, tn), lambda i, j: (0, j)),
            pl.BlockSpec((2, tm, HEAD_DIM), lambda i, j: (0, i, 0)),
            pl.BlockSpec((2, tm, HEAD_DIM), lambda i, j: (0, i, 0)),
        ],
        out_specs=pl.BlockSpec((tm, tn), lambda i, j: (i, j)),
        out_shape=jax.ShapeDtypeStruct((n_lat, n), BF16),
        scratch_shapes=[pltpu.VMEM((tm, d), BF16)],
        compiler_params=_cparams(2),
        name="proj1_q",
    )(x_all, mod, w_bf, cos2, sin2)


def _projkv_kernel(x_ref, mod_ref, wv_ref, wkt_ref, cos_ref, sin_ref, v_ref, kt_ref, *, tm, n_lat):
    i = pl.program_id(0)
    xm = _modulate_rows(x_ref[...], mod_ref, 0, 1, i * tm, n_lat).astype(BF16)
    v_ref[...] = _dot(xm, wv_ref[...]).astype(BF16)
    kt = _dot_nt(wkt_ref[...], xm)
    q = HEAD_DIM // 4
    n_gk = GQA_KV_HEADS * HEAD_DIM
    for c in range(kt.shape[0] // HEAD_DIM):
        xs = kt[c * HEAD_DIM:(c + 1) * HEAD_DIM, :]
        t = 0 if c * HEAD_DIM < n_gk else 1
        if t == 0:
            xs = xs * lax.rsqrt(jnp.mean(xs * xs, axis=0, keepdims=True) + LN_EPS)
        partner = jnp.concatenate([xs[q:2 * q], xs[0:q], xs[3 * q:4 * q], xs[2 * q:3 * q]], axis=0)
        kt_ref[c * HEAD_DIM:(c + 1) * HEAD_DIM, :] = (xs * cos_ref[t] + partner * sin_ref[t]).astype(BF16)


def _projkv(x_all, mod, wv_bf, wkt_bf, cos_t, sin_t, n_lat):
    m, d = x_all.shape
    nv = wv_bf.shape[1]
    nk = wkt_bf.shape[0]
    tm = 640 if m % 640 == 0 else 256
    return pl.pallas_call(
        functools.partial(_projkv_kernel, tm=tm, n_lat=n_lat),
        grid=(m // tm,),
        in_specs=[
            pl.BlockSpec((tm, d), lambda i: (i, 0)),
            _const_spec((2, 6, d)),
            _const_spec((d, nv)),
            _const_spec((nk, d)),
            pl.BlockSpec((2, HEAD_DIM, tm), lambda i: (0, 0, i)),
            pl.BlockSpec((2, HEAD_DIM, tm), lambda i: (0, 0, i)),
        ],
        out_specs=[
            pl.BlockSpec((tm, nv), lambda i: (i, 0)),
            pl.BlockSpec((nk, tm), lambda i: (0, i)),
        ],
        out_shape=[jax.ShapeDtypeStruct((m, nv), BF16), jax.ShapeDtypeStruct((nk, m), BF16)],
        compiler_params=_cparams(1),
        name="proj1_kv",
    )(x_all, mod, wv_bf, wkt_bf, cos_t, sin_t)


def _flash(qs, kt_ref, v_ref, scratch, tk, stack):
    (qs_ref, s_refs, pm_refs, m_ref, l_ref, acc_ref) = scratch
    rows = qs_ref.shape[0]
    lk = v_ref.shape[0]
    n = lk // tk
    rs = rows // stack
    part = rows // len(qs)
    for a, q_part in enumerate(qs):
        qs_ref[a * part:(a + 1) * part, :] = q_part
    m_ref[...] = jnp.full(m_ref.shape, -jnp.inf, F32)
    l_ref[...] = jnp.zeros(l_ref.shape, F32)
    acc_ref[...] = jnp.zeros(acc_ref.shape, F32)

    n_grp = tk // LANES
    dv = acc_ref.shape[1]
    rb = FLASH_BLOCK if rows % FLASH_BLOCK == 0 else rows
    n_rb = rows // rb
    ps = min(rb, rs)

    def score_rows(j, r, s_ref, pm_ref):
        off = pl.multiple_of(j * tk, tk)
        for piece in range(rb // ps):
            r2 = pl.multiple_of(r + piece * ps, ps)
            ka = 0 if stack == 1 else pl.multiple_of((r2 // rs) * HEAD_DIM, HEAD_DIM)
            s = _dot(qs_ref[pl.ds(r2, ps), :], kt_ref[pl.ds(ka, HEAD_DIM), pl.ds(off, tk)])
            s_ref[pl.ds(r2, ps), :] = s
            pm = s[:, 0:LANES]
            for c in range(1, n_grp):
                pm = jnp.maximum(pm, s[:, c * LANES:(c + 1) * LANES])
            pm_ref[pl.ds(r2, ps), :] = pm

    sb = FLASH_SUB if rb % FLASH_SUB == 0 else rb

    def attend_rows(j, r, s_ref, pm_ref):
        off = pl.multiple_of(j * tk, tk)
        blk = pl.ds(r, sb)
        m_prev = m_ref[blk, :]
        m_next = jnp.maximum(m_prev, jnp.max(pm_ref[blk, :], axis=1, keepdims=True))
        alpha = jnp.exp2(m_prev - m_next)
        m_ref[blk, :] = m_next
        pieces = []
        for q in range(sb // FLASH_ROWS):
            sub = pl.ds(pl.multiple_of(r + q * FLASH_ROWS, FLASH_ROWS), FLASH_ROWS)
            mb = m_next[q * FLASH_ROWS:(q + 1) * FLASH_ROWS]
            lsum = alpha[q * FLASH_ROWS:(q + 1) * FLASH_ROWS] * l_ref[sub, :]
            row = []
            for c in range(n_grp):
                pc = jnp.exp2(s_ref[sub, c * LANES:(c + 1) * LANES] - mb)
                lsum = lsum + pc
                row.append(pc.astype(BF16))
            l_ref[sub, :] = lsum
            pieces.append(jnp.concatenate(row, axis=1))
        pv = _dot(jnp.concatenate(pieces, axis=0), v_ref[pl.ds(off, tk), :])
        for c in range(dv // LANES):
            cols = slice(c * LANES, (c + 1) * LANES)
            acc_ref[blk, cols] = alpha * acc_ref[blk, cols] + pv[:, cols]

    def step_parity(j, cur, with_scores):
        nxt = 1 - cur

        def body(i, carry):
            r = pl.multiple_of(i * rb, rb)
            if with_scores:
                score_rows(j + 1, r, s_refs[nxt], pm_refs[nxt])
            for h in range(rb // sb):
                attend_rows(j, pl.multiple_of(r + h * sb, sb), s_refs[cur], pm_refs[cur])
            return carry

        lax.fori_loop(0, n_rb, body, 0)

    def first_scores(i, carry):
        score_rows(0, pl.multiple_of(i * rb, rb), s_refs[0], pm_refs[0])
        return carry

    lax.fori_loop(0, n_rb, first_scores, 0)

    def pair(i, carry):
        j = 2 * i
        step_parity(j, 0, True)
        step_parity(j + 1, 1, True)
        return carry

    lax.fori_loop(0, (n - 1) // 2, pair, 0)
    if (n - 1) % 2:
        step_parity(n - 2, (n - 2) % 2, True)
    step_parity(n - 1, (n - 1) % 2, False)
    return acc_ref[...] / jnp.sum(l_ref[...], axis=1, keepdims=True)


def _flash_scratch(rows, tk, dv):
    stat = pltpu.VMEM((rows, LANES), F32)
    return [pltpu.VMEM((rows, HEAD_DIM), BF16),
            pltpu.VMEM((rows, tk), F32), pltpu.VMEM((rows, tk), F32), stat, stat,
            stat, stat, pltpu.VMEM((rows, dv), F32)]


def _unpack_flash_scratch(refs):
    qs_ref, s0, s1, pm0, pm1, m_ref, l_ref, acc_ref = refs
    return (qs_ref, (s0, s1), (pm0, pm1), m_ref, l_ref, acc_ref)


def _gqa_kernel(q_ref, kt_ref, v_ref, o_ref, *scratch, tk):
    tq = q_ref.shape[0]
    qs = [q_ref[:, a * HEAD_DIM:(a + 1) * HEAD_DIM] for a in range(GQA_GROUP)]
    o = _flash(qs, kt_ref, v_ref, _unpack_flash_scratch(scratch), tk, 1)
    for a in range(GQA_GROUP):
        o_ref[:, a * HEAD_DIM:(a + 1) * HEAD_DIM] = o[a * tq:(a + 1) * tq].astype(BF16)


def _diff_kernel(lam_ref, q_ref, kt_ref, v_ref, g_ref, o_ref, *scratch, tk, out_scale):
    tq = q_ref.shape[0]
    qs = [q_ref[:, 0:HEAD_DIM], q_ref[:, HEAD_DIM:2 * HEAD_DIM]]
    o = _flash(qs, kt_ref, v_ref, _unpack_flash_scratch(scratch), tk, 2)
    od = o[0:tq] - lam_ref[0] * o[tq:2 * tq]
    od = od * lax.rsqrt(jnp.mean(od * od, axis=-1, keepdims=True) + LN_EPS)
    o_ref[...] = (od * g_ref[...] * out_scale).astype(BF16)


def _key_chunk(lk):
    for tk in (1280, 1024, 640, 512, 256, 128):
        if lk % tk == 0 and lk // tk >= 2:
            return tk
    return lk


def _attention(q1, kt, v1, lam, subln_g, lambda_init, n_lat):
    lk = v1.shape[0]
    tk = _key_chunk(lk)
    tq = 256
    gw = GQA_GROUP * HEAD_DIM
    n_gv = GQA_KV_HEADS * HEAD_DIM
    once = pl.Buffered(1)
    og = pl.pallas_call(
        functools.partial(_gqa_kernel, tk=tk),
        grid=(GQA_KV_HEADS, n_lat // tq),
        in_specs=[
            pl.BlockSpec((tq, gw), lambda g, i: (i, g)),
            pl.BlockSpec((HEAD_DIM, lk), lambda g, i: (g, 0), pipeline_mode=once),
            pl.BlockSpec((lk, HEAD_DIM), lambda g, i: (0, g), pipeline_mode=once),
        ],
        out_specs=pl.BlockSpec((tq, gw), lambda g, i: (i, g)),
        out_shape=jax.ShapeDtypeStruct((n_lat, GQA_HEADS * HEAD_DIM), BF16),
        scratch_shapes=_flash_scratch(GQA_GROUP * tq, tk, HEAD_DIM),
        compiler_params=_cparams(2),
        name="gqa_attention",
    )(q1, kt, v1)
    q_off = GQA_HEADS * HEAD_DIM // DIFF_V_DIM
    kv_off = n_gv // DIFF_V_DIM
    tq = 512 if n_lat % 512 == 0 else 256
    od = pl.pallas_call(
        functools.partial(_diff_kernel, tk=tk, out_scale=1.0 - lambda_init),
        grid=(DIFF_HEADS, n_lat // tq),
        in_specs=[
            pl.BlockSpec(memory_space=pltpu.SMEM),
            pl.BlockSpec((tq, DIFF_V_DIM), lambda h, i: (i, q_off + h)),
            pl.BlockSpec((DIFF_V_DIM, lk), lambda h, i: (kv_off + h, 0), pipeline_mode=once),
            pl.BlockSpec((lk, DIFF_V_DIM), lambda h, i: (0, kv_off + h), pipeline_mode=once),
            pl.BlockSpec((1, DIFF_V_DIM), lambda h, i: (0, 0)),
        ],
        out_specs=pl.BlockSpec((tq, DIFF_V_DIM), lambda h, i: (i, h)),
        out_shape=jax.ShapeDtypeStruct((n_lat, DIFF_HEADS * DIFF_V_DIM), BF16),
        scratch_shapes=_flash_scratch(2 * tq, tk, DIFF_V_DIM),
        compiler_params=_cparams(2),
        name="diff_attention",
    )(lam, q1, kt, v1, subln_g.reshape(1, DIFF_V_DIM).astype(F32))
    return og, od


def _router_operand(router_w):
    return router_w.astype(F32).T


def kernel(x, c, ctx, c_ctx, mod_w, mod_b, ln_g, ln_b, ev_w_in, ev_w_out, ret_log_decay, s5_a_re, s5_a_im, s5_log_dt, s5_b_re, s5_b_im, s5_c_re, s5_c_im, s5_d, s5_w_glu, s5_b_glu, od_w_in, od_w_out, qk_norm_g, diff_lambda, diff_subln_g, router_w, router_bias, exp_w_gate, exp_w_up, exp_w_down, sh_w_gate, sh_w_up, sh_w_down):
    b_, n_lat, d = x.shape
    n_ctx = ctx.shape[1]
    assert b_ == 1 and d == D_MODEL and n_lat % RET_CHUNK == 0 and n_ctx % RET_CHUNK == 0
    m_all = n_lat + n_ctx
    x_all = jnp.concatenate([x[0], ctx[0]], axis=0)
    mods = _modulation(c, c_ctx, mod_w, mod_b)

    cos0, sin0 = _rope_tables(n_lat, n_ctx, RET_DIM)
    p0 = _proj0(x_all, mods[0], ev_w_in[0].astype(BF16), cos0, sin0, n_lat)
    rn = _retention(p0, ret_log_decay[0], n_lat, n_ctx)
    s5_tabs = functools.partial(_s5_tables, s5_a_re[0], s5_a_im[0], s5_log_dt[0], s5_b_re[0], s5_b_im[0],
                                s5_c_re[0], s5_c_im[0], s5_d[0])
    s5y = _s5(p0, 4 * RET_HEADS * RET_DIM, s5_tabs, n_lat, n_ctx)
    ln0 = jnp.stack([ln_g[0], ln_b[0]], axis=1)
    x_all, xp, logits = _finish0(rn, p0, s5y, x_all, mods[0], ln0[0], s5_w_glu[0].astype(BF16),
                                 s5_b_glu[0].reshape(1, -1).astype(F32), ev_w_out[0].astype(BF16),
                                 _router_operand(router_w[0]), n_lat)
    x_all = _moe(xp, logits, x_all, mods[0], ln0[1], router_bias[0], exp_w_gate, exp_w_up, exp_w_down, 0,
                 sh_w_gate[0], sh_w_up[0], sh_w_down[0], n_lat)

    i = 1
    lambda_init = 0.8 - 0.6 * math.exp(-0.3 * i)
    gq_w = GQA_HEADS * HEAD_DIM
    gk_w = GQA_KV_HEADS * HEAD_DIM
    dq_w = DIFF_HEADS * 2 * HEAD_DIM
    cuts = [gq_w, gq_w + gk_w, gq_w + 2 * gk_w, gq_w + 2 * gk_w + dq_w, gq_w + 2 * gk_w + 2 * dq_w]
    w_in = od_w_in[0]
    w_gq, w_gk, w_gv, w_dq, w_dk, w_dv = (w_in[:, a:b] for a, b in zip([0] + cuts, cuts + [w_in.shape[1]]))
    cos1, sin1 = _rope_tables(n_lat, n_ctx, HEAD_DIM)
    qscale = HEAD_DIM ** -0.5 * LOG2E
    gq_gain = qk_norm_g[0, 0].astype(F32)
    gk_gain = qk_norm_g[0, 1].astype(F32)
    quarter = HEAD_DIM // 4

    def partner_gain(g):
        return jnp.concatenate([g[quarter:2 * quarter], g[0:quarter], g[3 * quarter:], g[2 * quarter:3 * quarter]])

    cos_q = jnp.stack([cos1[:n_lat] * gq_gain[None, :], cos1[:n_lat]]) * qscale
    sin_q = jnp.stack([sin1[:n_lat] * partner_gain(gq_gain)[None, :], sin1[:n_lat]]) * qscale
    cos_k = jnp.stack([cos1.T * gk_gain[:, None], cos1.T])
    sin_k = jnp.stack([sin1.T * partner_gain(gk_gain)[:, None], sin1.T])
    q1 = _projq(x_all, mods[1], jnp.concatenate([w_gq, w_dq], axis=1).astype(BF16), cos_q, sin_q, n_lat)
    v1, kt = _projkv(x_all, mods[1], jnp.concatenate([w_gv, w_dv], axis=1).astype(BF16),
                     jnp.concatenate([w_gk, w_dk], axis=1).T.astype(BF16), cos_k, sin_k, n_lat)
    lf = diff_lambda[0].astype(F32)
    lam = (jnp.exp(jnp.sum(lf[0] * lf[1])) - jnp.exp(jnp.sum(lf[2] * lf[3])) + lambda_init).reshape(1)
    og, od = _attention(q1, kt, v1, lam, diff_subln_g[0], lambda_init, n_lat)
    ln1 = jnp.stack([ln_g[1], ln_b[1]], axis=1)
    x_lat, xp, logits = _finish1(og, od, x_all, mods[1], ln1[0], od_w_out[0].astype(BF16),
                                 _router_operand(router_w[1]), n_lat)
    x_lat = _moe(xp, logits, x_lat, mods[1], ln1[1], router_bias[1], exp_w_gate, exp_w_up, exp_w_down, 1,
                 sh_w_gate[1], sh_w_up[1], sh_w_down[1], n_lat)
    return x_lat[None]
```

```python
import functools
import math

import jax
import jax.numpy as jnp
from jax import lax
from jax.experimental import pallas as pl
from jax.experimental.pallas import tpu as pltpu

F32 = jnp.float32
BF16 = jnp.bfloat16
U32 = jnp.uint32
I32 = jnp.int32

D_MODEL = 2048
DEPTH = 2
GRID_W = 64
ROPE_BASE = 10000.0
LN_EPS = 1e-6
DEEPNORM_ALPHA = (2 * DEPTH) ** 0.25
RET_HEADS = 4
RET_DIM = 256
S5_CHANNELS = D_MODEL // 2
S5_GROUP = 16
S5_GROUPS = S5_CHANNELS // S5_GROUP
S5_STATE = 64
HEAD_DIM = 128
GQA_HEADS = 8
GQA_KV_HEADS = 2
GQA_GROUP = GQA_HEADS // GQA_KV_HEADS
DIFF_HEADS = 4
DIFF_V_DIM = 2 * HEAD_DIM
N_EXPERTS = 64
TOP_K = 8
N_EXPERT_GROUPS = 8
TOPK_GROUPS = 4
EXPERT_HIDDEN = 512
ROUTED_SCALE = 2.5
LOG2E = 1.4426950408889634

LANES = 128
VMEM_LIMIT = 56 * 1024 * 1024

RET_CHUNK = 256
S5_STEP = 16
MOE_TM = 256
ROW_TILE = 256
COMBINE_CHUNK = 16
COMBINE_ROWS = ROW_TILE * TOP_K + 2 * N_EXPERTS * COMBINE_CHUNK
COMBINE_PIECE = 512
FLASH_ROWS = 64
FLASH_BLOCK = 1024
FLASH_SUB = 256


def _cparams(n_axes):
    return pltpu.CompilerParams(dimension_semantics=("arbitrary",) * n_axes, vmem_limit_bytes=VMEM_LIMIT)


def _dot(a, b):
    return jnp.dot(a, b, preferred_element_type=F32)


def _dot_nt(a, b):
    return lax.dot_general(a, b, (((1,), (1,)), ((), ())), preferred_element_type=F32)


def _dot_tn(a, b):
    return lax.dot_general(a, b, (((0,), (0,)), ((), ())), preferred_element_type=F32)


def _split(a):
    hi = a.astype(BF16)
    lo = (a - hi.astype(F32)).astype(BF16)
    return hi, lo


def _dot3(a, b):
    ah, al = _split(a)
    bh, bl = _split(b)
    return _dot(ah, bh) + _dot(al, bh) + _dot(ah, bl)


def _sigmoid(x):
    return 1.0 / (1.0 + jnp.exp(-x))


def _silu(x):
    return x * _sigmoid(x)


def _gelu_tanh(x):
    return 0.5 * x * (1.0 + jnp.tanh(math.sqrt(2.0 / math.pi) * (x + 0.044715 * (x * x * x))))


def _layer_norm(h, g, b):
    mu = jnp.mean(h, axis=-1, keepdims=True)
    d = h - mu
    var = jnp.mean(d * d, axis=-1, keepdims=True)
    return d * lax.rsqrt(var + LN_EPS) * g + b


def _pack_bf16_pairs(x):
    n = x.shape[1] // 2
    lo = lax.bitcast_convert_type(x[:, :n].astype(BF16).astype(F32), U32)
    hi = lax.bitcast_convert_type(x[:, n:].astype(BF16).astype(F32), U32)
    return (lo >> 16) | (hi & jnp.uint32(0xFFFF0000))


def _unpack_bf16_pairs(w):
    lo = lax.bitcast_convert_type(w << 16, F32)
    hi = lax.bitcast_convert_type(w & jnp.uint32(0xFFFF0000), F32)
    return lo, hi


def _modulate_rows(x, mod_ref, shift_row, scale_row, row0, n_lat):
    rows = row0 + lax.broadcasted_iota(I32, (x.shape[0], 1), 0)
    is_ctx = rows >= n_lat
    sc = jnp.where(is_ctx, mod_ref[1, scale_row:scale_row + 1, :], mod_ref[0, scale_row:scale_row + 1, :])
    sh = jnp.where(is_ctx, mod_ref[1, shift_row:shift_row + 1, :], mod_ref[0, shift_row:shift_row + 1, :])
    return x * (1.0 + sc) + sh


def _select_rows(mod_ref, row, row0, n_rows, n_lat):
    rows = row0 + lax.broadcasted_iota(I32, (n_rows, 1), 0)
    return jnp.where(rows >= n_lat, mod_ref[1, row:row + 1, :], mod_ref[0, row:row + 1, :])


def _mod_kernel(v_ref, w_ref, b_ref, o_ref):
    v = v_ref[...]
    o_ref[0] = _dot3(_silu(v), w_ref[0]) + b_ref[0]


def _modulation(c, c_ctx, mod_w, mod_b):
    depth, d, n = mod_w.shape
    v = jnp.concatenate([c[:1], c_ctx[None], jnp.zeros((6, d), F32)], axis=0)
    tn = 512
    out = pl.pallas_call(
        _mod_kernel,
        grid=(depth, n // tn),
        in_specs=[
            pl.BlockSpec((8, d), lambda i, j: (0, 0)),
            pl.BlockSpec((1, d, tn), lambda i, j: (i, 0, j)),
            pl.BlockSpec((1, 1, tn), lambda i, j: (i, 0, j)),
        ],
        out_specs=pl.BlockSpec((1, 8, tn), lambda i, j: (i, 0, j)),
        out_shape=jax.ShapeDtypeStruct((depth, 8, n), F32),
        compiler_params=_cparams(2),
        name="modulation",
    )(v, mod_w, mod_b.reshape(depth, 1, n))
    return out[:, :2].reshape(depth, 2, 6, d)


def _rope_tables(n_lat, n_ctx, head_dim):
    q = head_dim // 4
    t = jnp.arange(n_lat, dtype=I32)
    row = (t // GRID_W).astype(F32)
    col = (t % GRID_W).astype(F32)
    freqs = ROPE_BASE ** (-jnp.arange(q, dtype=F32) / q)
    ar = row[:, None] * freqs[None, :]
    ac = col[:, None] * freqs[None, :]
    cos = jnp.concatenate([jnp.cos(ar), jnp.cos(ar), jnp.cos(ac), jnp.cos(ac)], axis=-1)
    sin = jnp.concatenate([-jnp.sin(ar), jnp.sin(ar), -jnp.sin(ac), jnp.sin(ac)], axis=-1)
    cos = jnp.concatenate([cos, jnp.ones((n_ctx, head_dim), F32)], axis=0)
    sin = jnp.concatenate([sin, jnp.zeros((n_ctx, head_dim), F32)], axis=0)
    return cos, sin


def _proj0_kernel(x_ref, mod_ref, w_ref, cos_ref, sin_ref, o_ref, xm_ref, *, tm, n_lat):
    i = pl.program_id(0)
    j = pl.program_id(1)

    @pl.when(j == 0)
    def _():
        xm_ref[...] = _modulate_rows(x_ref[...], mod_ref, 0, 1, i * tm, n_lat).astype(BF16)

    acc = _dot(xm_ref[...], w_ref[...])

    @pl.when(j >= 2)
    def _():
        o_ref[...] = acc.astype(BF16)

    @pl.when(j < 2)
    def _():
        scale = jnp.where(j == 0, RET_DIM ** -0.5, 1.0).astype(F32)
        for c in range(acc.shape[1] // LANES):
            xs = acc[:, c * LANES:(c + 1) * LANES]
            t0 = (c % 2) * LANES
            rot = xs * cos_ref[:, t0:t0 + LANES] + pltpu.roll(xs, LANES // 2, 1) * sin_ref[:, t0:t0 + LANES]
            o_ref[:, c * LANES:(c + 1) * LANES] = (rot * scale).astype(BF16)


def _proj0(x_all, mod, w_bf, cos, sin, n_lat):
    m, d = x_all.shape
    n = w_bf.shape[1]
    tm = 640 if m % 640 == 0 else 256
    tn = 1024
    return pl.pallas_call(
        functools.partial(_proj0_kernel, tm=tm, n_lat=n_lat),
        grid=(m // tm, n // tn),
        in_specs=[
            pl.BlockSpec((tm, d), lambda i, j: (i, 0)),
            pl.BlockSpec((2, 6, d), lambda i, j: (0, 0, 0)),
            pl.BlockSpec((d, tn), lambda i, j: (0, j)),
            pl.BlockSpec((tm, RET_DIM), lambda i, j: (i, 0)),
            pl.BlockSpec((tm, RET_DIM), lambda i, j: (i, 0)),
        ],
        out_specs=pl.BlockSpec((tm, tn), lambda i, j: (i, j)),
        out_shape=jax.ShapeDtypeStruct((m, n), BF16),
        scratch_shapes=[pltpu.VMEM((tm, d), BF16)],
        compiler_params=_cparams(2),
        name="proj0",
    )(x_all, mod, w_bf, cos, sin)


def _ret_tables(log_decay):
    c = RET_CHUNK
    lg = log_decay.astype(F32)
    idx = jnp.arange(c, dtype=F32)
    diff = idx[:, None] - idx[None, :]
    lower = jnp.where(diff >= 0, jnp.exp(jnp.maximum(diff, 0.0)[None] * lg[0][:, None, None]), 0.0)
    upper = jnp.where(diff <= 0, jnp.exp(jnp.maximum(-diff, 0.0)[None] * lg[1][:, None, None]), 0.0)
    mask = lower + upper
    ones = jnp.ones((1, 1, RET_DIM), F32)
    kdec_f = jnp.exp((c - 1 - idx)[None, :] * lg[0][:, None])[:, :, None] * ones
    qdec_f = jnp.exp((idx + 1)[None, :] * lg[0][:, None])[:, :, None] * ones
    kdec_b = jnp.exp(idx[None, :] * lg[1][:, None])[:, :, None] * ones
    qdec_b = jnp.exp((c - idx)[None, :] * lg[1][:, None])[:, :, None] * ones
    g_chunk = jnp.exp(c * lg)
    return mask, kdec_f, qdec_f, kdec_b, qdec_b, g_chunk


def _ret_bwd_state_kernel(gc_ref, k_ref, v_ref, kdec_ref, sb_ref, s_ref):
    h = pl.program_id(0)
    s = pl.program_id(1)

    @pl.when(s == 0)
    def _():
        s_ref[...] = jnp.zeros_like(s_ref)

    sb_ref[0, 0] = s_ref[...].astype(BF16)
    kd = (k_ref[...].astype(F32) * kdec_ref[0]).astype(BF16)
    s_ref[...] = gc_ref[1, h] * s_ref[...] + _dot_tn(kd, v_ref[...])


def _ret_out_kernel(gc_ref, q_ref, k_ref, v_ref, sb_ref, mask_ref, qdf_ref, qdb_ref, kdf_ref, o_ref, s_ref):
    h = pl.program_id(0)
    s = pl.program_id(1)

    @pl.when(s == 0)
    def _():
        s_ref[...] = jnp.zeros_like(s_ref)

    q = q_ref[...]
    k = k_ref[...]
    v = v_ref[...]
    qf = q.astype(F32)
    w = (_dot_nt(q, k) * mask_ref[0]).astype(BF16)
    o = _dot(w, v)
    o = o + _dot((qf * qdf_ref[0]).astype(BF16), s_ref[...].astype(BF16))
    o = o + _dot((qf * qdb_ref[0]).astype(BF16), sb_ref[0, 0])
    mu = jnp.mean(o, axis=-1, keepdims=True)
    d = o - mu
    var = jnp.mean(d * d, axis=-1, keepdims=True)
    o_ref[...] = (d * lax.rsqrt(var + LN_EPS)).astype(BF16)
    kd = (k.astype(F32) * kdf_ref[0]).astype(BF16)
    s_ref[...] = gc_ref[0, h] * s_ref[...] + _dot_tn(kd, v)


def _retention(p0, log_decay, n_lat, n_ctx):
    m = p0.shape[0]
    c = RET_CHUNK
    nc = m // c
    nlc = n_lat // c
    hh = RET_HEADS
    mask, kdec_f, qdec_f, kdec_b, qdec_b, g_chunk = _ret_tables(log_decay)
    smem = pl.BlockSpec(memory_space=pltpu.SMEM)

    def bchunk(s):
        return nc - 1 - s

    sb = pl.pallas_call(
        _ret_bwd_state_kernel,
        grid=(hh, nc),
        in_specs=[
            smem,
            pl.BlockSpec((c, RET_DIM), lambda h, s: (bchunk(s), hh + h)),
            pl.BlockSpec((c, RET_DIM), lambda h, s: (bchunk(s), 2 * hh + h)),
            pl.BlockSpec((1, c, RET_DIM), lambda h, s: (h, 0, 0)),
        ],
        out_specs=pl.BlockSpec((1, 1, RET_DIM, RET_DIM), lambda h, s: (h, bchunk(s), 0, 0)),
        out_shape=jax.ShapeDtypeStruct((hh, nc, RET_DIM, RET_DIM), BF16),
        scratch_shapes=[pltpu.VMEM((RET_DIM, RET_DIM), F32)],
        compiler_params=_cparams(2),
        name="ret_bwd_state",
    )(g_chunk, p0, p0, kdec_b)

    def fchunk(s):
        return (s + nlc) % nc

    return pl.pallas_call(
        _ret_out_kernel,
        grid=(hh, nc),
        in_specs=[
            smem,
            pl.BlockSpec((c, RET_DIM), lambda h, s: (fchunk(s), h)),
            pl.BlockSpec((c, RET_DIM), lambda h, s: (fchunk(s), hh + h)),
            pl.BlockSpec((c, RET_DIM), lambda h, s: (fchunk(s), 2 * hh + h)),
            pl.BlockSpec((1, 1, RET_DIM, RET_DIM), lambda h, s: (h, fchunk(s), 0, 0)),
            pl.BlockSpec((1, c, c), lambda h, s: (h, 0, 0)),
            pl.BlockSpec((1, c, RET_DIM), lambda h, s: (h, 0, 0)),
            pl.BlockSpec((1, c, RET_DIM), lambda h, s: (h, 0, 0)),
            pl.BlockSpec((1, c, RET_DIM), lambda h, s: (h, 0, 0)),
        ],
        out_specs=pl.BlockSpec((c, RET_DIM), lambda h, s: (fchunk(s), h)),
        out_shape=jax.ShapeDtypeStruct((m, hh * RET_DIM), BF16),
        scratch_shapes=[pltpu.VMEM((RET_DIM, RET_DIM), F32)],
        compiler_params=_cparams(2),
        name="ret_out",
    )(g_chunk, p0, p0, p0, sb, mask, qdec_f, qdec_b, kdec_f)


def _s5_tables(a_re, a_im, log_dt, b_re, b_im, c_re, c_im, d_skip, n_scan):
    f32 = F32
    st = S5_STEP
    g_, p_, s_ = S5_GROUPS, S5_STATE, S5_GROUP
    b_re, b_im, c_re, c_im = (t.astype(f32) for t in (b_re, b_im, c_re, c_im))
    ws, vs, aps, ks = [], [], [], []
    for direction in range(2):
        are, aim = a_re[direction].astype(f32), a_im[direction].astype(f32)
        dt = jnp.exp(log_dt[direction].astype(f32))[:, None]
        zr, zi = are * dt, aim * dt
        mag = jnp.exp(zr)
        ab_re, ab_im = mag * jnp.cos(zi), mag * jnp.sin(zi)
        den = jnp.square(are) + jnp.square(aim)
        nr, ni = ab_re - 1.0, ab_im
        f_re = (nr * are + ni * aim) / den
        f_im = (ni * are - nr * aim) / den
        bb_re = f_re[..., None] * b_re - f_im[..., None] * b_im
        bb_im = f_re[..., None] * b_im + f_im[..., None] * b_re
        pr, pi = [jnp.ones_like(ab_re)], [jnp.zeros_like(ab_im)]
        for _ in range(st):
            pr.append(pr[-1] * ab_re - pi[-1] * ab_im)
            pi.append(pr[-2] * ab_im + pi[-1] * ab_re)
        pw_re, pw_im = jnp.stack(pr), jnp.stack(pi)
        ca_re = c_re[None] * pw_re[:, :, None, :] - c_im[None] * pw_im[:, :, None, :]
        ca_im = c_re[None] * pw_im[:, :, None, :] + c_im[None] * pw_re[:, :, None, :]
        kk = (jnp.einsum('tgip,gpj->tgij', ca_re[:st], bb_re, precision='highest')
              - jnp.einsum('tgip,gpj->tgij', ca_im[:st], bb_im, precision='highest'))
        ks.append(kk)
        e = (st - 1 - jnp.arange(st)) if direction == 0 else jnp.arange(st)
        w_re = pw_re[e][:, :, :, None] * bb_re[None] - pw_im[e][:, :, :, None] * bb_im[None]
        w_im = pw_re[e][:, :, :, None] * bb_im[None] + pw_im[e][:, :, :, None] * bb_re[None]
        w = jnp.concatenate([w_re, w_im], axis=2)
        ws.append(jnp.transpose(w, (1, 0, 3, 2)).reshape(g_, st * s_, 2 * p_))
        e2 = (jnp.arange(st) + 1) if direction == 0 else (st - jnp.arange(st))
        v = jnp.concatenate([ca_re[e2], -ca_im[e2]], axis=3)
        vs.append(jnp.transpose(v, (1, 3, 0, 2)).reshape(g_, 2 * p_, st * s_))
        qr, qi = pw_re[st], pw_im[st]
        rows = []
        for _ in range(n_scan):
            rows.append(jnp.stack([jnp.concatenate([qr, qr], -1), jnp.concatenate([-qi, qi], -1)], axis=1))
            qr, qi = qr * qr - qi * qi, 2.0 * qr * qi
        aps.append(jnp.stack(rows, axis=1))
    tt = jnp.arange(st)
    lag = tt[None, :] - tt[:, None]
    kf = ks[0][jnp.clip(lag, 0, st - 1)]
    kb = ks[1][jnp.clip(-lag, 0, st - 1)]
    tm = jnp.where((lag >= 0)[:, :, None, None, None], kf, 0.0) + jnp.where((lag <= 0)[:, :, None, None, None], kb, 0.0)
    tm = jnp.transpose(tm, (2, 0, 4, 1, 3)).reshape(g_, st * s_, st * s_)
    dsk = d_skip.astype(f32).reshape(g_, s_)
    tm = tm + jnp.eye(st * s_, dtype=f32)[None] * jnp.tile(dsk, (1, st))[:, None, :]
    return tm, jnp.stack(ws), jnp.stack(vs), jnp.stack(aps)


def _s5_kernel(u_ref, t_ref, w_ref, v_ref, a_ref, y_ref, uf_ref, ut_ref, yg_ref, ha_ref, hb_ref,
               *, ncs, nls, n_scan):
    st = S5_STEP
    nt = nls + ncs
    groups = LANES // S5_GROUP
    tw = st * S5_GROUP
    shift = S5_GROUP.bit_length() - 1
    lane_shift = LANES.bit_length() - 1
    uf_ref[...] = u_ref[...].astype(F32)
    for t in range(st):
        ut_ref[t // 2, :, (t % 2) * LANES:(t % 2 + 1) * LANES] = uf_ref[pl.ds(t, nt, stride=st), :].astype(BF16)
    src = lax.broadcasted_iota(I32, (2 * LANES, tw), 0)
    pos = lax.broadcasted_iota(I32, (2 * LANES, tw), 1)
    for g in range(groups):
        base = ((src & (LANES - 1)) == g * S5_GROUP + (pos & (S5_GROUP - 1)))
        ug = jnp.zeros((nt, tw), F32)
        for p in range(st // 2):
            sel = jnp.where(base & ((pos >> shift) == 2 * p + (src >> lane_shift)), 1.0, 0.0).astype(BF16)
            ug = ug + _dot(ut_ref[p], sel)
        ug = ug.astype(BF16)
        useq = jnp.concatenate([ug[nls:nt], ug[0:nls], ug[nls:nt]], axis=0)
        yg_ref[g] = _s5_group(useq, t_ref, w_ref, v_ref, a_ref, ha_ref, hb_ref, g, ncs, nls, n_scan).astype(BF16)
    dst = lax.broadcasted_iota(I32, (tw, 2 * LANES), 1)
    pos_t = lax.broadcasted_iota(I32, (tw, 2 * LANES), 0)
    for p in range(st // 2):
        rows2 = jnp.zeros((nt, 2 * LANES), F32)
        for g in range(groups):
            hit = (((dst & (LANES - 1)) == g * S5_GROUP + (pos_t & (S5_GROUP - 1)))
                   & ((pos_t >> shift) == 2 * p + (dst >> lane_shift)))
            rows2 = rows2 + _dot(yg_ref[g], jnp.where(hit, 1.0, 0.0).astype(BF16))
        uf_ref[pl.ds(2 * p, nt, stride=st), :] = rows2[:, 0:LANES]
        uf_ref[pl.ds(2 * p + 1, nt, stride=st), :] = rows2[:, LANES:2 * LANES]
    y_ref[...] = uf_ref[...].astype(BF16)


def _s5_group(u, t_ref, w_ref, v_ref, a_ref, ha_ref, hb_ref, g, ncs, nls, n_scan):
    n = ncs + nls + ncs

    def mm(x, mat):
        mh, ml = _split(mat)
        return _dot(x, mh) + _dot(x, ml)

    def mm3(x, mat):
        xh, xl = _split(x)
        mh, ml = _split(mat)
        return _dot(xh, mh) + _dot(xl, mh) + _dot(xh, ml)

    rows = lax.broadcasted_iota(I32, (n, 1), 0)

    def scan(z, direction):
        bufs = (ha_ref, hb_ref)
        bufs[0][...] = z
        for kk in range(n_scan):
            src, dst = bufs[kk % 2], bufs[(kk + 1) % 2]
            sft = 1 << kk
            a1 = a_ref[direction, g, kk, 0:1, :]
            a2 = a_ref[direction, g, kk, 1:2, :]
            if sft >= n:
                dst[...] = src[...]
                continue
            if sft % 8 == 0:
                if direction == 0:
                    prev = src[0:n - sft, :]
                    dst[0:sft, :] = src[0:sft, :]
                    dst[sft:n, :] = src[sft:n, :] + a1 * prev + a2 * pltpu.roll(prev, S5_STATE, 1)
                else:
                    nxt = src[sft:n, :]
                    dst[n - sft:n, :] = src[n - sft:n, :]
                    dst[0:n - sft, :] = src[0:n - sft, :] + a1 * nxt + a2 * pltpu.roll(nxt, S5_STATE, 1)
            else:
                cur = src[...]
                if direction == 0:
                    sh = jnp.where(rows >= sft, pltpu.roll(cur, sft, 0), 0.0)
                else:
                    sh = jnp.where(rows < n - sft, pltpu.roll(cur, n - sft, 0), 0.0)
                dst[...] = cur + a1 * sh + a2 * pltpu.roll(sh, S5_STATE, 1)
        return bufs[n_scan % 2][...]

    y = mm(u, t_ref[g])
    hf = scan(mm(u, w_ref[0, g]), 0)
    hf_prev = jnp.where(rows >= 1, pltpu.roll(hf, 1, 0), 0.0)
    yf = mm3(hf_prev, v_ref[0, g])
    hb = scan(mm(u, w_ref[1, g]), 1)
    hb_next = jnp.where(rows < n - 1, pltpu.roll(hb, n - 1, 0), 0.0)
    yb = mm3(hb_next, v_ref[1, g])
    return jnp.concatenate([(y + yf + yb)[ncs:ncs + nls, :], (y + yf)[0:ncs, :] + yb[ncs + nls:n, :]], axis=0)


def _s5(p0, col0, tabs, n_lat, n_ctx):
    m = p0.shape[0]
    st = S5_STEP
    nls, ncs = n_lat // st, n_ctx // st
    n = ncs + nls + ncs
    nt = nls + ncs
    n_scan = max(1, (n - 1).bit_length())
    t_mat, w_mat, v_mat, apow = tabs(n_scan)
    tw = st * S5_GROUP
    gpl = LANES // S5_GROUP
    blk0 = col0 // LANES
    return pl.pallas_call(
        functools.partial(_s5_kernel, ncs=ncs, nls=nls, n_scan=n_scan),
        grid=(S5_CHANNELS // LANES,),
        in_specs=[
            pl.BlockSpec((m, LANES), lambda j: (0, blk0 + j)),
            pl.BlockSpec((gpl, tw, tw), lambda j: (j, 0, 0)),
            pl.BlockSpec((2, gpl, tw, 2 * S5_STATE), lambda j: (0, j, 0, 0)),
            pl.BlockSpec((2, gpl, 2 * S5_STATE, tw), lambda j: (0, j, 0, 0)),
            pl.BlockSpec((2, gpl, n_scan, 2, 2 * S5_STATE), lambda j: (0, j, 0, 0, 0)),
        ],
        out_specs=pl.BlockSpec((m, LANES), lambda j: (0, j)),
        out_shape=jax.ShapeDtypeStruct((m, S5_CHANNELS), BF16),
        scratch_shapes=[pltpu.VMEM((m, LANES), F32), pltpu.VMEM((st // 2, nt, 2 * LANES), BF16),
                        pltpu.VMEM((gpl, nt, tw), BF16),
                        pltpu.VMEM((n, 2 * S5_STATE), F32), pltpu.VMEM((n, 2 * S5_STATE), F32)],
        compiler_params=_cparams(1),
        name="s5",
    )(p0, t_mat, w_mat, v_mat, apow)


def _post_mix(y, x_ref, mod_ref, ln_ref, rw_ref, xo_ref, xp_ref, lg_ref, row0, n_lat):
    tm = y.shape[0]
    g1 = _select_rows(mod_ref, 2, row0, tm, n_lat)
    xn = _layer_norm(DEEPNORM_ALPHA * x_ref[...] + g1 * y, ln_ref[0:1, :], ln_ref[1:2, :])
    xo_ref[...] = xn
    sc2 = _select_rows(mod_ref, 4, row0, tm, n_lat)
    sh2 = _select_rows(mod_ref, 3, row0, tm, n_lat)
    x2 = xn * (1.0 + sc2) + sh2
    xp_ref[...] = _pack_bf16_pairs(x2)
    xh, xl = _split(x2)
    rh, rl = _split(rw_ref[...])
    lg_ref[...] = _dot_nt(rh, xh) + _dot_nt(rh, xl) + _dot_nt(rl, xh)


def _finish0_kernel(r_ref, g_ref, s_ref, x_ref, mod_ref, ln_ref, wglu_ref, bglu_ref, wout_ref, rw_ref,
                    xo_ref, xp_ref, lg_ref, *, tm, n_lat):
    row0 = pl.program_id(0) * tm
    ret = (r_ref[...].astype(F32) * _silu(g_ref[...].astype(F32))).astype(BF16)
    z = _gelu_tanh(s_ref[...].astype(F32))
    zb = z.astype(BF16)
    gate = _sigmoid(_dot(zb, wglu_ref[...]) + bglu_ref[...])
    s5o = (z * gate).astype(BF16)
    half = ret.shape[1]
    y = _dot(ret, wout_ref[0:half, :]) + _dot(s5o, wout_ref[half:, :])
    _post_mix(y, x_ref, mod_ref, ln_ref, rw_ref, xo_ref, xp_ref, lg_ref, row0, n_lat)


def _const_spec(shape):
    nd = len(shape)
    return pl.BlockSpec(shape, lambda i: (0,) * nd)


def _finish0(rn, p0, s5y, x_all, mod, ln, wglu, bglu, wout, rw, n_lat):
    m, d = x_all.shape
    tm = ROW_TILE
    half = d // 2
    outs = pl.pallas_call(
        functools.partial(_finish0_kernel, tm=tm, n_lat=n_lat),
        grid=(m // tm,),
        in_specs=[
            pl.BlockSpec((tm, half), lambda i: (i, 0)),
            pl.BlockSpec((tm, half), lambda i: (i, 3)),
            pl.BlockSpec((tm, half), lambda i: (i, 0)),
            pl.BlockSpec((tm, d), lambda i: (i, 0)),
            _const_spec((2, 6, d)),
            _const_spec((2, d)),
            _const_spec((half, half)),
            _const_spec((1, half)),
            _const_spec((d, d)),
            _const_spec((N_EXPERTS, d)),
        ],
        out_specs=[
            pl.BlockSpec((tm, d), lambda i: (i, 0)),
            pl.BlockSpec((tm, half), lambda i: (i, 0)),
            pl.BlockSpec((N_EXPERTS, tm), lambda i: (0, i)),
        ],
        out_shape=[
            jax.ShapeDtypeStruct((m, d), F32),
            jax.ShapeDtypeStruct((m, half), U32),
            jax.ShapeDtypeStruct((N_EXPERTS, m), F32),
        ],
        compiler_params=_cparams(1),
        name="finish0",
    )(rn, p0, s5y, x_all, mod, ln, wglu, bglu, wout, rw)
    return outs


def _finish1_kernel(og_ref, od_ref, x_ref, mod_ref, ln_ref, wout_ref, rw_ref, xo_ref, xp_ref, lg_ref, *, tm, n_lat):
    row0 = pl.program_id(0) * tm
    half = og_ref.shape[1]
    y = _dot(og_ref[...], wout_ref[0:half, :]) + _dot(od_ref[...], wout_ref[half:, :])
    _post_mix(y, x_ref, mod_ref, ln_ref, rw_ref, xo_ref, xp_ref, lg_ref, row0, n_lat)


def _finish1(og, od, x_all, mod, ln, wout, rw, n_lat):
    d = x_all.shape[1]
    m = og.shape[0]
    tm = ROW_TILE
    half = d // 2
    return pl.pallas_call(
        functools.partial(_finish1_kernel, tm=tm, n_lat=n_lat),
        grid=(m // tm,),
        in_specs=[
            pl.BlockSpec((tm, half), lambda i: (i, 0)),
            pl.BlockSpec((tm, half), lambda i: (i, 0)),
            pl.BlockSpec((tm, d), lambda i: (i, 0)),
            _const_spec((2, 6, d)),
            _const_spec((2, d)),
            _const_spec((d, d)),
            _const_spec((N_EXPERTS, d)),
        ],
        out_specs=[
            pl.BlockSpec((tm, d), lambda i: (i, 0)),
            pl.BlockSpec((tm, half), lambda i: (i, 0)),
            pl.BlockSpec((N_EXPERTS, tm), lambda i: (0, i)),
        ],
        out_shape=[
            jax.ShapeDtypeStruct((m, d), F32),
            jax.ShapeDtypeStruct((m, half), U32),
            jax.ShapeDtypeStruct((N_EXPERTS, m), F32),
        ],
        compiler_params=_cparams(1),
        name="finish1",
    )(og, od, x_all, mod, ln, wout, rw)


def _route_kernel(lg_ref, bias_ref, e_ref, r_ref, w_ref, cnt_ref, base_ref, blk_ref, carry_ref, *, tb):
    ne = N_EXPERTS
    gsz = ne // N_EXPERT_GROUPS
    neg = -jnp.inf

    @pl.when(pl.program_id(0) == 0)
    def _():
        carry_ref[...] = jnp.zeros_like(carry_ref)

    s = _sigmoid(lg_ref[...])
    sel = s + bias_ref[...]
    gs = []
    for g in range(N_EXPERT_GROUPS):
        blk = sel[g * gsz:(g + 1) * gsz, :]
        m1 = jnp.max(blk, axis=0, keepdims=True)
        n_eq = jnp.sum(jnp.where(blk == m1, 1.0, 0.0), axis=0, keepdims=True)
        m2 = jnp.max(jnp.where(blk < m1, blk, neg), axis=0, keepdims=True)
        gs.append(m1 + jnp.where(n_eq >= 2.0, m1, m2))
    masked = []
    for g in range(N_EXPERT_GROUPS):
        ahead = jnp.zeros_like(gs[g])
        for h in range(N_EXPERT_GROUPS):
            if h == g:
                continue
            beats = ((gs[h] > gs[g]) | (gs[h] == gs[g])) if h < g else (gs[h] > gs[g])
            ahead = ahead + jnp.where(beats, 1.0, 0.0)
        masked.append(jnp.where(ahead < float(TOPK_GROUPS), sel[g * gsz:(g + 1) * gsz, :], neg))
    selm = jnp.concatenate(masked, axis=0)
    eid = lax.broadcasted_iota(I32, (ne, 1), 0)
    ahead = jnp.zeros_like(selm)
    for e in range(ne):
        row = selm[e:e + 1, :]
        beats = (row > selm) | ((row == selm) & (eid > e))
        ahead = ahead + jnp.where(beats, 1.0, 0.0)
    chosen = ahead < float(TOP_K)
    member = jnp.where(chosen, 1.0, 0.0)
    ssel = jnp.where(chosen, s, 0.0)
    wd = ssel / jnp.sum(ssel, axis=0, keepdims=True) * ROUTED_SCALE
    mb = member.astype(BF16)
    ti = lax.broadcasted_iota(I32, (tb, tb), 0)
    tj = lax.broadcasted_iota(I32, (tb, tb), 1)
    tri = jnp.where(ti < tj, 1.0, 0.0).astype(BF16)
    rank = carry_ref[:, 0:1] + _dot(mb, tri)
    in_block = jnp.sum(member, axis=1, keepdims=True)
    base_ref[0] = carry_ref[...]
    blk_ref[0] = jnp.zeros_like(carry_ref) + in_block
    carry_ref[...] = carry_ref[...] + in_block
    cnt_ref[...] = carry_ref[...]
    ei = lax.broadcasted_iota(I32, (ne, ne), 0)
    ej = lax.broadcasted_iota(I32, (ne, ne), 1)
    low = jnp.where(ej < ei, 1.0, 0.0).astype(BF16)
    slot = _dot(low, mb)
    eidf = eid.astype(F32)
    for k in range(TOP_K):
        hit = chosen & (slot == float(k))
        e_ref[k:k + 1, :] = jnp.sum(jnp.where(hit, eidf, 0.0), axis=0, keepdims=True).astype(I32)
        r_ref[k:k + 1, :] = jnp.sum(jnp.where(hit, rank, 0.0), axis=0, keepdims=True).astype(I32)
        w_ref[k:k + 1, :] = jnp.sum(jnp.where(hit, wd, 0.0), axis=0, keepdims=True)


def _route(logits_t, router_bias, tm):
    e, t = logits_t.shape
    tb = ROW_TILE
    eidx, rank, wts, counts, base, in_block = pl.pallas_call(
        functools.partial(_route_kernel, tb=tb),
        grid=(t // tb,),
        in_specs=[
            pl.BlockSpec((e, tb), lambda i: (0, i)),
            pl.BlockSpec((e, 1), lambda i: (0, 0)),
        ],
        out_specs=[
            pl.BlockSpec((TOP_K, tb), lambda i: (0, i)),
            pl.BlockSpec((TOP_K, tb), lambda i: (0, i)),
            pl.BlockSpec((TOP_K, tb), lambda i: (0, i)),
            pl.BlockSpec((e, LANES), lambda i: (0, 0)),
            pl.BlockSpec((1, e, LANES), lambda i: (i, 0, 0)),
            pl.BlockSpec((1, e, LANES), lambda i: (i, 0, 0)),
        ],
        out_shape=[
            jax.ShapeDtypeStruct((TOP_K, t), I32),
            jax.ShapeDtypeStruct((TOP_K, t), I32),
            jax.ShapeDtypeStruct((TOP_K, t), F32),
            jax.ShapeDtypeStruct((e, LANES), F32),
            jax.ShapeDtypeStruct((t // tb, e, LANES), F32),
            jax.ShapeDtypeStruct((t // tb, e, LANES), F32),
        ],
        scratch_shapes=[pltpu.VMEM((e, LANES), F32)],
        compiler_params=_cparams(1),
        name="moe_route",
    )(logits_t, router_bias.astype(F32).reshape(e, 1))
    counts = counts[:, 0].astype(I32)
    padded = (counts + tm - 1) // tm * tm
    pad_end = jnp.cumsum(padded)
    pad_start = pad_end - padded
    ids = jnp.arange(e, dtype=I32)
    start_of = jnp.sum(jnp.where(eidx[:, :, None] == ids, pad_start, 0), axis=-1)
    dest = (start_of + rank).T
    n_blocks = (t * TOP_K + e * (tm - 1)) // tm
    blk_start = jnp.arange(n_blocks, dtype=I32) * tm
    blk_expert = jnp.minimum(jnp.sum((blk_start[:, None] >= pad_end[None, :]).astype(I32), axis=1), e - 1)
    n_used = (pad_end[-1] // tm).astype(I32).reshape(1)
    plan = _combine_plan(dest, eidx.T, base[:, :, 0].astype(I32), in_block[:, :, 0].astype(I32), pad_start, tb)
    return dest, wts.T, blk_expert, n_used, n_blocks, pad_end.astype(I32), padded.astype(I32), plan


def _dispatch_kernel(pad_end_ref, padded_ref, dest_ref, x_ref, xs_ref, zero_ref, sem, zsem, *, tb, tm):
    @pl.when(pl.program_id(0) == 0)
    def _():
        zero_ref[...] = jnp.zeros_like(zero_ref)

        def fill(e):
            first = pl.multiple_of(pad_end_ref[e] - tm, tm)
            return pltpu.make_async_copy(zero_ref, xs_ref.at[pl.ds(first, tm)], zsem)

        for e in range(N_EXPERTS):
            @pl.when(padded_ref[e] > 0)
            def _():
                fill(e).start()
        for e in range(N_EXPERTS):
            @pl.when(padded_ref[e] > 0)
            def _():
                fill(e).wait()

    def row_copy(r, k):
        d = dest_ref[0, 0, r * TOP_K + k]
        return pltpu.make_async_copy(x_ref.at[pl.ds(r, 1)], xs_ref.at[pl.ds(d, 1)], sem)

    def start(r, carry):
        for k in range(TOP_K):
            row_copy(r, k).start(priority=k % 2)
        return carry

    def wait(r, carry):
        for k in range(TOP_K):
            row_copy(r, k).wait()
        return carry

    lax.fori_loop(0, tb, start, 0)
    lax.fori_loop(0, tb, wait, 0)


def _dispatch(xp, dest, pad_end, padded, n_pad, tm):
    t, w = xp.shape
    tb = ROW_TILE
    smem = pl.BlockSpec(memory_space=pltpu.SMEM)
    return pl.pallas_call(
        functools.partial(_dispatch_kernel, tb=tb, tm=tm),
        grid=(t // tb,),
        in_specs=[
            smem,
            smem,
            pl.BlockSpec((1, 1, tb * TOP_K), lambda i: (i, 0, 0), memory_space=pltpu.SMEM),
            pl.BlockSpec((tb, w), lambda i: (i, 0)),
        ],
        out_specs=pl.BlockSpec(memory_space=pl.ANY),
        out_shape=jax.ShapeDtypeStruct((n_pad, w), U32),
        scratch_shapes=[pltpu.VMEM((tm, w), U32), pltpu.SemaphoreType.DMA(()), pltpu.SemaphoreType.DMA(())],
        compiler_params=_cparams(1),
        name="moe_dispatch",
    )(pad_end, padded, dest.reshape(t // tb, 1, tb * TOP_K), xp)


def _expert_kernel(be_ref, nu_ref, xs_ref, wg_ref, wu_ref, wd_ref, y_ref, wgb_ref, wub_ref, wdb_ref):
    b = pl.program_id(0)
    e = be_ref[b]
    prev = be_ref[jnp.maximum(b - 1, 0)]

    @pl.when((b == 0) | (e != prev))
    def _():
        wgb_ref[...] = wg_ref[0, 0].astype(BF16)
        wub_ref[...] = wu_ref[0, 0].astype(BF16)
        wdb_ref[...] = wd_ref[0, 0].astype(BF16)

    @pl.when(b < nu_ref[0])
    def _():
        lo, hi = _unpack_bf16_pairs(xs_ref[...])
        lo = lo.astype(BF16)
        hi = hi.astype(BF16)
        half = lo.shape[1]
        gate = _dot(lo, wgb_ref[0:half, :]) + _dot(hi, wgb_ref[half:, :])
        up = _dot(lo, wub_ref[0:half, :]) + _dot(hi, wub_ref[half:, :])
        hid = (_silu(gate) * up).astype(BF16)
        y_ref[...] = _dot(hid, wdb_ref[...]).astype(BF16)

    @pl.when(b >= nu_ref[0])
    def _():
        y_ref[...] = jnp.zeros_like(y_ref)


def _experts(xs, blk_expert, n_used, w_gate, w_up, w_down, layer):
    n_pad, w = xs.shape
    tm = MOE_TM
    d, hdn = w_gate.shape[2], w_gate.shape[3]
    return pl.pallas_call(
        _expert_kernel,
        grid_spec=pltpu.PrefetchScalarGridSpec(
            num_scalar_prefetch=2,
            grid=(n_pad // tm,),
            in_specs=[
                pl.BlockSpec((tm, w), lambda b, be, nu: (b, 0)),
                pl.BlockSpec((1, 1, d, hdn), lambda b, be, nu: (layer, be[b], 0, 0)),
                pl.BlockSpec((1, 1, d, hdn), lambda b, be, nu: (layer, be[b], 0, 0)),
                pl.BlockSpec((1, 1, hdn, d), lambda b, be, nu: (layer, be[b], 0, 0)),
            ],
            out_specs=pl.BlockSpec((tm, d), lambda b, be, nu: (b, 0)),
            scratch_shapes=[pltpu.VMEM((d, hdn), BF16), pltpu.VMEM((d, hdn), BF16), pltpu.VMEM((hdn, d), BF16)],
        ),
        out_shape=jax.ShapeDtypeStruct((n_pad, d), BF16),
        compiler_params=_cparams(1),
        name="moe_experts",
    )(blk_expert, n_used, xs, w_gate, w_up, w_down)


def _combine_kernel(tot_ref, src_ref, xp_ref, ws_ref, col_ref, x_ref, mod_ref, ln_ref, sg_ref, su_ref, sd_ref,
                    y_hbm, xo_ref, rbuf, sem, *, tb, n_lat):
    i = pl.program_id(0)
    row0 = i * tb
    ch = COMBINE_CHUNK
    n_rows = rbuf.shape[0]

    @pl.when(i == 0)
    def _():
        rbuf[...] = jnp.zeros_like(rbuf)

    total = tot_ref[i]

    def chunk_copy(c):
        first = pl.multiple_of(src_ref[0, 0, c] * ch, ch)
        return pltpu.make_async_copy(y_hbm.at[pl.ds(first, ch)], rbuf.at[pl.ds(pl.multiple_of(c * ch, ch), ch)], sem)

    def start(c, carry):
        chunk_copy(c).start()
        return carry

    def wait(c, carry):
        chunk_copy(c).wait()
        return carry

    lax.fori_loop(0, total, start, 0)
    lo, hi = _unpack_bf16_pairs(xp_ref[...])
    lo = lo.astype(BF16)
    hi = hi.astype(BF16)
    half = lo.shape[1]
    gate = _dot(lo, sg_ref[0:half, :]) + _dot(hi, sg_ref[half:, :])
    up = _dot(lo, su_ref[0:half, :]) + _dot(hi, su_ref[half:, :])
    f = _dot((_silu(gate) * up).astype(BF16), sd_ref[...])
    cols = col_ref[...]
    ws = ws_ref[...]
    piece = COMBINE_PIECE
    pieces = []
    for p in range(n_rows // piece):
        pos = p * piece + lax.broadcasted_iota(I32, (tb, piece), 1)
        g = jnp.zeros((tb, piece), F32)
        for k in range(TOP_K):
            g = jnp.where(pos == cols[:, k:k + 1], ws[:, k:k + 1], g)
        pieces.append(g.astype(BF16))
    lax.fori_loop(0, total, wait, 0)
    for p in range(n_rows // piece):
        f = f + _dot(pieces[p], rbuf[p * piece:(p + 1) * piece, :])
    g2 = _select_rows(mod_ref, 5, row0, tb, n_lat)
    xo_ref[...] = _layer_norm(DEEPNORM_ALPHA * x_ref[...] + g2 * f, ln_ref[0:1, :], ln_ref[1:2, :])


def _combine_plan(dest, eidx, base, cnt, pad_start, tb):
    t, k = dest.shape
    e = pad_start.shape[0]
    ch = COMBINE_CHUNK
    nb = t // tb
    cmax = COMBINE_ROWS // ch
    run0 = pad_start[None, :] + base
    first = run0 // ch
    last = (run0 + cnt - 1) // ch
    nch = jnp.where(cnt > 0, last - first + 1, 0)
    end = jnp.cumsum(nch, axis=1)
    off = end - nch
    total = end[:, -1].astype(I32)
    c = jnp.arange(cmax, dtype=I32)
    e_of_c = jnp.minimum(jnp.sum((end[:, None, :] <= c[None, :, None]).astype(I32), axis=-1), e - 1)
    onehot_c = e_of_c[:, :, None] == jnp.arange(e, dtype=I32)
    src = jnp.sum(jnp.where(onehot_c, (first - off)[:, None, :], 0), axis=-1) + c[None, :]
    src = jnp.where(c[None, :] < total[:, None], src, 0).astype(I32)
    onehot_t = eidx.reshape(nb, tb, k)[..., None] == jnp.arange(e, dtype=I32)
    shift = jnp.sum(jnp.where(onehot_t, ((off - first) * ch)[:, None, None, :], 0), axis=-1)
    col = (dest.reshape(nb, tb, k) + shift).reshape(t, k).astype(I32)
    return total, src.reshape(nb, 1, cmax), col


def _combine(plan, xp, wsel, x_all, mod, ln, sg, su, sd, y, n_lat):
    t, w = xp.shape
    d = x_all.shape[1]
    tb = ROW_TILE
    hdn = sg.shape[1]
    total, src, col = plan
    return pl.pallas_call(
        functools.partial(_combine_kernel, tb=tb, n_lat=n_lat),
        grid=(t // tb,),
        in_specs=[
            pl.BlockSpec(memory_space=pltpu.SMEM),
            pl.BlockSpec((1, 1, src.shape[2]), lambda i: (i, 0, 0), memory_space=pltpu.SMEM),
            pl.BlockSpec((tb, w), lambda i: (i, 0)),
            pl.BlockSpec((tb, TOP_K), lambda i: (i, 0)),
            pl.BlockSpec((tb, TOP_K), lambda i: (i, 0)),
            pl.BlockSpec((tb, d), lambda i: (i, 0)),
            _const_spec((2, 6, d)),
            _const_spec((2, d)),
            _const_spec((d, hdn)),
            _const_spec((d, hdn)),
            _const_spec((hdn, d)),
            pl.BlockSpec(memory_space=pl.ANY),
        ],
        out_specs=pl.BlockSpec((tb, d), lambda i: (i, 0)),
        out_shape=jax.ShapeDtypeStruct((t, d), F32),
        scratch_shapes=[pltpu.VMEM((COMBINE_ROWS, d), BF16), pltpu.SemaphoreType.DMA(())],
        compiler_params=_cparams(1),
        name="moe_combine",
    )(total, src, xp, wsel, col, x_all, mod, ln, sg, su, sd, y)


def _moe(xp, logits, x_res, mod, ln, router_bias, w_gate, w_up, w_down, layer, sg, su, sd, n_lat):
    dest, wsel, blk_expert, n_used, n_blocks, pad_end, padded, plan = _route(logits, router_bias, MOE_TM)
    xs = _dispatch(xp, dest, pad_end, padded, n_blocks * MOE_TM, MOE_TM)
    y = _experts(xs, blk_expert, n_used, w_gate, w_up, w_down, layer)
    return _combine(plan, xp, wsel, x_res, mod, ln, sg.astype(BF16), su.astype(BF16), sd.astype(BF16), y, n_lat)


def _rot_lanes(xs, cos, sin):
    q = HEAD_DIM // 4
    lane = lax.broadcasted_iota(I32, xs.shape, 1)
    partner = jnp.where((lane % (2 * q)) < q, pltpu.roll(xs, HEAD_DIM - q, 1), pltpu.roll(xs, q, 1))
    return xs * cos + partner * sin


def _projq_kernel(x_ref, mod_ref, w_ref, cos_ref, sin_ref, o_ref, xm_ref, *, tm, n_lat):
    i = pl.program_id(0)
    j = pl.program_id(1)

    @pl.when(j == 0)
    def _():
        xm_ref[...] = _modulate_rows(x_ref[...], mod_ref, 0, 1, i * tm, n_lat).astype(BF16)

    acc = _dot(xm_ref[...], w_ref[...])
    n_heads = acc.shape[1] // HEAD_DIM

    @pl.when(j == 0)
    def _():
        for c in range(n_heads):
            xs = acc[:, c * HEAD_DIM:(c + 1) * HEAD_DIM]
            xs = xs * lax.rsqrt(jnp.mean(xs * xs, axis=-1, keepdims=True) + LN_EPS)
            o_ref[:, c * HEAD_DIM:(c + 1) * HEAD_DIM] = _rot_lanes(xs, cos_ref[0], sin_ref[0]).astype(BF16)

    @pl.when(j == 1)
    def _():
        for c in range(n_heads):
            xs = acc[:, c * HEAD_DIM:(c + 1) * HEAD_DIM]
            o_ref[:, c * HEAD_DIM:(c + 1) * HEAD_DIM] = _rot_lanes(xs, cos_ref[1], sin_ref[1]).astype(BF16)


def _projq(x_all, mod, w_bf, cos2, sin2, n_lat):
    d = x_all.shape[1]
    n = w_bf.shape[1]
    tm = 512 if n_lat % 512 == 0 else 256
    tn = n // 2
    return pl.pallas_call(
        functools.partial(_projq_kernel, tm=tm, n_lat=n_lat),
        grid=(n_lat // tm, 2),
        in_specs=[
            pl.BlockSpec((tm, d), lambda i, j: (i, 0)),
            pl.BlockSpec((2, 6, d), lambda i, j: (0, 0, 0)),
            pl.BlockSpec((d, tn), lambda i, j: (0, j)),
            pl.BlockSpec((2, tm, HEAD_DIM), lambda i, j: (0, i, 0)),
            pl.BlockSpec((2, tm, HEAD_DIM), lambda i, j: (0, i, 0)),
        ],
        out_specs=pl.BlockSpec((tm, tn), lambda i, j: (i, j)),
        out_shape=jax.ShapeDtypeStruct((n_lat, n), BF16),
        scratch_shapes=[pltpu.VMEM((tm, d), BF16)],
        compiler_params=_cparams(2),
        name="proj1_q",
    )(x_all, mod, w_bf, cos2, sin2)


def _projkv_kernel(x_ref, mod_ref, wv_ref, wkt_ref, cos_ref, sin_ref, v_ref, kt_ref, *, tm, n_lat):
    i = pl.program_id(0)
    xm = _modulate_rows(x_ref[...], mod_ref, 0, 1, i * tm, n_lat).astype(BF16)
    v_ref[...] = _dot(xm, wv_ref[...]).astype(BF16)
    kt = _dot_nt(wkt_ref[...], xm)
    q = HEAD_DIM // 4
    n_gk = GQA_KV_HEADS * HEAD_DIM
    for c in range(kt.shape[0] // HEAD_DIM):
        xs = kt[c * HEAD_DIM:(c + 1) * HEAD_DIM, :]
        t = 0 if c * HEAD_DIM < n_gk else 1
        if t == 0:
            xs = xs * lax.rsqrt(jnp.mean(xs * xs, axis=0, keepdims=True) + LN_EPS)
        partner = jnp.concatenate([xs[q:2 * q], xs[0:q], xs[3 * q:4 * q], xs[2 * q:3 * q]], axis=0)
        kt_ref[c * HEAD_DIM:(c + 1) * HEAD_DIM, :] = (xs * cos_ref[t] + partner * sin_ref[t]).astype(BF16)


def _projkv(x_all, mod, wv_bf, wkt_bf, cos_t, sin_t, n_lat):
    m, d = x_all.shape
    nv = wv_bf.shape[1]
    nk = wkt_bf.shape[0]
    tm = 640 if m % 640 == 0 else 256
    return pl.pallas_call(
        functools.partial(_projkv_kernel, tm=tm, n_lat=n_lat),
        grid=(m // tm,),
        in_specs=[
            pl.BlockSpec((tm, d), lambda i: (i, 0)),
            _const_spec((2, 6, d)),
            _const_spec((d, nv)),
            _const_spec((nk, d)),
            pl.BlockSpec((2, HEAD_DIM, tm), lambda i: (0, 0, i)),
            pl.BlockSpec((2, HEAD_DIM, tm), lambda i: (0, 0, i)),
        ],
        out_specs=[
            pl.BlockSpec((tm, nv), lambda i: (i, 0)),
            pl.BlockSpec((nk, tm), lambda i: (0, i)),
        ],
        out_shape=[jax.ShapeDtypeStruct((m, nv), BF16), jax.ShapeDtypeStruct((nk, m), BF16)],
        compiler_params=_cparams(1),
        name="proj1_kv",
    )(x_all, mod, wv_bf, wkt_bf, cos_t, sin_t)


def _flash(qs, kt_ref, v_ref, scratch, tk, stack):
    (qs_ref, s_refs, pm_refs, m_ref, l_ref, acc_ref) = scratch
    rows = qs_ref.shape[0]
    lk = v_ref.shape[0]
    n = lk // tk
    rs = rows // stack
    part = rows // len(qs)
    for a, q_part in enumerate(qs):
        qs_ref[a * part:(a + 1) * part, :] = q_part
    m_ref[...] = jnp.full(m_ref.shape, -jnp.inf, F32)
    l_ref[...] = jnp.zeros(l_ref.shape, F32)
    acc_ref[...] = jnp.zeros(acc_ref.shape, F32)

    n_grp = tk // LANES
    dv = acc_ref.shape[1]
    rb = FLASH_BLOCK if rows % FLASH_BLOCK == 0 else rows
    n_rb = rows // rb
    ps = min(rb, rs)

    def score_rows(j, r, s_ref, pm_ref):
        off = pl.multiple_of(j * tk, tk)
        for piece in range(rb // ps):
            r2 = pl.multiple_of(r + piece * ps, ps)
            ka = 0 if stack == 1 else pl.multiple_of((r2 // rs) * HEAD_DIM, HEAD_DIM)
            s = _dot(qs_ref[pl.ds(r2, ps), :], kt_ref[pl.ds(ka, HEAD_DIM), pl.ds(off, tk)])
            s_ref[pl.ds(r2, ps), :] = s
            pm = s[:, 0:LANES]
            for c in range(1, n_grp):
                pm = jnp.maximum(pm, s[:, c * LANES:(c + 1) * LANES])
            pm_ref[pl.ds(r2, ps), :] = pm

    sb = FLASH_SUB if rb % FLASH_SUB == 0 else rb

    def attend_rows(j, r, s_ref, pm_ref):
        off = pl.multiple_of(j * tk, tk)
        blk = pl.ds(r, sb)
        m_prev = m_ref[blk, :]
        m_next = jnp.maximum(m_prev, jnp.max(pm_ref[blk, :], axis=1, keepdims=True))
        alpha = jnp.exp2(m_prev - m_next)
        m_ref[blk, :] = m_next
        pieces = []
        for q in range(sb // FLASH_ROWS):
            sub = pl.ds(pl.multiple_of(r + q * FLASH_ROWS, FLASH_ROWS), FLASH_ROWS)
            mb = m_next[q * FLASH_ROWS:(q + 1) * FLASH_ROWS]
            lsum = alpha[q * FLASH_ROWS:(q + 1) * FLASH_ROWS] * l_ref[sub, :]
            row = []
            for c in range(n_grp):
                pc = jnp.exp2(s_ref[sub, c * LANES:(c + 1) * LANES] - mb)
                lsum = lsum + pc
                row.append(pc.astype(BF16))
            l_ref[sub, :] = lsum
            pieces.append(jnp.concatenate(row, axis=1))
        pv = _dot(jnp.concatenate(pieces, axis=0), v_ref[pl.ds(off, tk), :])
        for c in range(dv // LANES):
            cols = slice(c * LANES, (c + 1) * LANES)
            acc_ref[blk, cols] = alpha * acc_ref[blk, cols] + pv[:, cols]

    def step_parity(j, cur, with_scores):
        nxt = 1 - cur

        def body(i, carry):
            r = pl.multiple_of(i * rb, rb)
            if with_scores:
                score_rows(j + 1, r, s_refs[nxt], pm_refs[nxt])
            for h in range(rb // sb):
                attend_rows(j, pl.multiple_of(r + h * sb, sb), s_refs[cur], pm_refs[cur])
            return carry

        lax.fori_loop(0, n_rb, body, 0)

    def first_scores(i, carry):
        score_rows(0, pl.multiple_of(i * rb, rb), s_refs[0], pm_refs[0])
        return carry

    lax.fori_loop(0, n_rb, first_scores, 0)

    def pair(i, carry):
        j = 2 * i
        step_parity(j, 0, True)
        step_parity(j + 1, 1, True)
        return carry

    lax.fori_loop(0, (n - 1) // 2, pair, 0)
    if (n - 1) % 2:
        step_parity(n - 2, (n - 2) % 2, True)
    step_parity(n - 1, (n - 1) % 2, False)
    return acc_ref[...] / jnp.sum(l_ref[...], axis=1, keepdims=True)


def _flash_scratch(rows, tk, dv):
    stat = pltpu.VMEM((rows, LANES), F32)
    return [pltpu.VMEM((rows, HEAD_DIM), BF16),
            pltpu.VMEM((rows, tk), F32), pltpu.VMEM((rows, tk), F32), stat, stat,
            stat, stat, pltpu.VMEM((rows, dv), F32)]


def _unpack_flash_scratch(refs):
    qs_ref, s0, s1, pm0, pm1, m_ref, l_ref, acc_ref = refs
    return (qs_ref, (s0, s1), (pm0, pm1), m_ref, l_ref, acc_ref)


def _gqa_kernel(q_ref, kt_ref, v_ref, o_ref, *scratch, tk):
    tq = q_ref.shape[0]
    qs = [q_ref[:, a * HEAD_DIM:(a + 1) * HEAD_DIM] for a in range(GQA_GROUP)]
    o = _flash(qs, kt_ref, v_ref, _unpack_flash_scratch(scratch), tk, 1)
    for a in range(GQA_GROUP):
        o_ref[:, a * HEAD_DIM:(a + 1) * HEAD_DIM] = o[a * tq:(a + 1) * tq].astype(BF16)


def _diff_kernel(lam_ref, q_ref, kt_ref, v_ref, g_ref, o_ref, *scratch, tk, out_scale):
    tq = q_ref.shape[0]
    qs = [q_ref[:, 0:HEAD_DIM], q_ref[:, HEAD_DIM:2 * HEAD_DIM]]
    o = _flash(qs, kt_ref, v_ref, _unpack_flash_scratch(scratch), tk, 2)
    od = o[0:tq] - lam_ref[0] * o[tq:2 * tq]
    od = od * lax.rsqrt(jnp.mean(od * od, axis=-1, keepdims=True) + LN_EPS)
    o_ref[...] = (od * g_ref[...] * out_scale).astype(BF16)


def _key_chunk(lk):
    for tk in (1280, 1024, 640, 512, 256, 128):
        if lk % tk == 0 and lk // tk >= 2:
            return tk
    return lk


def _attention(q1, kt, v1, lam, subln_g, lambda_init, n_lat):
    lk = v1.shape[0]
    tk = _key_chunk(lk)
    tq = 256
    gw = GQA_GROUP * HEAD_DIM
    n_gv = GQA_KV_HEADS * HEAD_DIM
    once = pl.Buffered(1)
    og = pl.pallas_call(
        functools.partial(_gqa_kernel, tk=tk),
        grid=(GQA_KV_HEADS, n_lat // tq),
        in_specs=[
            pl.BlockSpec((tq, gw), lambda g, i: (i, g)),
            pl.BlockSpec((HEAD_DIM, lk), lambda g, i: (g, 0), pipeline_mode=once),
            pl.BlockSpec((lk, HEAD_DIM), lambda g, i: (0, g), pipeline_mode=once),
        ],
        out_specs=pl.BlockSpec((tq, gw), lambda g, i: (i, g)),
        out_shape=jax.ShapeDtypeStruct((n_lat, GQA_HEADS * HEAD_DIM), BF16),
        scratch_shapes=_flash_scratch(GQA_GROUP * tq, tk, HEAD_DIM),
        compiler_params=_cparams(2),
        name="gqa_attention",
    )(q1, kt, v1)
    q_off = GQA_HEADS * HEAD_DIM // DIFF_V_DIM
    kv_off = n_gv // DIFF_V_DIM
    tq = 512 if n_lat % 512 == 0 else 256
    od = pl.pallas_call(
        functools.partial(_diff_kernel, tk=tk, out_scale=1.0 - lambda_init),
        grid=(DIFF_HEADS, n_lat // tq),
        in_specs=[
            pl.BlockSpec(memory_space=pltpu.SMEM),
            pl.BlockSpec((tq, DIFF_V_DIM), lambda h, i: (i, q_off + h)),
            pl.BlockSpec((DIFF_V_DIM, lk), lambda h, i: (kv_off + h, 0), pipeline_mode=once),
            pl.BlockSpec((lk, DIFF_V_DIM), lambda h, i: (0, kv_off + h), pipeline_mode=once),
            pl.BlockSpec((1, DIFF_V_DIM), lambda h, i: (0, 0)),
        ],
        out_specs=pl.BlockSpec((tq, DIFF_V_DIM), lambda h, i: (i, h)),
        out_shape=jax.ShapeDtypeStruct((n_lat, DIFF_HEADS * DIFF_V_DIM), BF16),
        scratch_shapes=_flash_scratch(2 * tq, tk, DIFF_V_DIM),
        compiler_params=_cparams(2),
        name="diff_attention",
    )(lam, q1, kt, v1, subln_g.reshape(1, DIFF_V_DIM).astype(F32))
    return og, od


def _router_operand(router_w):
    return router_w.astype(F32).T


def kernel(x, c, ctx, c_ctx, mod_w, mod_b, ln_g, ln_b, ev_w_in, ev_w_out, ret_log_decay, s5_a_re, s5_a_im, s5_log_dt, s5_b_re, s5_b_im, s5_c_re, s5_c_im, s5_d, s5_w_glu, s5_b_glu, od_w_in, od_w_out, qk_norm_g, diff_lambda, diff_subln_g, router_w, router_bias, exp_w_gate, exp_w_up, exp_w_down, sh_w_gate, sh_w_up, sh_w_down):
    b_, n_lat, d = x.shape
    n_ctx = ctx.shape[1]
    assert b_ == 1 and d == D_MODEL and n_lat % RET_CHUNK == 0 and n_ctx % RET_CHUNK == 0
    m_all = n_lat + n_ctx
    x_all = jnp.concatenate([x[0], ctx[0]], axis=0)
    mods = _modulation(c, c_ctx, mod_w, mod_b)

    cos0, sin0 = _rope_tables(n_lat, n_ctx, RET_DIM)
    p0 = _proj0(x_all, mods[0], ev_w_in[0].astype(BF16), cos0, sin0, n_lat)
    rn = _retention(p0, ret_log_decay[0], n_lat, n_ctx)
    s5_tabs = functools.partial(_s5_tables, s5_a_re[0], s5_a_im[0], s5_log_dt[0], s5_b_re[0], s5_b_im[0],
                                s5_c_re[0], s5_c_im[0], s5_d[0])
    s5y = _s5(p0, 4 * RET_HEADS * RET_DIM, s5_tabs, n_lat, n_ctx)
    ln0 = jnp.stack([ln_g[0], ln_b[0]], axis=1)
    x_all, xp, logits = _finish0(rn, p0, s5y, x_all, mods[0], ln0[0], s5_w_glu[0].astype(BF16),
                                 s5_b_glu[0].reshape(1, -1).astype(F32), ev_w_out[0].astype(BF16),
                                 _router_operand(router_w[0]), n_lat)
    x_all = _moe(xp, logits, x_all, mods[0], ln0[1], router_bias[0], exp_w_gate, exp_w_up, exp_w_down, 0,
                 sh_w_gate[0], sh_w_up[0], sh_w_down[0], n_lat)

    i = 1
    lambda_init = 0.8 - 0.6 * math.exp(-0.3 * i)
    gq_w = GQA_HEADS * HEAD_DIM
    gk_w = GQA_KV_HEADS * HEAD_DIM
    dq_w = DIFF_HEADS * 2 * HEAD_DIM
    cuts = [gq_w, gq_w + gk_w, gq_w + 2 * gk_w, gq_w + 2 * gk_w + dq_w, gq_w + 2 * gk_w + 2 * dq_w]
    w_in = od_w_in[0]
    w_gq, w_gk, w_gv, w_dq, w_dk, w_dv = (w_in[:, a:b] for a, b in zip([0] + cuts, cuts + [w_in.shape[1]]))
    cos1, sin1 = _rope_tables(n_lat, n_ctx, HEAD_DIM)
    qscale = HEAD_DIM ** -0.5 * LOG2E
    gq_gain = qk_norm_g[0, 0].astype(F32)
    gk_gain = qk_norm_g[0, 1].astype(F32)
    quarter = HEAD_DIM // 4

    def partner_gain(g):
        return jnp.concatenate([g[quarter:2 * quarter], g[0:quarter], g[3 * quarter:], g[2 * quarter:3 * quarter]])

    cos_q = jnp.stack([cos1[:n_lat] * gq_gain[None, :], cos1[:n_lat]]) * qscale
    sin_q = jnp.stack([sin1[:n_lat] * partner_gain(gq_gain)[None, :], sin1[:n_lat]]) * qscale
    cos_k = jnp.stack([cos1.T * gk_gain[:, None], cos1.T])
    sin_k = jnp.stack([sin1.T * partner_gain(gk_gain)[:, None], sin1.T])
    q1 = _projq(x_all, mods[1], jnp.concatenate([w_gq, w_dq], axis=1).astype(BF16), cos_q, sin_q, n_lat)
    v1, kt = _projkv(x_all, mods[1], jnp.concatenate([w_gv, w_dv], axis=1).astype(BF16),
                     jnp.concatenate([w_gk, w_dk], axis=1).T.astype(BF16), cos_k, sin_k, n_lat)
    lf = diff_lambda[0].astype(F32)
    lam = (jnp.exp(jnp.sum(lf[0] * lf[1])) - jnp.exp(jnp.sum(lf[2] * lf[3])) + lambda_init).reshape(1)
    og, od = _attention(q1, kt, v1, lam, diff_subln_g[0], lambda_init, n_lat)
    ln1 = jnp.stack([ln_g[1], ln_b[1]], axis=1)
    x_lat, xp, logits = _finish1(og, od, x_all, mods[1], ln1[0], od_w_out[0].astype(BF16),
                                 _router_operand(router_w[1]), n_lat)
    x_lat = _moe(xp, logits, x_lat, mods[1], ln1[1], router_bias[1], exp_w_gate, exp_w_up, exp_w_down, 1,
                 sh_w_gate[1], sh_w_up[1], sh_w_down[1], n_lat)
    return x_lat[None]
```
